```python
import math
import jax, jax.numpy as jnp
from jax import lax
import numpy as np

D_MODEL = 2048
BATCH = 2
SEQ = 8192
DEPTH = 1

DIFF_HEADS = 8
DIFF_D = 64
NSA_HEADS = 8
NSA_KV = 2
NSA_GROUP = NSA_HEADS // NSA_KV
NSA_DK = 128
NSA_DV = 128
CMP_LEN = 32
CMP_STRIDE = 16
CMP_HIDDEN = 256
SLC_LEN = 64
SLC_TOPK = 16
WINDOW = 512
N_BUCKETS = 32
MAX_DISTANCE = 128
D_FF = 5632
CONV_W = 3
Q_BLOCK = 128
EPS = 1e-6
NEG = -1e30
FORCE = 1e30
TINY = 1e-30

DIFF_QK = DIFF_HEADS * 2 * DIFF_D
DIFF_V = DIFF_HEADS * 2 * DIFF_D
NSA_Q = NSA_HEADS * NSA_DK
NSA_K = NSA_KV * NSA_DK
NSA_V = NSA_KV * NSA_DV
N_GATES = NSA_HEADS * 3
SPLIT_SIZES = (DIFF_QK, DIFF_QK, DIFF_V, NSA_Q, NSA_K, NSA_V, NSA_K, NSA_V, NSA_K, NSA_V, N_GATES)
IN_COLS = 2 * DIFF_QK + DIFF_V + NSA_Q + 3 * (NSA_K + NSA_V) + N_GATES
MIX_WIDTH = DIFF_HEADS * 2 * DIFF_D + NSA_HEADS * NSA_DV

kernel_name = 'hybrid_diffattn_nsa_convglu_block'


def rms_norm(x, g):
    xf = x.astype(jnp.float32)
    y = xf * lax.rsqrt(jnp.mean(xf * xf, axis=-1, keepdims=True) + EPS)
    return (y * g.astype(jnp.float32)).astype(x.dtype)


def rel_bucket(delta):
    n = jnp.maximum(delta, 0)
    max_exact = N_BUCKETS // 2
    nf = jnp.maximum(n, 1).astype(jnp.float32)
    large = max_exact + (jnp.log(nf / max_exact) / math.log(MAX_DISTANCE / max_exact)
                         * (N_BUCKETS - max_exact)).astype(jnp.int32)
    large = jnp.minimum(large, N_BUCKETS - 1)
    return jnp.where(n < max_exact, n, large)


def masked_softmax(s, mask):
    s = jnp.where(mask, s, NEG)
    m = jnp.max(s, axis=-1, keepdims=True)
    p = jnp.exp(s - m) * mask
    return p / jnp.maximum(jnp.sum(p, axis=-1, keepdims=True), TINY)


def selection_map(nb, nc):
    r = SLC_LEN // CMP_STRIDE
    j = np.arange(nb)[:, None, None]
    m = np.arange(r)[None, :, None]
    n = np.arange(CMP_LEN // CMP_STRIDE)[None, None, :]
    c = r * j - m - n
    jj = np.broadcast_to(j, c.shape)
    valid = (c >= 0) & (c < nc)
    out = np.zeros((nb, nc), np.float32)
    np.add.at(out, (jj[valid], c[valid]), 1.0)
    return out


def compress(k, pos_emb, w1, w2):
    B, S, G, d = k.shape
    nc = (S - CMP_LEN) // CMP_STRIDE + 1
    idx = jnp.arange(nc)[:, None] * CMP_STRIDE + jnp.arange(CMP_LEN)[None, :]
    blocks = k[:, idx] + pos_emb[:, None, :]
    blocks = jnp.transpose(blocks, (0, 3, 1, 2, 4)).reshape(B, G, nc, CMP_LEN * d)
    return jax.nn.gelu(blocks @ w1, approximate=True) @ w2


def diff_attention(q, k, v, lam, lam_init, subln, bias_tab):
    B, H, S, _, d = q.shape
    nblk = S // Q_BLOCK
    kpos = jnp.arange(S)
    scale = d ** -0.5

    def block(i):
        s0 = i * Q_BLOCK
        qb = lax.dynamic_slice_in_dim(q, s0, Q_BLOCK, axis=2)
        qpos = s0 + jnp.arange(Q_BLOCK)
        delta = qpos[:, None] - kpos[None, :]
        bias = jnp.transpose(bias_tab[rel_bucket(delta)], (2, 0, 1))
        logits = jnp.einsum('bhqmd,bhkmd->bhmqk', qb, k).astype(jnp.float32) * scale + bias[None, :, None]
        p = masked_softmax(logits, (delta >= 0)[None, None, None])
        a = p[:, :, 0] - lam * p[:, :, 1]
        return jnp.einsum('bhqk,bhke->bhqe', a.astype(v.dtype), v)

    o = lax.map(block, jnp.arange(nblk))
    o = jnp.transpose(o, (1, 2, 0, 3, 4)).reshape(B, H, S, 2 * d)
    return rms_norm(o, subln) * (1.0 - lam_init)


def nsa_attention(q, k_cmp, v_cmp, k_slc, v_slc, k_win, v_win, gates, bias_nsa):
    B, G, Hg, S, dk = q.shape
    nc = k_cmp.shape[2]
    nb = S // SLC_LEN
    topk = min(SLC_TOPK, nb)
    nsel = topk * SLC_LEN
    scale = dk ** -0.5
    sel_map = jnp.asarray(selection_map(nb, nc))
    cmp_end = jnp.arange(nc) * CMP_STRIDE + (CMP_LEN - 1)
    ks_blk = k_slc.reshape(B, G, nb, SLC_LEN, dk)
    vs_blk = v_slc.reshape(B, G, nb, SLC_LEN, -1)
    pad = ((0, 0), (0, 0), (WINDOW, 0), (0, 0))
    kw_pad = jnp.pad(k_win, pad)
    vw_pad = jnp.pad(v_win, pad)
    table_g = jnp.transpose(bias_nsa, (1, 0, 2))
    bi = jnp.arange(B)[:, None, None, None]
    gi = jnp.arange(G)[None, :, None, None]
    blk = jnp.arange(nb)

    def block(i):
        s0 = i * Q_BLOCK
        qb = lax.dynamic_slice_in_dim(q, s0, Q_BLOCK, axis=3)
        gb = lax.dynamic_slice_in_dim(gates, s0, Q_BLOCK, axis=3)
        qpos = s0 + jnp.arange(Q_BLOCK)
        lc = jnp.einsum('bghqd,bgcd->bghqc', qb, k_cmp).astype(jnp.float32) * scale
        pc = masked_softmax(lc, cmp_end[None, :] <= qpos[:, None])
        o_cmp = jnp.einsum('bghqc,bgcd->bghqd', pc.astype(v_cmp.dtype), v_cmp)
        imp = jnp.einsum('bgqc,jc->bgqj', jnp.sum(pc, axis=2), sel_map)
        cur = (qpos // SLC_LEN)[:, None]
        forced = (blk[None] == 0) | (blk[None] == cur) | (blk[None] == cur - 1)
        imp = jnp.where(forced, FORCE, jnp.where(blk[None] <= cur, imp, NEG))
        _, idx = lax.top_k(imp, topk)
        k_sel = ks_blk[bi, gi, idx].reshape(B, G, Q_BLOCK, nsel, dk)
        v_sel = vs_blk[bi, gi, idx].reshape(B, G, Q_BLOCK, nsel, -1)
        kpos_sel = (idx[..., None] * SLC_LEN + jnp.arange(SLC_LEN)).reshape(B, G, Q_BLOCK, nsel)
        d_sel = qpos[:, None] - kpos_sel
        b_sel = jnp.transpose(table_g[gi, rel_bucket(d_sel)], (0, 1, 4, 2, 3))
        ls = jnp.einsum('bghqd,bgqkd->bghqk', qb, k_sel).astype(jnp.float32) * scale + b_sel
        ps = masked_softmax(ls, (d_sel >= 0)[:, :, None])
        o_slc = jnp.einsum('bghqk,bgqkd->bghqd', ps.astype(v_sel.dtype), v_sel)
        kwb = lax.dynamic_slice_in_dim(kw_pad, s0, Q_BLOCK + WINDOW, axis=2)
        vwb = lax.dynamic_slice_in_dim(vw_pad, s0, Q_BLOCK + WINDOW, axis=2)
        kpos_w = s0 - WINDOW + jnp.arange(Q_BLOCK + WINDOW)
        d_w = qpos[:, None] - kpos_w[None, :]
        wmask = (d_w >= 0) & (d_w < WINDOW) & (kpos_w[None, :] >= 0)
        b_w = jnp.transpose(bias_nsa[rel_bucket(d_w)], (2, 3, 0, 1))
        lw = jnp.einsum('bghqd,bgkd->bghqk', qb, kwb).astype(jnp.float32) * scale + b_w
        pw = masked_softmax(lw, wmask)
        o_win = jnp.einsum('bghqk,bgkd->bghqd', pw.astype(vwb.dtype), vwb)
        return gb[..., 0:1] * o_cmp + gb[..., 1:2] * o_slc + gb[..., 2:3] * o_win

    o = lax.map(block, jnp.arange(S // Q_BLOCK))
    return jnp.transpose(o, (1, 2, 3, 0, 4, 5)).reshape(B, G, Hg, S, -1)


def causal_dwconv(u, w, b):
    S = u.shape[1]
    up = jnp.pad(u, ((0, 0), (CONV_W - 1, 0), (0, 0)))
    y = b
    for j in range(CONV_W):
        y = y + w[j] * up[:, j:j + S]
    return y


def setup_inputs(seed: int = 0) -> dict:
    key = jax.random.key(seed)
    ks = jax.random.split(key, 24)
    f32 = jnp.float32

    def nrm(k, shape, scale):
        return jax.random.normal(k, shape, f32) * scale

    def gain(k, shape):
        return 1.0 + 0.05 * jax.random.normal(k, shape, f32)

    return {
        'x': nrm(ks[0], (BATCH, SEQ, D_MODEL), 1.0),
        'pre_mix_norm': gain(ks[1], (DEPTH, D_MODEL)),
        'w_in': nrm(ks[2], (DEPTH, D_MODEL, IN_COLS), D_MODEL ** -0.5),
        'lambda_q1': nrm(ks[3], (DEPTH, DIFF_D), 0.1),
        'lambda_k1': nrm(ks[4], (DEPTH, DIFF_D), 0.1),
        'lambda_q2': nrm(ks[5], (DEPTH, DIFF_D), 0.1),
        'lambda_k2': nrm(ks[6], (DEPTH, DIFF_D), 0.1),
        'diff_subln': gain(ks[7], (DEPTH, 2 * DIFF_D)),
        'cmp_pos_k': nrm(ks[8], (DEPTH, CMP_LEN, NSA_DK), 0.1),
        'cmp_pos_v': nrm(ks[9], (DEPTH, CMP_LEN, NSA_DV), 0.1),
        'cmp_k_w1': nrm(ks[10], (DEPTH, CMP_LEN * NSA_DK, CMP_HIDDEN), (CMP_LEN * NSA_DK) ** -0.5),
        'cmp_k_w2': nrm(ks[11], (DEPTH, CMP_HIDDEN, NSA_DK), CMP_HIDDEN ** -0.5),
        'cmp_v_w1': nrm(ks[12], (DEPTH, CMP_LEN * NSA_DV, CMP_HIDDEN), (CMP_LEN * NSA_DV) ** -0.5),
        'cmp_v_w2': nrm(ks[13], (DEPTH, CMP_HIDDEN, NSA_DV), CMP_HIDDEN ** -0.5),
        'rel_bias': nrm(ks[14], (N_BUCKETS, DIFF_HEADS + NSA_HEADS), 0.5),
        'w_out': nrm(ks[15], (DEPTH, MIX_WIDTH, D_MODEL), MIX_WIDTH ** -0.5),
        'post_mix_norm': gain(ks[16], (DEPTH, D_MODEL)),
        'pre_ffn_norm': gain(ks[17], (DEPTH, D_MODEL)),
        'w_up': nrm(ks[18], (DEPTH, D_MODEL, 2 * D_FF), D_MODEL ** -0.5),
        'conv_w': nrm(ks[19], (DEPTH, CONV_W, 2 * D_FF), CONV_W ** -0.5),
        'conv_b': nrm(ks[20], (DEPTH, 2 * D_FF), 0.01),
        'w_down': nrm(ks[21], (DEPTH, D_FF, D_MODEL), D_FF ** -0.5),
        'post_ffn_norm': gain(ks[22], (DEPTH, D_MODEL)),
    }


def reference(x, pre_mix_norm, w_in, lambda_q1, lambda_k1, lambda_q2, lambda_k2, diff_subln,
              cmp_pos_k, cmp_pos_v, cmp_k_w1, cmp_k_w2, cmp_v_w1, cmp_v_w2, rel_bias, w_out,
              post_mix_norm, pre_ffn_norm, w_up, conv_w, conv_b, w_down, post_ffn_norm):
    B, S, _ = x.shape
    f32 = jnp.float32
    offsets = np.cumsum(SPLIT_SIZES)[:-1].tolist()
    bias_diff = rel_bias[:, :DIFF_HEADS]
    bias_nsa = rel_bias[:, DIFF_HEADS:].reshape(N_BUCKETS, NSA_KV, NSA_GROUP)
    for l in range(DEPTH):
        h = rms_norm(x, pre_mix_norm[l])
        proj = h @ w_in[l]
        (dq, dk, dv, nq, kc, vc, ksl, vsl, kwn, vwn, g) = jnp.split(proj, offsets, axis=-1)
        q_d = dq.reshape(B, S, DIFF_HEADS, 2, DIFF_D).transpose(0, 2, 1, 3, 4)
        k_d = dk.reshape(B, S, DIFF_HEADS, 2, DIFF_D).transpose(0, 2, 1, 3, 4)
        v_d = dv.reshape(B, S, DIFF_HEADS, 2 * DIFF_D).transpose(0, 2, 1, 3)
        lam_init = 0.8 - 0.6 * math.exp(-0.3 * l)
        lam = (jnp.exp(jnp.sum(lambda_q1[l].astype(f32) * lambda_k1[l].astype(f32)))
               - jnp.exp(jnp.sum(lambda_q2[l].astype(f32) * lambda_k2[l].astype(f32))) + lam_init)
        o_diff = diff_attention(q_d, k_d, v_d, lam, lam_init, diff_subln[l], bias_diff)
        q_n = nq.reshape(B, S, NSA_KV, NSA_GROUP, NSA_DK).transpose(0, 2, 3, 1, 4)
        k_c = compress(kc.reshape(B, S, NSA_KV, NSA_DK), cmp_pos_k[l], cmp_k_w1[l], cmp_k_w2[l])
        v_c = compress(vc.reshape(B, S, NSA_KV, NSA_DV), cmp_pos_v[l], cmp_v_w1[l], cmp_v_w2[l])
        k_s = ksl.reshape(B, S, NSA_KV, NSA_DK).transpose(0, 2, 1, 3)
        v_s = vsl.reshape(B, S, NSA_KV, NSA_DV).transpose(0, 2, 1, 3)
        k_w = kwn.reshape(B, S, NSA_KV, NSA_DK).transpose(0, 2, 1, 3)
        v_w = vwn.reshape(B, S, NSA_KV, NSA_DV).transpose(0, 2, 1, 3)
        gates = jax.nn.sigmoid(g.reshape(B, S, NSA_KV, NSA_GROUP, 3)).transpose(0, 2, 3, 1, 4)
        o_nsa = nsa_attention(q_n, k_c, v_c, k_s, v_s, k_w, v_w, gates, bias_nsa)
        mix = jnp.concatenate([o_diff.transpose(0, 2, 1, 3).reshape(B, S, -1),
                               o_nsa.transpose(0, 3, 1, 2, 4).reshape(B, S, -1)], axis=-1)
        x = x + rms_norm(mix @ w_out[l], post_mix_norm[l])
        h = rms_norm(x, pre_ffn_norm[l])
        u = causal_dwconv(h @ w_up[l], conv_w[l], conv_b[l])
        gate, up = jnp.split(u, 2, axis=-1)
        y = (jax.nn.gelu(gate, approximate=True) * up) @ w_down[l]
        x = x + rms_norm(y, post_ffn_norm[l])
    return x
```

```python
import functools
import math

import jax
import jax.numpy as jnp
from jax import lax
from jax.experimental import pallas as pl
from jax.experimental.pallas import tpu as pltpu

F32 = jnp.float32
BF16 = jnp.bfloat16

D_MODEL = 2048
DIFF_HEADS = 8
DIFF_D = 64
NSA_HEADS = 8
NSA_KV = 2
NSA_GROUP = NSA_HEADS // NSA_KV
NSA_DK = 128
NSA_DV = 128
CMP_LEN = 32
CMP_STRIDE = 16
CMP_HIDDEN = 256
SLC_LEN = 64
SLC_TOPK = 16
WINDOW = 512
N_BUCKETS = 32
MAX_DISTANCE = 128
D_FF = 5632
CONV_W = 3
EPS = 1e-6
NEG = -1e30
FORCE = 1e30
TINY = 1e-30
LOG2E = math.log2(math.e)

LANES = 128
SUBLANES = 8
VMEM_LIMIT = 56 * 1024 * 1024

MAIN_COLS = 5632
N_GATES = NSA_HEADS * 3
COL_DQ, COL_DK, COL_DV, COL_NQ = 0, 1024, 2048, 3072
COL_KC, COL_VC, COL_KS, COL_VS, COL_KW, COL_VW = 4096, 4352, 4608, 4864, 5120, 5376

DIFF_TQ = 512
NSA_TQ = 128
NSA_TK = 512
NSA_WK = WINDOW + NSA_TQ
TS_OFF = 896
TS_W = TS_OFF + NSA_TK
TW_W = WINDOW + NSA_WK


def _cparams(sem):
    return pltpu.CompilerParams(dimension_semantics=sem, vmem_limit_bytes=VMEM_LIMIT)


def _rms(x, g):
    return x * lax.rsqrt(jnp.mean(x * x, axis=-1, keepdims=True) + EPS) * g


def _gelu_tanh(x):
    return 0.5 * x * (1.0 + jnp.tanh(math.sqrt(2.0 / math.pi) * (x + 0.044715 * (x * x * x))))


def _dot(a, b):
    return jnp.dot(a, b, preferred_element_type=F32)


def _dot_nt(a, b, precision=None):
    return lax.dot_general(a, b, (((1,), (1,)), ((), ())), preferred_element_type=F32,
                           precision=precision)


def _lane_tile(a, n):
    return a if n == 1 else jnp.concatenate([a] * n, axis=1)


def _inproj_kernel(x_ref, g_ref, w_ref, wg_ref, proj_ref, gates_ref, h_ref):
    @pl.when(pl.program_id(1) == 0)
    def _():
        h_ref[...] = _rms(x_ref[...], g_ref[...]).astype(BF16)
        gl = _dot(h_ref[...], wg_ref[...])
        gates_ref[...] = 1.0 / (1.0 + jnp.exp(-gl))

    proj_ref[...] = _dot(h_ref[...], w_ref[...]).astype(BF16)


def _inproj(xf, g, w_main, w_gate, tm, tn):
    m = xf.shape[0]
    return pl.pallas_call(
        _inproj_kernel,
        grid=(m // tm, MAIN_COLS // tn),
        in_specs=[
            pl.BlockSpec((tm, D_MODEL), lambda i, j: (i, 0)),
            pl.BlockSpec((1, D_MODEL), lambda i, j: (0, 0)),
            pl.BlockSpec((D_MODEL, tn), lambda i, j: (0, j)),
            pl.BlockSpec((D_MODEL, LANES), lambda i, j: (0, 0)),
        ],
        out_specs=[
            pl.BlockSpec((tm, tn), lambda i, j: (i, j)),
            pl.BlockSpec((tm, LANES), lambda i, j: (i, 0)),
        ],
        out_shape=[
            jax.ShapeDtypeStruct((m, MAIN_COLS), BF16),
            jax.ShapeDtypeStruct((m, LANES), F32),
        ],
        scratch_shapes=[pltpu.VMEM((tm, D_MODEL), BF16)],
        compiler_params=_cparams(("arbitrary", "arbitrary")),
        name="inproj",
    )(xf, g, w_main, w_gate)


def _diff_kernel(q_ref, k_ref, v_ref, t_ref, lq1_ref, lk1_ref, lq2_ref, lk2_ref, sub_ref,
                 o_ref, qs_ref, m_ref, l_ref, acc_ref, *, lam_init):
    tq = DIFF_TQ
    i = pl.program_id(2)
    q = q_ref[0].astype(F32) * (DIFF_D ** -0.5 * LOG2E)
    lane = lax.broadcasted_iota(jnp.int32, q.shape, 1)
    qs_ref[...] = jnp.concatenate(
        [jnp.where(lane < DIFF_D, q, 0.0), jnp.where(lane >= DIFF_D, q, 0.0)], axis=0).astype(BF16)
    m_ref[...] = jnp.full(m_ref.shape, NEG, F32)
    l_ref[...] = jnp.zeros(l_ref.shape, F32)
    acc_ref[...] = jnp.zeros(acc_ref.shape, F32)

    def step(kstart, bias):
        k = k_ref[0, pl.ds(kstart, tq), :]
        v = v_ref[0, pl.ds(kstart, tq), :]
        s = _dot_nt(qs_ref[...], k)
        if bias is not None:
            s = s + jnp.concatenate([bias, bias], axis=0)
        m_prev = m_ref[...]
        m_new = jnp.maximum(m_prev, jnp.max(s, axis=1, keepdims=True))
        alpha = jnp.exp2(m_prev - m_new)
        p = jnp.exp2(s - _lane_tile(m_new, tq // LANES))
        l_ref[...] = alpha * l_ref[...] + jnp.sum(p, axis=1, keepdims=True)
        acc_ref[...] = alpha * acc_ref[...] + _dot(p.astype(BF16), v)
        m_ref[...] = m_new

    def far_body(j, carry):
        step(pl.multiple_of(j * tq, tq), None)
        return carry

    lax.fori_loop(0, jnp.maximum(i - 1, 0), far_body, 0)

    @pl.when(i > 0)
    def _():
        step(pl.multiple_of((i - 1) * tq, tq), t_ref[0, :, 0:tq])

    step(pl.multiple_of(i * tq, tq), t_ref[0, :, tq:2 * tq])

    o = acc_ref[...] / jnp.maximum(l_ref[...], TINY)
    lam = (jnp.exp(jnp.sum(lq1_ref[...] * lk1_ref[...], axis=1, keepdims=True))
           - jnp.exp(jnp.sum(lq2_ref[...] * lk2_ref[...], axis=1, keepdims=True)) + lam_init)
    a = o[0:tq] - lam * o[tq:2 * tq]
    o_ref[0] = (_rms(a, sub_ref[...]) * (1.0 - lam_init)).astype(BF16)


def _diff_attention(proj3, t_diff, lq1, lk1, lq2, lk2, subln, lam_init):
    b, s, _ = proj3.shape
    tq = DIFF_TQ
    vec = lambda n: pl.BlockSpec((1, n), lambda bi, h, i: (0, 0))
    return pl.pallas_call(
        functools.partial(_diff_kernel, lam_init=lam_init),
        grid=(b, DIFF_HEADS, s // tq),
        in_specs=[
            pl.BlockSpec((1, tq, LANES), lambda bi, h, i: (bi, i, COL_DQ // LANES + h)),
            pl.BlockSpec((1, s, LANES), lambda bi, h, i: (bi, 0, COL_DK // LANES + h)),
            pl.BlockSpec((1, s, LANES), lambda bi, h, i: (bi, 0, COL_DV // LANES + h)),
            pl.BlockSpec((1, tq, 2 * tq), lambda bi, h, i: (h, 0, 0)),
            vec(DIFF_D), vec(DIFF_D), vec(DIFF_D), vec(DIFF_D), vec(2 * DIFF_D),
        ],
        out_specs=pl.BlockSpec((1, tq, LANES), lambda bi, h, i: (bi, i, h)),
        out_shape=jax.ShapeDtypeStruct((b, s, DIFF_HEADS * 2 * DIFF_D), BF16),
        scratch_shapes=[
            pltpu.VMEM((2 * tq, LANES), BF16),
            pltpu.VMEM((2 * tq, LANES), F32),
            pltpu.VMEM((2 * tq, LANES), F32),
            pltpu.VMEM((2 * tq, LANES), F32),
        ],
        compiler_params=_cparams(("arbitrary", "arbitrary", "arbitrary")),
        name="diff_attn",
    )(proj3, proj3, proj3, t_diff, lq1, lk1, lq2, lk2, subln)


def _compress_kernel(r_ref, pos_ref, w1_ref, w2_ref, o_ref):
    half = CMP_STRIDE * NSA_DK
    r = r_ref[0, 0, 0]
    n = r.shape[0]
    a = _dot(r, w1_ref[0, 0:half, :])
    bm = _dot(r, w1_ref[0, half:2 * half, :])
    posb = jnp.broadcast_to(pos_ref[0], (SUBLANES, 2 * half)).astype(BF16)
    pt = _dot(posb, w1_ref[0])[0:1]
    hid = a + pltpu.roll(bm, n - 1, axis=0) + pt
    o_ref[0, 0, 0] = _dot(_gelu_tanh(hid).astype(BF16), w2_ref[0]).astype(BF16)


def _compress(r, pos, w1, w2):
    _, b, g, n, width = r.shape
    return pl.pallas_call(
        _compress_kernel,
        grid=(2, b, g),
        in_specs=[
            pl.BlockSpec((1, 1, 1, n, width), lambda t, bi, gi: (t, bi, gi, 0, 0)),
            pl.BlockSpec((1, 1, CMP_LEN * NSA_DK), lambda t, bi, gi: (t, 0, 0)),
            pl.BlockSpec((1, CMP_LEN * NSA_DK, CMP_HIDDEN), lambda t, bi, gi: (t, 0, 0)),
            pl.BlockSpec((1, CMP_HIDDEN, NSA_DK), lambda t, bi, gi: (t, 0, 0)),
        ],
        out_specs=pl.BlockSpec((1, 1, 1, n, NSA_DK), lambda t, bi, gi: (t, bi, gi, 0, 0)),
        out_shape=jax.ShapeDtypeStruct((2, b, g, n, NSA_DK), BF16),
        compiler_params=_cparams(("arbitrary", "arbitrary", "arbitrary")),
        name="nsa_compress",
    )(r, pos, w1, w2)


def _nsa_kernel(q_ref, kc_ref, vc_ref, ks_ref, vs_ref, kw_ref, vw_ref, ts_ref, tw_ref, gate_ref,
                selmap_ref, o_ref, qx_ref, m_ref, l_ref, acc_ref):
    tq, tk, hg = NSA_TQ, NSA_TK, NSA_GROUP
    rows = hg * tq
    i = pl.program_id(2)
    q0 = i * tq
    qt = q_ref[0]
    q4 = jnp.concatenate([qt[:, h * NSA_DK:(h + 1) * NSA_DK] for h in range(hg)], axis=0)
    q4 = (q4.astype(F32) * (NSA_DK ** -0.5 * LOG2E)).astype(BF16)

    kc = kc_ref[0, 0, 0]
    nc = kc.shape[0]
    lc = _dot_nt(q4, kc)
    qpos = q0 + lax.rem(lax.broadcasted_iota(jnp.int32, (rows, nc), 0), tq)
    cend = lax.broadcasted_iota(jnp.int32, (rows, nc), 1) * CMP_STRIDE + (CMP_LEN - 1)
    vis = cend <= qpos
    sc = jnp.where(vis, lc, NEG)
    pc = jnp.where(vis, jnp.exp2(sc - jnp.max(sc, axis=1, keepdims=True)), 0.0)
    pc = pc / jnp.maximum(jnp.sum(pc, axis=1, keepdims=True), TINY)
    o_cmp = _dot(pc.astype(BF16), vc_ref[0, 0, 0])

    pcsum = pc[0:tq] + pc[tq:2 * tq] + pc[2 * tq:3 * tq] + pc[3 * tq:4 * tq]
    imp_t = _dot_nt(selmap_ref[...], pcsum, precision=lax.Precision.HIGHEST)
    nb = imp_t.shape[0]
    blk = lax.broadcasted_iota(jnp.int32, (nb, tq), 0)
    cur = lax.shift_right_logical(q0 + lax.broadcasted_iota(jnp.int32, (nb, tq), 1),
                                  int(math.log2(SLC_LEN)))
    forced = (blk == 0) | (blk == cur) | (blk == cur - 1)
    val = jnp.where(forced, FORCE, jnp.where(blk <= cur, imp_t, NEG))
    sel = jnp.zeros((nb, tq), F32)
    for _ in range(min(SLC_TOPK, nb)):
        best = jnp.max(val, axis=0, keepdims=True)
        first = jnp.min(jnp.where(val == best, blk, nb), axis=0, keepdims=True)
        hit = blk == first
        sel = jnp.where(hit, 1.0, sel)
        val = jnp.where(hit, -jnp.inf, val)
    maskbias = jnp.where(sel.T > 0.5, 0.0, NEG).astype(BF16)
    qx_ref[:, 0:NSA_DK] = q4
    qx_ref[:, NSA_DK:NSA_DK + nb] = jnp.concatenate([maskbias] * hg, axis=0)

    m_ref[...] = jnp.full(m_ref.shape, NEG, F32)
    l_ref[...] = jnp.zeros(l_ref.shape, F32)
    acc_ref[...] = jnp.zeros(acc_ref.shape, F32)

    def step(j, near):
        kstart = pl.multiple_of(j * tk, tk)
        s = _dot_nt(qx_ref[...], ks_ref[0, 0, pl.ds(kstart, tk), :])
        if near:
            u0 = pl.multiple_of(TS_OFF - (q0 - j * tk), LANES)
            s = s + ts_ref[0, :, pl.ds(u0, tk)]
        m_prev = m_ref[...]
        m_new = jnp.maximum(m_prev, jnp.max(s, axis=1, keepdims=True))
        alpha = jnp.exp2(m_prev - m_new)
        p = jnp.exp2(s - _lane_tile(m_new, tk // LANES))
        l_ref[...] = alpha * l_ref[...] + jnp.sum(p, axis=1, keepdims=True)
        acc_ref[...] = alpha * acc_ref[...] + _dot(p.astype(BF16), vs_ref[0, pl.ds(kstart, tk), :])
        m_ref[...] = m_new

    jl = (q0 + tq - 1) // tk

    def far_body(j, carry):
        step(j, False)
        return carry

    lax.fori_loop(0, jnp.maximum(jl - 1, 0), far_body, 0)

    @pl.when(jl > 0)
    def _():
        step(jl - 1, True)

    step(jl, True)
    o_slc = acc_ref[...] / jnp.maximum(l_ref[...], TINY)

    kst = pl.multiple_of(jnp.maximum(q0 - WINDOW, 0), tq)
    w0 = pl.multiple_of(WINDOW - (q0 - kst), LANES)
    lw = _dot_nt(q4, kw_ref[0, pl.ds(kst, NSA_WK), :]) + tw_ref[0, :, pl.ds(w0, NSA_WK)]
    pw = jnp.exp2(lw - jnp.max(lw, axis=1, keepdims=True))
    o_win = _dot(pw.astype(BF16), vw_ref[0, pl.ds(kst, NSA_WK), :])
    o_win = o_win / jnp.maximum(jnp.sum(pw, axis=1, keepdims=True), TINY)

    gt = gate_ref[0, 0]
    for h in range(hg):
        r0 = h * tq
        o = (gt[:, 3 * h:3 * h + 1] * o_cmp[r0:r0 + tq]
             + gt[:, 3 * h + 1:3 * h + 2] * o_slc[r0:r0 + tq]
             + gt[:, 3 * h + 2:3 * h + 3] * o_win[r0:r0 + tq])
        o_ref[0, :, h * NSA_DV:(h + 1) * NSA_DV] = o.astype(BF16)


def _nsa_attention(proj3, kvc, ksx, t_sel, t_win, gates_g, selmap):
    b, s, _ = proj3.shape
    tq, hg = NSA_TQ, NSA_GROUP
    nc = kvc.shape[3]
    nb = s // SLC_LEN
    qcols = hg * NSA_DK
    return pl.pallas_call(
        _nsa_kernel,
        grid=(b, NSA_KV, s // tq),
        in_specs=[
            pl.BlockSpec((1, tq, qcols), lambda bi, g, i: (bi, i, COL_NQ // qcols + g)),
            pl.BlockSpec((1, 1, 1, nc, NSA_DK), lambda bi, g, i: (0, bi, g, 0, 0)),
            pl.BlockSpec((1, 1, 1, nc, NSA_DV), lambda bi, g, i: (1, bi, g, 0, 0)),
            pl.BlockSpec((1, 1, s, NSA_DK + nb), lambda bi, g, i: (bi, g, 0, 0)),
            pl.BlockSpec((1, s, NSA_DV), lambda bi, g, i: (bi, 0, COL_VS // NSA_DV + g)),
            pl.BlockSpec((1, s, NSA_DK), lambda bi, g, i: (bi, 0, COL_KW // NSA_DK + g)),
            pl.BlockSpec((1, s, NSA_DV), lambda bi, g, i: (bi, 0, COL_VW // NSA_DV + g)),
            pl.BlockSpec((1, hg * tq, TS_W), lambda bi, g, i: (g, 0, 0)),
            pl.BlockSpec((1, hg * tq, TW_W), lambda bi, g, i: (g, 0, 0)),
            pl.BlockSpec((1, 1, tq, 3 * hg), lambda bi, g, i: (bi, g, i, 0)),
            pl.BlockSpec((nb, nc), lambda bi, g, i: (0, 0)),
        ],
        out_specs=pl.BlockSpec((1, tq, hg * NSA_DV), lambda bi, g, i: (bi, i, g)),
        out_shape=jax.ShapeDtypeStruct((b, s, NSA_HEADS * NSA_DV), BF16),
        scratch_shapes=[
            pltpu.VMEM((hg * tq, NSA_DK + nb), BF16),
            pltpu.VMEM((hg * tq, LANES), F32),
            pltpu.VMEM((hg * tq, LANES), F32),
            pltpu.VMEM((hg * tq, NSA_DV), F32),
        ],
        compiler_params=_cparams(("arbitrary", "arbitrary", "arbitrary")),
        name="nsa_attn",
    )(proj3, kvc, kvc, ksx, proj3, proj3, proj3, t_sel, t_win, gates_g, selmap)


def _outproj_kernel(od_ref, on_ref, x_ref, wd_ref, wn_ref, g_ref, o_ref):
    y = _dot(od_ref[...], wd_ref[...]) + _dot(on_ref[...], wn_ref[...])
    o_ref[...] = x_ref[...] + _rms(y, g_ref[...])


def _outproj(od, on, xf, w_d, w_n, g, tm):
    m = xf.shape[0]
    kd, kn = od.shape[1], on.shape[1]
    return pl.pallas_call(
        _outproj_kernel,
        grid=(m // tm,),
        in_specs=[
            pl.BlockSpec((tm, kd), lambda i: (i, 0)),
            pl.BlockSpec((tm, kn), lambda i: (i, 0)),
            pl.BlockSpec((tm, D_MODEL), lambda i: (i, 0)),
            pl.BlockSpec((kd, D_MODEL), lambda i: (0, 0)),
            pl.BlockSpec((kn, D_MODEL), lambda i: (0, 0)),
            pl.BlockSpec((1, D_MODEL), lambda i: (0, 0)),
        ],
        out_specs=pl.BlockSpec((tm, D_MODEL), lambda i: (i, 0)),
        out_shape=jax.ShapeDtypeStruct((m, D_MODEL), F32),
        compiler_params=_cparams(("arbitrary",)),
        name="outproj",
    )(od, on, xf, w_d, w_n, g)


def _ffn_kernel(x_ref, gpre_ref, wg_ref, wu_ref, cwg_ref, cwu_ref, cbg_ref, cbu_ref, wd_ref,
                gpost_ref, o_ref, h_ref, acc_ref, carry_ref, yg_ref, yu_ref, *, tiles_per_seq):
    i = pl.program_id(0)
    j = pl.program_id(1)
    tm = x_ref.shape[0]
    halo = SUBLANES

    @pl.when(j == 0)
    def _():
        h_ref[...] = _rms(x_ref[...], gpre_ref[...]).astype(BF16)

    @pl.when(lax.rem(i, tiles_per_seq) == 0)
    def _():
        carry_ref[j] = jnp.zeros(carry_ref.shape[1:], F32)

    def conv(w_ref, cw_ref, cb_ref, slot, y_ref):
        u = _dot(h_ref[...], w_ref[...])
        prev = carry_ref[j, slot]
        carry_ref[j, slot] = u[tm - halo:tm]
        w0, w1, w2 = cw_ref[0:1, :], cw_ref[1:2, :], cw_ref[2:3, :]
        bias = cb_ref[...]

        def taps(z):
            n = z.shape[0]
            return bias + w2 * z + w1 * pltpu.roll(z, 1, axis=0) + w0 * pltpu.roll(z, 2, axis=0)

        y_ref[...] = taps(u)
        y_ref[0:halo, :] = taps(jnp.concatenate([prev, u[0:halo]], axis=0))[halo:2 * halo]

    conv(wg_ref, cwg_ref, cbg_ref, 0, yg_ref)
    conv(wu_ref, cwu_ref, cbu_ref, 1, yu_ref)
    act = (_gelu_tanh(yg_ref[...]) * yu_ref[...]).astype(BF16)
    part = _dot(act, wd_ref[...])

    @pl.when(j == 0)
    def _():
        acc_ref[...] = part

    @pl.when(j > 0)
    def _():
        acc_ref[...] += part

    @pl.when(j == pl.num_programs(1) - 1)
    def _():
        o_ref[...] = x_ref[...] + _rms(acc_ref[...], gpost_ref[...])


def _ffn(x1, gpre, w_up, conv_w, conv_b, w_down, gpost, tm, tf, seq):
    m = x1.shape[0]
    nj = D_FF // tf
    return pl.pallas_call(
        functools.partial(_ffn_kernel, tiles_per_seq=seq // tm),
        grid=(m // tm, nj),
        in_specs=[
            pl.BlockSpec((tm, D_MODEL), lambda i, j: (i, 0)),
            pl.BlockSpec((1, D_MODEL), lambda i, j: (0, 0)),
            pl.BlockSpec((D_MODEL, tf), lambda i, j: (0, j)),
            pl.BlockSpec((D_MODEL, tf), lambda i, j: (0, j + nj)),
            pl.BlockSpec((CONV_W, tf), lambda i, j: (0, j)),
            pl.BlockSpec((CONV_W, tf), lambda i, j: (0, j + nj)),
            pl.BlockSpec((1, tf), lambda i, j: (0, j)),
            pl.BlockSpec((1, tf), lambda i, j: (0, j + nj)),
            pl.BlockSpec((tf, D_MODEL), lambda i, j: (j, 0)),
            pl.BlockSpec((1, D_MODEL), lambda i, j: (0, 0)),
        ],
        out_specs=pl.BlockSpec((tm, D_MODEL), lambda i, j: (i, 0)),
        out_shape=jax.ShapeDtypeStruct((m, D_MODEL), F32),
        scratch_shapes=[
            pltpu.VMEM((tm, D_MODEL), BF16),
            pltpu.VMEM((tm, D_MODEL), F32),
            pltpu.VMEM((nj, 2, SUBLANES, tf), F32),
            pltpu.VMEM((tm, tf), F32),
            pltpu.VMEM((tm, tf), F32),
        ],
        compiler_params=_cparams(("arbitrary", "arbitrary")),
        name="ffn",
    )(x1, gpre, w_up, w_up, conv_w, conv_w, conv_b, conv_b, w_down, gpost)


def _rel_bucket(delta):
    n = jnp.maximum(delta, 0)
    max_exact = N_BUCKETS // 2
    nf = jnp.maximum(n, 1).astype(F32)
    large = max_exact + (jnp.log(nf / max_exact) / math.log(MAX_DISTANCE / max_exact)
                         * (N_BUCKETS - max_exact)).astype(jnp.int32)
    large = jnp.minimum(large, N_BUCKETS - 1)
    return jnp.where(n < max_exact, n, large)


def _bias_strip(table, dist, lo_shift, hi):
    vals = jnp.transpose(table[_rel_bucket(dist)], (2, 0, 1)) - lo_shift[:, None, None]
    masked = (dist < 0) | (dist >= hi)
    return jnp.where(masked[None], NEG, vals * LOG2E).astype(F32)


def _selection_map(nb, nc_pad):
    r = SLC_LEN // CMP_STRIDE
    j = jnp.arange(nb)[:, None]
    c = jnp.arange(nc_pad)[None, :]
    off = r * j - c
    out = jnp.zeros((nb, nc_pad), F32)
    for mm in range(r):
        for nn in range(CMP_LEN // CMP_STRIDE):
            out = out + (off == mm + nn).astype(F32)
    return out * (c < nc_pad - 1)


def kernel(x, pre_mix_norm, w_in, lambda_q1, lambda_k1, lambda_q2, lambda_k2, diff_subln,
           cmp_pos_k, cmp_pos_v, cmp_k_w1, cmp_k_w2, cmp_v_w1, cmp_v_w2, rel_bias, w_out,
           post_mix_norm, pre_ffn_norm, w_up, conv_w, conv_b, w_down, post_ffn_norm):
    b, s, d = x.shape
    assert d == D_MODEL and s % DIFF_TQ == 0 and s // SLC_LEN >= SLC_TOPK
    m = b * s
    depth = w_in.shape[0]
    big = 1 << 30
    for l in range(depth):
        xf = x.reshape(m, d)
        w_main = w_in[l][:, :MAIN_COLS].astype(BF16)
        w_gate = jnp.pad(w_in[l][:, MAIN_COLS:], ((0, 0), (0, LANES - N_GATES))).astype(BF16)
        proj, gates = _inproj(xf, pre_mix_norm[l][None], w_main, w_gate, tm=1024, tn=512)
        proj3 = proj.reshape(b, s, MAIN_COLS)

        bias_diff = rel_bias[:, :DIFF_HEADS]
        bias_nsa = rel_bias[:, DIFF_HEADS:]
        qi = jnp.arange(DIFF_TQ)[:, None]
        t_diff = _bias_strip(bias_diff, qi - jnp.arange(2 * DIFF_TQ)[None, :] + DIFF_TQ,
                             bias_diff[N_BUCKETS - 1], big)
        qn = jnp.arange(NSA_TQ)[:, None]
        t_sel = _bias_strip(bias_nsa, qn - jnp.arange(TS_W)[None, :] + TS_OFF,
                            bias_nsa[N_BUCKETS - 1], big)
        t_win = _bias_strip(bias_nsa, qn - jnp.arange(TW_W)[None, :] + WINDOW,
                            jnp.zeros((NSA_HEADS,), F32), WINDOW)
        t_sel = t_sel.reshape(NSA_KV, NSA_GROUP * NSA_TQ, TS_W)
        t_win = t_win.reshape(NSA_KV, NSA_GROUP * NSA_TQ, TW_W)

        lam_init = 0.8 - 0.6 * math.exp(-0.3 * l)
        o_diff = _diff_attention(proj3, t_diff, lambda_q1[l][None], lambda_k1[l][None],
                                 lambda_q2[l][None], lambda_k2[l][None], diff_subln[l][None],
                                 lam_init)

        nrow = s // CMP_STRIDE

        def rows16(col):
            t = proj3[:, :, col:col + NSA_KV * NSA_DK].reshape(b, nrow, CMP_STRIDE, NSA_KV, NSA_DK)
            return jnp.transpose(t, (0, 3, 1, 2, 4)).reshape(b, NSA_KV, nrow, CMP_STRIDE * NSA_DK)

        r = jnp.stack([rows16(COL_KC), rows16(COL_VC)])
        pos = jnp.stack([cmp_pos_k[l].reshape(1, -1), cmp_pos_v[l].reshape(1, -1)])
        w1 = jnp.stack([cmp_k_w1[l], cmp_v_w1[l]]).astype(BF16)
        w2 = jnp.stack([cmp_k_w2[l], cmp_v_w2[l]]).astype(BF16)
        kvc = _compress(r, pos, w1, w2)

        nb = s // SLC_LEN
        k_slc = jnp.transpose(proj3[:, :, COL_KS:COL_KS + NSA_KV * NSA_DK]
                              .reshape(b, s, NSA_KV, NSA_DK), (0, 2, 1, 3))
        onehot = (jnp.arange(s)[:, None] // SLC_LEN == jnp.arange(nb)[None, :]).astype(BF16)
        ksx = jnp.concatenate([k_slc, jnp.broadcast_to(onehot, (b, NSA_KV, s, nb))], axis=-1)
        gates_g = jnp.transpose(gates[:, :N_GATES].reshape(b, s, NSA_KV, 3 * NSA_GROUP),
                                (0, 2, 1, 3))
        o_nsa = _nsa_attention(proj3, kvc, ksx, t_sel, t_win, gates_g, _selection_map(nb, nrow))

        half = DIFF_HEADS * 2 * DIFF_D
        x1 = _outproj(o_diff.reshape(m, -1), o_nsa.reshape(m, -1), xf,
                      w_out[l][:half].astype(BF16), w_out[l][half:].astype(BF16),
                      post_mix_norm[l][None], tm=512)
        x2 = _ffn(x1, pre_ffn_norm[l][None], w_up[l].astype(BF16), conv_w[l], conv_b[l][None],
                  w_down[l].astype(BF16), post_ffn_norm[l][None], tm=512, tf=512, seq=s)
        x = x2.reshape(b, s, d)
    return x
```

```python
import functools
import math

import jax
import jax.numpy as jnp
from jax import lax
from jax.experimental import pallas as pl
from jax.experimental.pallas import tpu as pltpu

F32 = jnp.float32
BF16 = jnp.bfloat16

D_MODEL = 2048
DIFF_HEADS = 8
DIFF_D = 64
NSA_HEADS = 8
NSA_KV = 2
NSA_GROUP = NSA_HEADS // NSA_KV
NSA_DK = 128
NSA_DV = 128
CMP_LEN = 32
CMP_STRIDE = 16
CMP_HIDDEN = 256
SLC_LEN = 64
SLC_TOPK = 16
WINDOW = 512
N_BUCKETS = 32
MAX_DISTANCE = 128
D_FF = 5632
CONV_W = 3
EPS = 1e-6
NEG = -1e30
FORCE = 1e30
TINY = 1e-30
LOG2E = math.log2(math.e)

LANES = 128
SUBLANES = 8
VMEM_LIMIT = 56 * 1024 * 1024

MAIN_COLS = 5632
N_GATES = NSA_HEADS * 3
COL_DQ, COL_DK, COL_DV, COL_NQ = 0, 1024, 2048, 3072
COL_KC, COL_VC, COL_KS, COL_VS, COL_KW, COL_VW = 4096, 4352, 4608, 4864, 5120, 5376

DIFF_TQ = 512
NSA_TQ = 256
NSA_TK = 512
NSA_WK = WINDOW + NSA_TQ
TS_OFF = 2 * NSA_TK - NSA_TQ
TS_W = TS_OFF + NSA_TK
TW_W = WINDOW + NSA_WK


def _cparams(sem):
    return pltpu.CompilerParams(dimension_semantics=sem, vmem_limit_bytes=VMEM_LIMIT)


def _rms(x, g):
    return x * lax.rsqrt(jnp.mean(x * x, axis=-1, keepdims=True) + EPS) * g


def _gelu_tanh(x):
    return 0.5 * x * (1.0 + jnp.tanh(math.sqrt(2.0 / math.pi) * (x + 0.044715 * (x * x * x))))


def _dot(a, b):
    return jnp.dot(a, b, preferred_element_type=F32)


def _dot_nt(a, b, precision=None):
    return lax.dot_general(a, b, (((1,), (1,)), ((), ())), preferred_element_type=F32,
                           precision=precision)


def _lane_tile(a, n):
    return a if n == 1 else jnp.concatenate([a] * n, axis=1)


def _inproj_kernel(x_ref, g_ref, w_ref, wg_ref, proj_ref, gates_ref, h_ref):
    @pl.when(pl.program_id(1) == 0)
    def _():
        h_ref[...] = _rms(x_ref[...], g_ref[...]).astype(BF16)
        gl = _dot(h_ref[...], wg_ref[...])
        gates_ref[...] = 1.0 / (1.0 + jnp.exp(-gl))

    proj_ref[...] = _dot(h_ref[...], w_ref[...]).astype(BF16)


def _inproj(xf, g, w_main, w_gate, tm, tn):
    m = xf.shape[0]
    return pl.pallas_call(
        _inproj_kernel,
        grid=(m // tm, MAIN_COLS // tn),
        in_specs=[
            pl.BlockSpec((tm, D_MODEL), lambda i, j: (i, 0)),
            pl.BlockSpec((1, D_MODEL), lambda i, j: (0, 0)),
            pl.BlockSpec((D_MODEL, tn), lambda i, j: (0, j)),
            pl.BlockSpec((D_MODEL, LANES), lambda i, j: (0, 0)),
        ],
        out_specs=[
            pl.BlockSpec((tm, tn), lambda i, j: (i, j)),
            pl.BlockSpec((tm, LANES), lambda i, j: (i, 0)),
        ],
        out_shape=[
            jax.ShapeDtypeStruct((m, MAIN_COLS), BF16),
            jax.ShapeDtypeStruct((m, LANES), F32),
        ],
        scratch_shapes=[pltpu.VMEM((tm, D_MODEL), BF16)],
        compiler_params=_cparams(("arbitrary", "arbitrary")),
        name="inproj",
    )(xf, g, w_main, w_gate)


def _diff_kernel(q_ref, k_ref, v_ref, t_ref, lq1_ref, lk1_ref, lq2_ref, lk2_ref, sub_ref,
                 o_ref, qs_ref, m_ref, l_ref, acc_ref, *, lam_init):
    tq = DIFF_TQ
    i = pl.program_id(2)
    q = q_ref[0].astype(F32) * (DIFF_D ** -0.5 * LOG2E)
    lane = lax.broadcasted_iota(jnp.int32, q.shape, 1)
    qs_ref[...] = jnp.concatenate(
        [jnp.where(lane < DIFF_D, q, 0.0), jnp.where(lane >= DIFF_D, q, 0.0)], axis=0).astype(BF16)
    m_ref[...] = jnp.full(m_ref.shape, NEG, F32)
    l_ref[...] = jnp.zeros(l_ref.shape, F32)
    acc_ref[...] = jnp.zeros(acc_ref.shape, F32)

    def step(kstart, bias):
        k = k_ref[0, pl.ds(kstart, tq), :]
        v = v_ref[0, pl.ds(kstart, tq), :]
        s = _dot_nt(qs_ref[...], k)
        if bias is not None:
            s = s + jnp.concatenate([bias, bias], axis=0)
        m_prev = m_ref[...]
        m_new = jnp.maximum(m_prev, jnp.max(s, axis=1, keepdims=True))
        alpha = jnp.exp2(m_prev - m_new)
        p = jnp.exp2(s - _lane_tile(m_new, tq // LANES))
        l_ref[...] = alpha * l_ref[...] + jnp.sum(p, axis=1, keepdims=True)
        acc_ref[...] = alpha * acc_ref[...] + _dot(p.astype(BF16), v)
        m_ref[...] = m_new

    def far_body(j, carry):
        step(pl.multiple_of(j * tq, tq), None)
        return carry

    lax.fori_loop(0, jnp.maximum(i - 1, 0), far_body, 0)

    @pl.when(i > 0)
    def _():
        step(pl.multiple_of((i - 1) * tq, tq), t_ref[0, :, 0:tq])

    step(pl.multiple_of(i * tq, tq), t_ref[0, :, tq:2 * tq])

    o = acc_ref[...] / jnp.maximum(l_ref[...], TINY)
    lam = (jnp.exp(jnp.sum(lq1_ref[...] * lk1_ref[...], axis=1, keepdims=True))
           - jnp.exp(jnp.sum(lq2_ref[...] * lk2_ref[...], axis=1, keepdims=True)) + lam_init)
    a = o[0:tq] - lam * o[tq:2 * tq]
    o_ref[0] = (_rms(a, sub_ref[...]) * (1.0 - lam_init)).astype(BF16)


def _diff_attention(proj3, t_diff, lq1, lk1, lq2, lk2, subln, lam_init):
    b, s, _ = proj3.shape
    tq = DIFF_TQ
    vec = lambda n: pl.BlockSpec((1, n), lambda bi, h, i: (0, 0))
    return pl.pallas_call(
        functools.partial(_diff_kernel, lam_init=lam_init),
        grid=(b, DIFF_HEADS, s // tq),
        in_specs=[
            pl.BlockSpec((1, tq, LANES), lambda bi, h, i: (bi, i, COL_DQ // LANES + h)),
            pl.BlockSpec((1, s, LANES), lambda bi, h, i: (bi, 0, COL_DK // LANES + h)),
            pl.BlockSpec((1, s, LANES), lambda bi, h, i: (bi, 0, COL_DV // LANES + h)),
            pl.BlockSpec((1, tq, 2 * tq), lambda bi, h, i: (h, 0, 0)),
            vec(DIFF_D), vec(DIFF_D), vec(DIFF_D), vec(DIFF_D), vec(2 * DIFF_D),
        ],
        out_specs=pl.BlockSpec((1, tq, LANES), lambda bi, h, i: (bi, i, h)),
        out_shape=jax.ShapeDtypeStruct((b, s, DIFF_HEADS * 2 * DIFF_D), BF16),
        scratch_shapes=[
            pltpu.VMEM((2 * tq, LANES), BF16),
            pltpu.VMEM((2 * tq, LANES), F32),
            pltpu.VMEM((2 * tq, LANES), F32),
            pltpu.VMEM((2 * tq, LANES), F32),
        ],
        compiler_params=_cparams(("arbitrary", "arbitrary", "arbitrary")),
        name="diff_attn",
    )(proj3, proj3, proj3, t_diff, lq1, lk1, lq2, lk2, subln)


def _compress_kernel(r_ref, pos_ref, w1_ref, w2_ref, o_ref):
    half = CMP_STRIDE * NSA_DK
    r = r_ref[0, 0, 0]
    n = r.shape[0]
    a = _dot(r, w1_ref[0, 0:half, :])
    bm = _dot(r, w1_ref[0, half:2 * half, :])
    posb = jnp.broadcast_to(pos_ref[0], (SUBLANES, 2 * half)).astype(BF16)
    pt = _dot(posb, w1_ref[0])[0:1]
    hid = a + pltpu.roll(bm, n - 1, axis=0) + pt
    o_ref[0, 0, 0] = _dot(_gelu_tanh(hid).astype(BF16), w2_ref[0]).astype(BF16)


def _compress(r, pos, w1, w2):
    _, b, g, n, width = r.shape
    return pl.pallas_call(
        _compress_kernel,
        grid=(2, b, g),
        in_specs=[
            pl.BlockSpec((1, 1, 1, n, width), lambda t, bi, gi: (t, bi, gi, 0, 0)),
            pl.BlockSpec((1, 1, CMP_LEN * NSA_DK), lambda t, bi, gi: (t, 0, 0)),
            pl.BlockSpec((1, CMP_LEN * NSA_DK, CMP_HIDDEN), lambda t, bi, gi: (t, 0, 0)),
            pl.BlockSpec((1, CMP_HIDDEN, NSA_DK), lambda t, bi, gi: (t, 0, 0)),
        ],
        out_specs=pl.BlockSpec((1, 1, 1, n, NSA_DK), lambda t, bi, gi: (t, bi, gi, 0, 0)),
        out_shape=jax.ShapeDtypeStruct((2, b, g, n, NSA_DK), BF16),
        compiler_params=_cparams(("arbitrary", "arbitrary", "arbitrary")),
        name="nsa_compress",
    )(r, pos, w1, w2)


def _nsa_kernel(q_ref, kc_ref, vc_ref, ks_ref, vs_ref, kw_ref, vw_ref, ts_ref, tw_ref, gate_ref,
                selmap_ref, o_ref, qx_ref, m_ref, l_ref, acc_ref, part_ref):
    tq, tk, hg = NSA_TQ, NSA_TK, NSA_GROUP
    rows = hg * tq
    i = pl.program_id(2)
    q0 = i * tq
    qt = q_ref[0]
    q4 = jnp.concatenate([qt[:, h * NSA_DK:(h + 1) * NSA_DK] for h in range(hg)], axis=0)
    q4 = (q4.astype(F32) * (NSA_DK ** -0.5 * LOG2E)).astype(BF16)

    kc = kc_ref[0, 0, 0]
    nc = kc.shape[0]
    cend = lax.broadcasted_iota(jnp.int32, (tq, nc), 1) * CMP_STRIDE + (CMP_LEN - 1)
    hidden = jnp.where(cend <= q0 + lax.broadcasted_iota(jnp.int32, (tq, nc), 0), 0.0, NEG)
    sc = _dot_nt(q4, kc) + jnp.concatenate([hidden] * hg, axis=0)
    mc = jnp.max(sc, axis=1, keepdims=True)
    pc = jnp.exp2(sc - mc)
    norm = jnp.where(mc > 0.5 * NEG,
                     1.0 / jnp.maximum(jnp.sum(pc, axis=1, keepdims=True), TINY), 0.0)
    pc = pc * norm
    o_cmp = _dot(pc.astype(BF16), vc_ref[0, 0, 0])

    pcsum = pc[0:tq] + pc[tq:2 * tq] + pc[2 * tq:3 * tq] + pc[3 * tq:4 * tq]
    imp_t = _dot_nt(selmap_ref[...], pcsum, precision=lax.Precision.HIGHEST)
    nb = imp_t.shape[0]
    blk = lax.broadcasted_iota(jnp.int32, (nb, tq), 0)
    cur = lax.shift_right_logical(q0 + lax.broadcasted_iota(jnp.int32, (nb, tq), 1),
                                  int(math.log2(SLC_LEN)))
    forced = (blk == 0) | (blk == cur) | (blk == cur - 1)
    excluded = -3.0e38
    val = jnp.where(forced | (blk > cur), excluded, imp_t)
    sel = jnp.where(forced, 1.0, 0.0)
    for _ in range(min(SLC_TOPK, nb) - 3):
        best = jnp.max(val, axis=0, keepdims=True)
        first = jnp.min(jnp.where(val == best, blk, nb), axis=0, keepdims=True)
        hit = blk == first
        sel = jnp.where(hit, 1.0, sel)
        val = jnp.where(hit, excluded, val)
    maskbias = jnp.where(sel.T > 0.5, 0.0, NEG).astype(BF16)
    qx_ref[:, 0:NSA_DK] = q4
    qx_ref[:, NSA_DK:NSA_DK + nb] = jnp.concatenate([maskbias] * hg, axis=0)

    kst = pl.multiple_of(jnp.maximum(q0 - WINDOW, 0), tq)
    w0 = pl.multiple_of(WINDOW - (q0 - kst), LANES)
    lw = _dot_nt(q4, kw_ref[0, pl.ds(kst, NSA_WK), :]) + tw_ref[0, :, pl.ds(w0, NSA_WK)]
    pw = jnp.exp2(lw - jnp.max(lw, axis=1, keepdims=True))
    o_win = _dot(pw.astype(BF16), vw_ref[0, pl.ds(kst, NSA_WK), :])
    o_win = o_win / jnp.maximum(jnp.sum(pw, axis=1, keepdims=True), TINY)

    gt = gate_ref[0, 0]
    gcol = lambda c: jnp.concatenate([gt[:, 3 * h + c:3 * h + c + 1] for h in range(hg)], axis=0)
    part_ref[...] = gcol(0) * o_cmp + gcol(2) * o_win

    m_ref[...] = jnp.full(m_ref.shape, NEG, F32)
    l_ref[...] = jnp.zeros(l_ref.shape, F32)
    acc_ref[...] = jnp.zeros(acc_ref.shape, F32)

    def step(j, near):
        kstart = pl.multiple_of(j * tk, tk)
        s = _dot_nt(qx_ref[...], ks_ref[0, 0, pl.ds(kstart, tk), :])
        if near:
            u0 = pl.multiple_of(TS_OFF - (q0 - j * tk), LANES)
            s = s + ts_ref[0, :, pl.ds(u0, tk)]
        m_prev = m_ref[...]
        m_new = jnp.maximum(m_prev, jnp.max(s, axis=1, keepdims=True))
        alpha = jnp.exp2(m_prev - m_new)
        p = jnp.exp2(s - _lane_tile(m_new, tk // LANES))
        l_ref[...] = alpha * l_ref[...] + jnp.sum(p, axis=1, keepdims=True)
        acc_ref[...] = alpha * acc_ref[...] + _dot(p.astype(BF16), vs_ref[0, pl.ds(kstart, tk), :])
        m_ref[...] = m_new

    jl = (q0 + tq - 1) // tk

    def far_body(j, carry):
        step(j, False)
        return carry

    lax.fori_loop(0, jnp.maximum(jl - 1, 0), far_body, 0)

    @pl.when(jl > 0)
    def _():
        step(jl - 1, True)

    step(jl, True)
    o = part_ref[...] + gcol(1) * (acc_ref[...] / jnp.maximum(l_ref[...], TINY))
    for h in range(hg):
        o_ref[0, :, h * NSA_DV:(h + 1) * NSA_DV] = o[h * tq:(h + 1) * tq].astype(BF16)


def _nsa_attention(proj3, kvc, ksx, t_sel, t_win, gates_g, selmap):
    b, s, _ = proj3.shape
    tq, hg = NSA_TQ, NSA_GROUP
    nc = kvc.shape[3]
    nb = s // SLC_LEN
    qcols = hg * NSA_DK
    return pl.pallas_call(
        _nsa_kernel,
        grid=(b, NSA_KV, s // tq),
        in_specs=[
            pl.BlockSpec((1, tq, qcols), lambda bi, g, i: (bi, i, COL_NQ // qcols + g)),
            pl.BlockSpec((1, 1, 1, nc, NSA_DK), lambda bi, g, i: (0, bi, g, 0, 0)),
            pl.BlockSpec((1, 1, 1, nc, NSA_DV), lambda bi, g, i: (1, bi, g, 0, 0)),
            pl.BlockSpec((1, 1, s, NSA_DK + nb), lambda bi, g, i: (bi, g, 0, 0)),
            pl.BlockSpec((1, s, NSA_DV), lambda bi, g, i: (bi, 0, COL_VS // NSA_DV + g)),
            pl.BlockSpec((1, s, NSA_DK), lambda bi, g, i: (bi, 0, COL_KW // NSA_DK + g)),
            pl.BlockSpec((1, s, NSA_DV), lambda bi, g, i: (bi, 0, COL_VW // NSA_DV + g)),
            pl.BlockSpec((1, hg * tq, TS_W), lambda bi, g, i: (g, 0, 0),
                         pipeline_mode=pl.Buffered(1)),
            pl.BlockSpec((1, hg * tq, TW_W), lambda bi, g, i: (g, 0, 0),
                         pipeline_mode=pl.Buffered(1)),
            pl.BlockSpec((1, 1, tq, 3 * hg), lambda bi, g, i: (bi, g, i, 0)),
            pl.BlockSpec((nb, nc), lambda bi, g, i: (0, 0)),
        ],
        out_specs=pl.BlockSpec((1, tq, hg * NSA_DV), lambda bi, g, i: (bi, i, g)),
        out_shape=jax.ShapeDtypeStruct((b, s, NSA_HEADS * NSA_DV), BF16),
        scratch_shapes=[
            pltpu.VMEM((hg * tq, NSA_DK + nb), BF16),
            pltpu.VMEM((hg * tq, LANES), F32),
            pltpu.VMEM((hg * tq, LANES), F32),
            pltpu.VMEM((hg * tq, NSA_DV), F32),
            pltpu.VMEM((hg * tq, NSA_DV), F32),
        ],
        compiler_params=_cparams(("arbitrary", "arbitrary", "arbitrary")),
        name="nsa_attn",
    )(proj3, kvc, kvc, ksx, proj3, proj3, proj3, t_sel, t_win, gates_g, selmap)


def _outproj_kernel(od_ref, on_ref, x_ref, wd_ref, wn_ref, g_ref, o_ref):
    y = _dot(od_ref[...], wd_ref[...]) + _dot(on_ref[...], wn_ref[...])
    o_ref[...] = x_ref[...] + _rms(y, g_ref[...])


def _outproj(od, on, xf, w_d, w_n, g, tm):
    m = xf.shape[0]
    kd, kn = od.shape[1], on.shape[1]
    return pl.pallas_call(
        _outproj_kernel,
        grid=(m // tm,),
        in_specs=[
            pl.BlockSpec((tm, kd), lambda i: (i, 0)),
            pl.BlockSpec((tm, kn), lambda i: (i, 0)),
            pl.BlockSpec((tm, D_MODEL), lambda i: (i, 0)),
            pl.BlockSpec((kd, D_MODEL), lambda i: (0, 0)),
            pl.BlockSpec((kn, D_MODEL), lambda i: (0, 0)),
            pl.BlockSpec((1, D_MODEL), lambda i: (0, 0)),
        ],
        out_specs=pl.BlockSpec((tm, D_MODEL), lambda i: (i, 0)),
        out_shape=jax.ShapeDtypeStruct((m, D_MODEL), F32),
        compiler_params=_cparams(("arbitrary",)),
        name="outproj",
    )(od, on, xf, w_d, w_n, g)


def _ffn_kernel(x_ref, gpre_ref, wg_ref, wu_ref, cwg_ref, cwu_ref, cbg_ref, cbu_ref, wd_ref,
                gpost_ref, o_ref, h_ref, acc_ref, carry_ref, yg_ref, yu_ref, *, tiles_per_seq):
    i = pl.program_id(0)
    j = pl.program_id(1)
    tm = x_ref.shape[0]
    halo = SUBLANES

    @pl.when(j == 0)
    def _():
        h_ref[...] = _rms(x_ref[...], gpre_ref[...]).astype(BF16)

    @pl.when(lax.rem(i, tiles_per_seq) == 0)
    def _():
        carry_ref[j] = jnp.zeros(carry_ref.shape[1:], F32)

    def conv(w_ref, cw_ref, cb_ref, slot, y_ref):
        u = _dot(h_ref[...], w_ref[...])
        prev = carry_ref[j, slot]
        carry_ref[j, slot] = u[tm - halo:tm]
        w0, w1, w2 = cw_ref[0:1, :], cw_ref[1:2, :], cw_ref[2:3, :]
        bias = cb_ref[...]

        def taps(z):
            n = z.shape[0]
            return bias + w2 * z + w1 * pltpu.roll(z, 1, axis=0) + w0 * pltpu.roll(z, 2, axis=0)

        y_ref[...] = taps(u)
        y_ref[0:halo, :] = taps(jnp.concatenate([prev, u[0:halo]], axis=0))[halo:2 * halo]

    conv(wg_ref, cwg_ref, cbg_ref, 0, yg_ref)
    conv(wu_ref, cwu_ref, cbu_ref, 1, yu_ref)
    act = (_gelu_tanh(yg_ref[...]) * yu_ref[...]).astype(BF16)
    part = _dot(act, wd_ref[...])

    @pl.when(j == 0)
    def _():
        acc_ref[...] = part

    @pl.when(j > 0)
    def _():
        acc_ref[...] += part

    @pl.when(j == pl.num_programs(1) - 1)
    def _():
        o_ref[...] = x_ref[...] + _rms(acc_ref[...], gpost_ref[...])


def _ffn(x1, gpre, w_up, conv_w, conv_b, w_down, gpost, tm, tf, seq):
    m = x1.shape[0]
    nj = D_FF // tf
    return pl.pallas_call(
        functools.partial(_ffn_kernel, tiles_per_seq=seq // tm),
        grid=(m // tm, nj),
        in_specs=[
            pl.BlockSpec((tm, D_MODEL), lambda i, j: (i, 0)),
            pl.BlockSpec((1, D_MODEL), lambda i, j: (0, 0)),
            pl.BlockSpec((D_MODEL, tf), lambda i, j: (0, j)),
            pl.BlockSpec((D_MODEL, tf), lambda i, j: (0, j + nj)),
            pl.BlockSpec((CONV_W, tf), lambda i, j: (0, j)),
            pl.BlockSpec((CONV_W, tf), lambda i, j: (0, j + nj)),
            pl.BlockSpec((1, tf), lambda i, j: (0, j)),
            pl.BlockSpec((1, tf), lambda i, j: (0, j + nj)),
            pl.BlockSpec((tf, D_MODEL), lambda i, j: (j, 0)),
            pl.BlockSpec((1, D_MODEL), lambda i, j: (0, 0)),
        ],
        out_specs=pl.BlockSpec((tm, D_MODEL), lambda i, j: (i, 0)),
        out_shape=jax.ShapeDtypeStruct((m, D_MODEL), F32),
        scratch_shapes=[
            pltpu.VMEM((tm, D_MODEL), BF16),
            pltpu.VMEM((tm, D_MODEL), F32),
            pltpu.VMEM((nj, 2, SUBLANES, tf), F32),
            pltpu.VMEM((tm, tf), F32),
            pltpu.VMEM((tm, tf), F32),
        ],
        compiler_params=_cparams(("arbitrary", "arbitrary")),
        name="ffn",
    )(x1, gpre, w_up, w_up, conv_w, conv_w, conv_b, conv_b, w_down, gpost)


def _rel_bucket(delta):
    n = jnp.maximum(delta, 0)
    max_exact = N_BUCKETS // 2
    nf = jnp.maximum(n, 1).astype(F32)
    large = max_exact + (jnp.log(nf / max_exact) / math.log(MAX_DISTANCE / max_exact)
                         * (N_BUCKETS - max_exact)).astype(jnp.int32)
    large = jnp.minimum(large, N_BUCKETS - 1)
    return jnp.where(n < max_exact, n, large)


def _bias_kernel(tab_ref, bkt_ref, o_ref):
    h = pl.program_id(0)
    bkt = bkt_ref[...]
    out = jnp.full(bkt.shape, NEG, F32)
    for bucket in range(N_BUCKETS):
        out = jnp.where(bkt == bucket, tab_ref[bucket, h], out)
    o_ref[0] = out


def _bias_strip(table, dist, hi):
    heads = table.shape[1]
    rows, width = dist.shape
    bkt = jnp.where((dist < 0) | (dist >= hi), -1, _rel_bucket(dist)).astype(jnp.int32)
    tr = LANES
    return pl.pallas_call(
        _bias_kernel,
        grid=(heads, rows // tr),
        in_specs=[
            pl.BlockSpec(memory_space=pltpu.SMEM),
            pl.BlockSpec((tr, width), lambda h, i: (i, 0)),
        ],
        out_specs=pl.BlockSpec((1, tr, width), lambda h, i: (h, i, 0)),
        out_shape=jax.ShapeDtypeStruct((heads, rows, width), F32),
        compiler_params=_cparams(("arbitrary", "arbitrary")),
        name="bias_strip",
    )(table, bkt)


def _selection_map(nb, nc_pad):
    r = SLC_LEN // CMP_STRIDE
    j = jnp.arange(nb)[:, None]
    c = jnp.arange(nc_pad)[None, :]
    off = r * j - c
    out = jnp.zeros((nb, nc_pad), F32)
    for mm in range(r):
        for nn in range(CMP_LEN // CMP_STRIDE):
            out = out + (off == mm + nn).astype(F32)
    return out * (c < nc_pad - 1)


def kernel(x, pre_mix_norm, w_in, lambda_q1, lambda_k1, lambda_q2, lambda_k2, diff_subln,
           cmp_pos_k, cmp_pos_v, cmp_k_w1, cmp_k_w2, cmp_v_w1, cmp_v_w2, rel_bias, w_out,
           post_mix_norm, pre_ffn_norm, w_up, conv_w, conv_b, w_down, post_ffn_norm):
    b, s, d = x.shape
    assert d == D_MODEL and s % DIFF_TQ == 0 and s // SLC_LEN >= SLC_TOPK
    m = b * s
    depth = w_in.shape[0]
    big = 1 << 30
    for l in range(depth):
        xf = x.reshape(m, d)
        w_main = w_in[l][:, :MAIN_COLS].astype(BF16)
        w_gate = jnp.pad(w_in[l][:, MAIN_COLS:], ((0, 0), (0, LANES - N_GATES))).astype(BF16)
        proj, gates = _inproj(xf, pre_mix_norm[l][None], w_main, w_gate, tm=1024, tn=512)
        proj3 = proj.reshape(b, s, MAIN_COLS)

        bias_diff = rel_bias[:, :DIFF_HEADS]
        bias_nsa = rel_bias[:, DIFF_HEADS:]
        qi = jnp.arange(DIFF_TQ)[:, None]
        t_diff = _bias_strip((bias_diff - bias_diff[N_BUCKETS - 1]) * LOG2E,
                             qi - jnp.arange(2 * DIFF_TQ)[None, :] + DIFF_TQ, big)
        qn = jnp.arange(NSA_TQ)[:, None]
        t_sel = _bias_strip((bias_nsa - bias_nsa[N_BUCKETS - 1]) * LOG2E,
                            qn - jnp.arange(TS_W)[None, :] + TS_OFF, big)
        t_win = _bias_strip(bias_nsa * LOG2E, qn - jnp.arange(TW_W)[None, :] + WINDOW, WINDOW)
        t_sel = t_sel.reshape(NSA_KV, NSA_GROUP * NSA_TQ, TS_W)
        t_win = t_win.reshape(NSA_KV, NSA_GROUP * NSA_TQ, TW_W)

        lam_init = 0.8 - 0.6 * math.exp(-0.3 * l)
        o_diff = _diff_attention(proj3, t_diff, lambda_q1[l][None], lambda_k1[l][None],
                                 lambda_q2[l][None], lambda_k2[l][None], diff_subln[l][None],
                                 lam_init)

        nrow = s // CMP_STRIDE

        def rows16(col):
            t = proj3[:, :, col:col + NSA_KV * NSA_DK].reshape(b, nrow, CMP_STRIDE, NSA_KV, NSA_DK)
            return jnp.transpose(t, (0, 3, 1, 2, 4)).reshape(b, NSA_KV, nrow, CMP_STRIDE * NSA_DK)

        r = jnp.stack([rows16(COL_KC), rows16(COL_VC)])
        pos = jnp.stack([cmp_pos_k[l].reshape(1, -1), cmp_pos_v[l].reshape(1, -1)])
        w1 = jnp.stack([cmp_k_w1[l], cmp_v_w1[l]]).astype(BF16)
        w2 = jnp.stack([cmp_k_w2[l], cmp_v_w2[l]]).astype(BF16)
        kvc = _compress(r, pos, w1, w2)

        nb = s // SLC_LEN
        k_slc = jnp.transpose(proj3[:, :, COL_KS:COL_KS + NSA_KV * NSA_DK]
                              .reshape(b, s, NSA_KV, NSA_DK), (0, 2, 1, 3))
        onehot = (jnp.arange(s)[:, None] // SLC_LEN == jnp.arange(nb)[None, :]).astype(BF16)
        ksx = jnp.concatenate([k_slc, jnp.broadcast_to(onehot, (b, NSA_KV, s, nb))], axis=-1)
        gates_g = jnp.transpose(gates[:, :N_GATES].reshape(b, s, NSA_KV, 3 * NSA_GROUP),
                                (0, 2, 1, 3))
        o_nsa = _nsa_attention(proj3, kvc, ksx, t_sel, t_win, gates_g, _selection_map(nb, nrow))

        half = DIFF_HEADS * 2 * DIFF_D
        x1 = _outproj(o_diff.reshape(m, -1), o_nsa.reshape(m, -1), xf,
                      w_out[l][:half].astype(BF16), w_out[l][half:].astype(BF16),
                      post_mix_norm[l][None], tm=512)
        x2 = _ffn(x1, pre_ffn_norm[l][None], w_up[l].astype(BF16), conv_w[l], conv_b[l][None],
                  w_down[l].astype(BF16), post_ffn_norm[l][None], tm=512, tf=512, seq=s)
        x = x2.reshape(b, s, d)
    return x
```

```python
import functools
import math

import jax
import jax.numpy as jnp
from jax import lax
from jax.experimental import pallas as pl
from jax.experimental.pallas import tpu as pltpu

F32 = jnp.float32
BF16 = jnp.bfloat16

D_MODEL = 2048
DIFF_HEADS = 8
DIFF_D = 64
NSA_HEADS = 8
NSA_KV = 2
NSA_GROUP = NSA_HEADS // NSA_KV
NSA_DK = 128
NSA_DV = 128
CMP_LEN = 32
CMP_STRIDE = 16
CMP_HIDDEN = 256
SLC_LEN = 64
SLC_TOPK = 16
WINDOW = 512
N_BUCKETS = 32
MAX_DISTANCE = 128
D_FF = 5632
CONV_W = 3
EPS = 1e-6
NEG = -1e30
FORCE = 1e30
TINY = 1e-30
LOG2E = math.log2(math.e)

LANES = 128
SUBLANES = 8
VMEM_LIMIT = 56 * 1024 * 1024

MAIN_COLS = 5632
N_GATES = NSA_HEADS * 3
COL_DQ, COL_DK, COL_DV, COL_NQ = 0, 1024, 2048, 3072
COL_KC, COL_VC, COL_KS, COL_VS, COL_KW, COL_VW = 4096, 4352, 4608, 4864, 5120, 5376

DIFF_TQ = 512
NSA_TQ = 256
FLASH_SUB = 512
NSA_TK = 512
NSA_WK = WINDOW + NSA_TQ
TS_OFF = 2 * NSA_TK - NSA_TQ
TS_W = TS_OFF + NSA_TK
TW_W = WINDOW + NSA_WK


def _cparams(sem):
    return pltpu.CompilerParams(dimension_semantics=sem, vmem_limit_bytes=VMEM_LIMIT)


def _rms(x, g):
    return x * lax.rsqrt(jnp.mean(x * x, axis=-1, keepdims=True) + EPS) * g


def _gelu_tanh(x):
    return 0.5 * x * (1.0 + jnp.tanh(math.sqrt(2.0 / math.pi) * (x + 0.044715 * (x * x * x))))


def _dot(a, b):
    return jnp.dot(a, b, preferred_element_type=F32)


def _dot_nt(a, b, precision=None):
    return lax.dot_general(a, b, (((1,), (1,)), ((), ())), preferred_element_type=F32,
                           precision=precision)


def _lane_tile(a, n):
    return a if n == 1 else jnp.concatenate([a] * n, axis=1)


def _online_softmax_update(s, v, m_ref, l_ref, acc_ref):
    m_prev, l, acc = m_ref[...], l_ref[...], acc_ref[...]
    for c in range(0, s.shape[1], FLASH_SUB):
        sc = s[:, c:c + FLASH_SUB]
        m_new = jnp.maximum(m_prev, jnp.max(sc, axis=1, keepdims=True))
        alpha = jnp.exp2(m_prev - m_new)
        p = jnp.exp2(sc - _lane_tile(m_new, FLASH_SUB // LANES))
        l = alpha * l + jnp.sum(p, axis=1, keepdims=True)
        acc = alpha * acc + _dot(p.astype(BF16), v[c:c + FLASH_SUB])
        m_prev = m_new
    m_ref[...], l_ref[...], acc_ref[...] = m_prev, l, acc


def _inproj_kernel(x_ref, g_ref, w_ref, wg_ref, proj_ref, gates_ref, ksx_ref, h_ref, *, seq):
    i = pl.program_id(0)
    j = pl.program_id(1)
    tm, tn = proj_ref.shape

    @pl.when(j == 0)
    def _():
        h_ref[...] = _rms(x_ref[...], g_ref[...]).astype(BF16)
        gl = _dot(h_ref[...], wg_ref[...])
        gates_ref[...] = 1.0 / (1.0 + jnp.exp(-gl))

    acc = _dot(h_ref[...], w_ref[...])
    proj_ref[...] = acc.astype(BF16)

    @pl.when(j == COL_KS // tn)
    def _():
        nb = seq // SLC_LEN
        xw = NSA_DK + nb
        pos = lax.rem(i * tm + lax.broadcasted_iota(jnp.int32, (tm, nb), 0), seq)
        blk = lax.shift_right_logical(pos, int(math.log2(SLC_LEN)))
        onehot = jnp.where(blk == lax.broadcasted_iota(jnp.int32, (tm, nb), 1), 1.0, 0.0)
        for g in range(NSA_KV):
            ksx_ref[:, g * xw:g * xw + NSA_DK] = acc[:, g * NSA_DK:(g + 1) * NSA_DK].astype(BF16)
            ksx_ref[:, g * xw + NSA_DK:(g + 1) * xw] = onehot.astype(BF16)


def _inproj(xf, g, w_main, w_gate, tm, tn, seq):
    m = xf.shape[0]
    assert COL_KS % tn == 0 and tn >= NSA_KV * NSA_DK
    xw = NSA_DK + seq // SLC_LEN
    ng = w_gate.shape[1]
    return pl.pallas_call(
        functools.partial(_inproj_kernel, seq=seq),
        grid=(m // tm, MAIN_COLS // tn),
        in_specs=[
            pl.BlockSpec((tm, D_MODEL), lambda i, j: (i, 0)),
            pl.BlockSpec((1, D_MODEL), lambda i, j: (0, 0)),
            pl.BlockSpec((D_MODEL, tn), lambda i, j: (0, j)),
            pl.BlockSpec((D_MODEL, ng), lambda i, j: (0, 0)),
        ],
        out_specs=[
            pl.BlockSpec((tm, tn), lambda i, j: (i, j)),
            pl.BlockSpec((tm, ng), lambda i, j: (i, 0)),
            pl.BlockSpec((tm, NSA_KV * xw), lambda i, j: (i, 0)),
        ],
        out_shape=[
            jax.ShapeDtypeStruct((m, MAIN_COLS), BF16),
            jax.ShapeDtypeStruct((m, ng), F32),
            jax.ShapeDtypeStruct((m, NSA_KV * xw), BF16),
        ],
        scratch_shapes=[pltpu.VMEM((tm, D_MODEL), BF16)],
        compiler_params=_cparams(("arbitrary", "arbitrary")),
        name="inproj",
    )(xf, g, w_main, w_gate)


def _diff_kernel(q_ref, k_ref, v_ref, t_ref, lq1_ref, lk1_ref, lq2_ref, lk2_ref, sub_ref,
                 o_ref, qs_ref, m_ref, l_ref, acc_ref, *, lam_init):
    tq = DIFF_TQ
    i = pl.program_id(2)
    q = q_ref[0].astype(F32) * (DIFF_D ** -0.5 * LOG2E)
    lane = lax.broadcasted_iota(jnp.int32, q.shape, 1)
    qs_ref[...] = jnp.concatenate(
        [jnp.where(lane < DIFF_D, q, 0.0), jnp.where(lane >= DIFF_D, q, 0.0)], axis=0).astype(BF16)
    m_ref[...] = jnp.full(m_ref.shape, NEG, F32)
    l_ref[...] = jnp.zeros(l_ref.shape, F32)
    acc_ref[...] = jnp.zeros(acc_ref.shape, F32)

    def step(kstart, width, bias):
        k = k_ref[0, pl.ds(kstart, width), :]
        s = _dot_nt(qs_ref[...], k)
        if bias is not None:
            s = s + jnp.concatenate([bias, bias], axis=0)
        _online_softmax_update(s, v_ref[0, pl.ds(kstart, width), :], m_ref, l_ref, acc_ref)

    nfar = jnp.maximum(i - 1, 0)

    def far_body(j, carry):
        step(pl.multiple_of(j * (2 * tq), 2 * tq), 2 * tq, None)
        return carry

    lax.fori_loop(0, lax.shift_right_logical(nfar, 1), far_body, 0)

    @pl.when(lax.rem(nfar, 2) == 1)
    def _():
        step(pl.multiple_of((nfar - 1) * tq, tq), tq, None)

    @pl.when(i > 0)
    def _():
        step(pl.multiple_of((i - 1) * tq, tq), 2 * tq, t_ref[0])

    @pl.when(i == 0)
    def _():
        step(0, tq, t_ref[0, :, tq:2 * tq])

    o = acc_ref[...] / jnp.maximum(l_ref[...], TINY)
    lam = (jnp.exp(jnp.sum(lq1_ref[...] * lk1_ref[...], axis=1, keepdims=True))
           - jnp.exp(jnp.sum(lq2_ref[...] * lk2_ref[...], axis=1, keepdims=True)) + lam_init)
    a = o[0:tq] - lam * o[tq:2 * tq]
    o_ref[0] = (_rms(a, sub_ref[...]) * (1.0 - lam_init)).astype(BF16)


def _diff_attention(proj3, t_diff, lq1, lk1, lq2, lk2, subln, lam_init):
    b, s, _ = proj3.shape
    tq = DIFF_TQ
    vec = lambda n: pl.BlockSpec((1, n), lambda bi, h, i: (0, 0))
    return pl.pallas_call(
        functools.partial(_diff_kernel, lam_init=lam_init),
        grid=(b, DIFF_HEADS, s // tq),
        in_specs=[
            pl.BlockSpec((1, tq, LANES), lambda bi, h, i: (bi, i, COL_DQ // LANES + h)),
            pl.BlockSpec((1, s, LANES), lambda bi, h, i: (bi, 0, COL_DK // LANES + h)),
            pl.BlockSpec((1, s, LANES), lambda bi, h, i: (bi, 0, COL_DV // LANES + h)),
            pl.BlockSpec((1, tq, 2 * tq), lambda bi, h, i: (h, 0, 0)),
            vec(DIFF_D), vec(DIFF_D), vec(DIFF_D), vec(DIFF_D), vec(2 * DIFF_D),
        ],
        out_specs=pl.BlockSpec((1, tq, LANES), lambda bi, h, i: (bi, i, h)),
        out_shape=jax.ShapeDtypeStruct((b, s, DIFF_HEADS * 2 * DIFF_D), BF16),
        scratch_shapes=[
            pltpu.VMEM((2 * tq, LANES), BF16),
            pltpu.VMEM((2 * tq, LANES), F32),
            pltpu.VMEM((2 * tq, LANES), F32),
            pltpu.VMEM((2 * tq, LANES), F32),
        ],
        compiler_params=_cparams(("arbitrary", "arbitrary", "arbitrary")),
        name="diff_attn",
    )(proj3, proj3, proj3, t_diff, lq1, lk1, lq2, lk2, subln)


def _compress_kernel(r_ref, pos_ref, w1_ref, w2_ref, o_ref):
    half = CMP_STRIDE * NSA_DK
    r = r_ref[0, 0, 0]
    n = r.shape[0]
    a = _dot(r, w1_ref[0, 0:half, :])
    bm = _dot(r, w1_ref[0, half:2 * half, :])
    posb = jnp.broadcast_to(pos_ref[0], (SUBLANES, 2 * half)).astype(BF16)
    pt = _dot(posb, w1_ref[0])[0:1]
    hid = a + pltpu.roll(bm, n - 1, axis=0) + pt
    o_ref[0, 0, 0] = _dot(_gelu_tanh(hid).astype(BF16), w2_ref[0]).astype(BF16)


def _compress(r, pos, w1, w2):
    _, b, g, n, width = r.shape
    return pl.pallas_call(
        _compress_kernel,
        grid=(2, b, g),
        in_specs=[
            pl.BlockSpec((1, 1, 1, n, width), lambda t, bi, gi: (t, bi, gi, 0, 0)),
            pl.BlockSpec((1, 1, CMP_LEN * NSA_DK), lambda t, bi, gi: (t, 0, 0)),
            pl.BlockSpec((1, CMP_LEN * NSA_DK, CMP_HIDDEN), lambda t, bi, gi: (t, 0, 0)),
            pl.BlockSpec((1, CMP_HIDDEN, NSA_DK), lambda t, bi, gi: (t, 0, 0)),
        ],
        out_specs=pl.BlockSpec((1, 1, 1, n, NSA_DK), lambda t, bi, gi: (t, bi, gi, 0, 0)),
        out_shape=jax.ShapeDtypeStruct((2, b, g, n, NSA_DK), BF16),
        compiler_params=_cparams(("arbitrary", "arbitrary", "arbitrary")),
        name="nsa_compress",
    )(r, pos, w1, w2)


def _nsa_kernel(q_ref, kc_ref, vc_ref, ks_ref, vs_ref, kw_ref, vw_ref, ts_ref, tw_ref, gate_ref,
                selmap_ref, o_ref, qx_ref, m_ref, l_ref, acc_ref, part_ref):
    tq, tk, hg = NSA_TQ, NSA_TK, NSA_GROUP
    rows = hg * tq
    i = pl.program_id(2)
    q0 = i * tq
    qt = q_ref[0]
    q4 = jnp.concatenate([qt[:, h * NSA_DK:(h + 1) * NSA_DK] for h in range(hg)], axis=0)
    q4 = (q4.astype(F32) * (NSA_DK ** -0.5 * LOG2E)).astype(BF16)

    kc = kc_ref[0, 0, 0]
    nc = kc.shape[0]
    cend = lax.broadcasted_iota(jnp.int32, (tq, nc), 1) * CMP_STRIDE + (CMP_LEN - 1)
    hidden = jnp.where(cend <= q0 + lax.broadcasted_iota(jnp.int32, (tq, nc), 0), 0.0, NEG)
    sc = _dot_nt(q4, kc) + jnp.concatenate([hidden] * hg, axis=0)
    mc = jnp.max(sc, axis=1, keepdims=True)
    pc = jnp.exp2(sc - mc)
    norm = jnp.where(mc > 0.5 * NEG,
                     1.0 / jnp.maximum(jnp.sum(pc, axis=1, keepdims=True), TINY), 0.0)
    pc = pc * norm
    o_cmp = _dot(pc.astype(BF16), vc_ref[0, 0, 0])

    pcsum = pc[0:tq] + pc[tq:2 * tq] + pc[2 * tq:3 * tq] + pc[3 * tq:4 * tq]
    imp_t = _dot_nt(selmap_ref[...], pcsum, precision=lax.Precision.HIGHEST)
    nb = imp_t.shape[0]
    blk = lax.broadcasted_iota(jnp.int32, (nb, tq), 0)
    cur = lax.shift_right_logical(q0 + lax.broadcasted_iota(jnp.int32, (nb, tq), 1),
                                  int(math.log2(SLC_LEN)))
    forced = (blk == 0) | (blk == cur) | (blk == cur - 1)
    excluded = -3.0e38
    val = jnp.where(forced | (blk > cur), excluded, imp_t)
    sel = jnp.where(forced, 1.0, 0.0)
    for _ in range(min(SLC_TOPK, nb) - 3):
        best = jnp.max(val, axis=0, keepdims=True)
        first = jnp.min(jnp.where(val == best, blk, nb), axis=0, keepdims=True)
        hit = blk == first
        sel = jnp.where(hit, 1.0, sel)
        val = jnp.where(hit, excluded, val)
    maskbias = jnp.where(sel.T > 0.5, 0.0, NEG).astype(BF16)
    qx_ref[:, 0:NSA_DK] = q4
    qx_ref[:, NSA_DK:NSA_DK + nb] = jnp.concatenate([maskbias] * hg, axis=0)

    kst = pl.multiple_of(jnp.maximum(q0 - WINDOW, 0), tq)
    w0 = pl.multiple_of(WINDOW - (q0 - kst), LANES)
    lw = _dot_nt(q4, kw_ref[0, pl.ds(kst, NSA_WK), :]) + tw_ref[0, :, pl.ds(w0, NSA_WK)]
    pw = jnp.exp2(lw - jnp.max(lw, axis=1, keepdims=True))
    o_win = _dot(pw.astype(BF16), vw_ref[0, pl.ds(kst, NSA_WK), :])
    o_win = o_win / jnp.maximum(jnp.sum(pw, axis=1, keepdims=True), TINY)

    gt = gate_ref[0]
    gcol = lambda c: jnp.concatenate([gt[:, 3 * h + c:3 * h + c + 1] for h in range(hg)], axis=0)
    part_ref[...] = gcol(0) * o_cmp + gcol(2) * o_win

    m_ref[...] = jnp.full(m_ref.shape, NEG, F32)
    l_ref[...] = jnp.zeros(l_ref.shape, F32)
    acc_ref[...] = jnp.zeros(acc_ref.shape, F32)

    def step(j, width, near):
        kstart = pl.multiple_of(j * tk, tk)
        s = _dot_nt(qx_ref[...], ks_ref[0, pl.ds(kstart, width), :])
        if near:
            u0 = pl.multiple_of(TS_OFF - (q0 - j * tk), LANES)
            s = s + ts_ref[0, :, pl.ds(u0, width)]
        _online_softmax_update(s, vs_ref[0, pl.ds(kstart, width), :], m_ref, l_ref, acc_ref)

    jl = lax.shift_right_logical(q0 + tq - 1, int(math.log2(tk)))
    nfar = jnp.maximum(jl - 1, 0)

    def far_body(j, carry):
        step(2 * j, 2 * tk, False)
        return carry

    lax.fori_loop(0, lax.shift_right_logical(nfar, 1), far_body, 0)

    @pl.when(lax.rem(nfar, 2) == 1)
    def _():
        step(nfar - 1, tk, False)

    @pl.when(jl > 0)
    def _():
        step(jl - 1, 2 * tk, True)

    @pl.when(jl == 0)
    def _():
        step(0, tk, True)
    o = part_ref[...] + gcol(1) * (acc_ref[...] / jnp.maximum(l_ref[...], TINY))
    for h in range(hg):
        o_ref[0, :, h * NSA_DV:(h + 1) * NSA_DV] = o[h * tq:(h + 1) * tq].astype(BF16)


def _nsa_attention(proj3, kvc, ksx3, t_sel, t_win, gates3, selmap):
    b, s, _ = proj3.shape
    tq, hg = NSA_TQ, NSA_GROUP
    nc = kvc.shape[3]
    nb = s // SLC_LEN
    qcols = hg * NSA_DK
    return pl.pallas_call(
        _nsa_kernel,
        grid=(b, NSA_KV, s // tq),
        in_specs=[
            pl.BlockSpec((1, tq, qcols), lambda bi, g, i: (bi, i, COL_NQ // qcols + g)),
            pl.BlockSpec((1, 1, 1, nc, NSA_DK), lambda bi, g, i: (0, bi, g, 0, 0)),
            pl.BlockSpec((1, 1, 1, nc, NSA_DV), lambda bi, g, i: (1, bi, g, 0, 0)),
            pl.BlockSpec((1, s, NSA_DK + nb), lambda bi, g, i: (bi, 0, g)),
            pl.BlockSpec((1, s, NSA_DV), lambda bi, g, i: (bi, 0, COL_VS // NSA_DV + g)),
            pl.BlockSpec((1, s, NSA_DK), lambda bi, g, i: (bi, 0, COL_KW // NSA_DK + g)),
            pl.BlockSpec((1, s, NSA_DV), lambda bi, g, i: (bi, 0, COL_VW // NSA_DV + g)),
            pl.BlockSpec((1, hg * tq, TS_W), lambda bi, g, i: (g, 0, 0),
                         pipeline_mode=pl.Buffered(1)),
            pl.BlockSpec((1, hg * tq, TW_W), lambda bi, g, i: (g, 0, 0),
                         pipeline_mode=pl.Buffered(1)),
            pl.BlockSpec((1, tq, LANES), lambda bi, g, i: (bi, i, g)),
            pl.BlockSpec((nb, nc), lambda bi, g, i: (0, 0)),
        ],
        out_specs=pl.BlockSpec((1, tq, hg * NSA_DV), lambda bi, g, i: (bi, i, g)),
        out_shape=jax.ShapeDtypeStruct((b, s, NSA_HEADS * NSA_DV), BF16),
        scratch_shapes=[
            pltpu.VMEM((hg * tq, NSA_DK + nb), BF16),
            pltpu.VMEM((hg * tq, LANES), F32),
            pltpu.VMEM((hg * tq, LANES), F32),
            pltpu.VMEM((hg * tq, NSA_DV), F32),
            pltpu.VMEM((hg * tq, NSA_DV), F32),
        ],
        compiler_params=_cparams(("arbitrary", "arbitrary", "arbitrary")),
        name="nsa_attn",
    )(proj3, kvc, kvc, ksx3, proj3, proj3, proj3, t_sel, t_win, gates3, selmap)


def _outproj_kernel(od_ref, on_ref, x_ref, wd_ref, wn_ref, g_ref, o_ref):
    y = _dot(od_ref[...], wd_ref[...]) + _dot(on_ref[...], wn_ref[...])
    o_ref[...] = x_ref[...] + _rms(y, g_ref[...])


def _outproj(od, on, xf, w_d, w_n, g, tm):
    m = xf.shape[0]
    kd, kn = od.shape[1], on.shape[1]
    return pl.pallas_call(
        _outproj_kernel,
        grid=(m // tm,),
        in_specs=[
            pl.BlockSpec((tm, kd), lambda i: (i, 0)),
            pl.BlockSpec((tm, kn), lambda i: (i, 0)),
            pl.BlockSpec((tm, D_MODEL), lambda i: (i, 0)),
            pl.BlockSpec((kd, D_MODEL), lambda i: (0, 0)),
            pl.BlockSpec((kn, D_MODEL), lambda i: (0, 0)),
            pl.BlockSpec((1, D_MODEL), lambda i: (0, 0)),
        ],
        out_specs=pl.BlockSpec((tm, D_MODEL), lambda i: (i, 0)),
        out_shape=jax.ShapeDtypeStruct((m, D_MODEL), F32),
        compiler_params=_cparams(("arbitrary",)),
        name="outproj",
    )(od, on, xf, w_d, w_n, g)


def _ffn_kernel(x_ref, gpre_ref, wg_ref, wu_ref, cwg_ref, cwu_ref, cbg_ref, cbu_ref, wd_ref,
                gpost_ref, o_ref, h_ref, acc_ref, carry_ref, yg_ref, yu_ref, *, tiles_per_seq):
    i = pl.program_id(0)
    j = pl.program_id(1)
    tm = x_ref.shape[0]
    halo = SUBLANES

    @pl.when(j == 0)
    def _():
        h_ref[...] = _rms(x_ref[...], gpre_ref[...]).astype(BF16)
        acc_ref[...] = jnp.zeros(acc_ref.shape, F32)

    @pl.when(lax.rem(i, tiles_per_seq) == 0)
    def _():
        carry_ref[j] = jnp.zeros(carry_ref.shape[1:], F32)

    def conv(w_ref, cw_ref, cb_ref, slot, y_ref):
        u = _dot(h_ref[...], w_ref[...])
        prev = carry_ref[j, slot]
        carry_ref[j, slot] = u[tm - halo:tm]
        w0, w1, w2 = cw_ref[0:1, :], cw_ref[1:2, :], cw_ref[2:3, :]
        bias = cb_ref[...]

        def taps(z):
            n = z.shape[0]
            return bias + w2 * z + w1 * pltpu.roll(z, 1, axis=0) + w0 * pltpu.roll(z, 2, axis=0)

        y_ref[...] = taps(u)
        y_ref[0:halo, :] = taps(jnp.concatenate([prev, u[0:halo]], axis=0))[halo:2 * halo]

    conv(wg_ref, cwg_ref, cbg_ref, 0, yg_ref)
    conv(wu_ref, cwu_ref, cbu_ref, 1, yu_ref)
    act = (_gelu_tanh(yg_ref[...]) * yu_ref[...]).astype(BF16)
    acc_ref[...] += _dot(act, wd_ref[...])

    @pl.when(j == pl.num_programs(1) - 1)
    def _():
        o_ref[...] = x_ref[...] + _rms(acc_ref[...], gpost_ref[...])


def _ffn(x1, gpre, w_up, conv_w, conv_b, w_down, gpost, tm, tf, seq):
    m = x1.shape[0]
    nj = D_FF // tf
    return pl.pallas_call(
        functools.partial(_ffn_kernel, tiles_per_seq=seq // tm),
        grid=(m // tm, nj),
        in_specs=[
            pl.BlockSpec((tm, D_MODEL), lambda i, j: (i, 0)),
            pl.BlockSpec((1, D_MODEL), lambda i, j: (0, 0)),
            pl.BlockSpec((D_MODEL, tf), lambda i, j: (0, j)),
            pl.BlockSpec((D_MODEL, tf), lambda i, j: (0, j + nj)),
            pl.BlockSpec((CONV_W, tf), lambda i, j: (0, j)),
            pl.BlockSpec((CONV_W, tf), lambda i, j: (0, j + nj)),
            pl.BlockSpec((1, tf), lambda i, j: (0, j)),
            pl.BlockSpec((1, tf), lambda i, j: (0, j + nj)),
            pl.BlockSpec((tf, D_MODEL), lambda i, j: (j, 0)),
            pl.BlockSpec((1, D_MODEL), lambda i, j: (0, 0)),
        ],
        out_specs=pl.BlockSpec((tm, D_MODEL), lambda i, j: (i, 0)),
        out_shape=jax.ShapeDtypeStruct((m, D_MODEL), F32),
        scratch_shapes=[
            pltpu.VMEM((tm, D_MODEL), BF16),
            pltpu.VMEM((tm, D_MODEL), F32),
            pltpu.VMEM((nj, 2, SUBLANES, tf), F32),
            pltpu.VMEM((tm, tf), F32),
            pltpu.VMEM((tm, tf), F32),
        ],
        compiler_params=_cparams(("arbitrary", "arbitrary")),
        name="ffn",
    )(x1, gpre, w_up, w_up, conv_w, conv_w, conv_b, conv_b, w_down, gpost)


def _rel_bucket(delta):
    n = jnp.maximum(delta, 0)
    max_exact = N_BUCKETS // 2
    nf = jnp.maximum(n, 1).astype(F32)
    large = max_exact + (jnp.log(nf / max_exact) / math.log(MAX_DISTANCE / max_exact)
                         * (N_BUCKETS - max_exact)).astype(jnp.int32)
    large = jnp.minimum(large, N_BUCKETS - 1)
    return jnp.where(n < max_exact, n, large)


def _bias_kernel(tab_ref, bkt_ref, o_ref):
    h = pl.program_id(0)
    bkt = bkt_ref[...]
    out = jnp.full(bkt.shape, NEG, F32)
    for bucket in range(N_BUCKETS):
        out = jnp.where(bkt == bucket, tab_ref[bucket, h], out)
    o_ref[0] = out


def _bias_strip(table, dist, hi):
    heads = table.shape[1]
    rows, width = dist.shape
    bkt = jnp.where((dist < 0) | (dist >= hi), -1, _rel_bucket(dist)).astype(jnp.int32)
    tr = LANES
    return pl.pallas_call(
        _bias_kernel,
        grid=(heads, rows // tr),
        in_specs=[
            pl.BlockSpec(memory_space=pltpu.SMEM),
            pl.BlockSpec((tr, width), lambda h, i: (i, 0)),
        ],
        out_specs=pl.BlockSpec((1, tr, width), lambda h, i: (h, i, 0)),
        out_shape=jax.ShapeDtypeStruct((heads, rows, width), F32),
        compiler_params=_cparams(("arbitrary", "arbitrary")),
        name="bias_strip",
    )(table, bkt)


def _selection_map(nb, nc_pad):
    r = SLC_LEN // CMP_STRIDE
    j = jnp.arange(nb)[:, None]
    c = jnp.arange(nc_pad)[None, :]
    off = r * j - c
    out = jnp.zeros((nb, nc_pad), F32)
    for mm in range(r):
        for nn in range(CMP_LEN // CMP_STRIDE):
            out = out + (off == mm + nn).astype(F32)
    return out * (c < nc_pad - 1)


def kernel(x, pre_mix_norm, w_in, lambda_q1, lambda_k1, lambda_q2, lambda_k2, diff_subln,
           cmp_pos_k, cmp_pos_v, cmp_k_w1, cmp_k_w2, cmp_v_w1, cmp_v_w2, rel_bias, w_out,
           post_mix_norm, pre_ffn_norm, w_up, conv_w, conv_b, w_down, post_ffn_norm):
    b, s, d = x.shape
    assert d == D_MODEL and s % DIFF_TQ == 0 and s // SLC_LEN >= SLC_TOPK
    m = b * s
    depth = w_in.shape[0]
    big = 1 << 30
    for l in range(depth):
        xf = x.reshape(m, d)
        w_main = w_in[l][:, :MAIN_COLS].astype(BF16)
        w_gate = jnp.pad(w_in[l][:, MAIN_COLS:].reshape(d, NSA_KV, 3 * NSA_GROUP),
                         ((0, 0), (0, 0), (0, LANES - 3 * NSA_GROUP)))
        w_gate = w_gate.reshape(d, NSA_KV * LANES).astype(BF16)
        proj, gates, ksx = _inproj(xf, pre_mix_norm[l][None], w_main, w_gate, tm=1024, tn=512,
                                   seq=s)
        proj3 = proj.reshape(b, s, MAIN_COLS)

        bias_diff = rel_bias[:, :DIFF_HEADS]
        bias_nsa = rel_bias[:, DIFF_HEADS:]
        qi = jnp.arange(DIFF_TQ)[:, None]
        t_diff = _bias_strip((bias_diff - bias_diff[N_BUCKETS - 1]) * LOG2E,
                             qi - jnp.arange(2 * DIFF_TQ)[None, :] + DIFF_TQ, big)
        qn = jnp.arange(NSA_TQ)[:, None]
        t_sel = _bias_strip((bias_nsa - bias_nsa[N_BUCKETS - 1]) * LOG2E,
                            qn - jnp.arange(TS_W)[None, :] + TS_OFF, big)
        t_win = _bias_strip(bias_nsa * LOG2E, qn - jnp.arange(TW_W)[None, :] + WINDOW, WINDOW)
        t_sel = t_sel.reshape(NSA_KV, NSA_GROUP * NSA_TQ, TS_W)
        t_win = t_win.reshape(NSA_KV, NSA_GROUP * NSA_TQ, TW_W)

        lam_init = 0.8 - 0.6 * math.exp(-0.3 * l)
        o_diff = _diff_attention(proj3, t_diff, lambda_q1[l][None], lambda_k1[l][None],
                                 lambda_q2[l][None], lambda_k2[l][None], diff_subln[l][None],
                                 lam_init)

        nrow = s // CMP_STRIDE

        def rows16(col):
            t = proj3[:, :, col:col + NSA_KV * NSA_DK].reshape(b, nrow, CMP_STRIDE, NSA_KV, NSA_DK)
            return jnp.transpose(t, (0, 3, 1, 2, 4)).reshape(b, NSA_KV, nrow, CMP_STRIDE * NSA_DK)

        r = jnp.stack([rows16(COL_KC), rows16(COL_VC)])
        pos = jnp.stack([cmp_pos_k[l].reshape(1, -1), cmp_pos_v[l].reshape(1, -1)])
        w1 = jnp.stack([cmp_k_w1[l], cmp_v_w1[l]]).astype(BF16)
        w2 = jnp.stack([cmp_k_w2[l], cmp_v_w2[l]]).astype(BF16)
        kvc = _compress(r, pos, w1, w2)

        nb = s // SLC_LEN
        o_nsa = _nsa_attention(proj3, kvc, ksx.reshape(b, s, -1), t_sel, t_win,
                               gates.reshape(b, s, -1), _selection_map(nb, nrow))

        half = DIFF_HEADS * 2 * DIFF_D
        x1 = _outproj(o_diff.reshape(m, -1), o_nsa.reshape(m, -1), xf,
                      w_out[l][:half].astype(BF16), w_out[l][half:].astype(BF16),
                      post_mix_norm[l][None], tm=512)
        x2 = _ffn(x1, pre_ffn_norm[l][None], w_up[l].astype(BF16), conv_w[l], conv_b[l][None],
                  w_down[l].astype(BF16), post_ffn_norm[l][None], tm=512, tf=512, seq=s)
        x = x2.reshape(b, s, d)
    return x
```

```python
import functools
import math

import jax
import jax.numpy as jnp
from jax import lax
from jax.experimental import pallas as pl
from jax.experimental.pallas import tpu as pltpu

F32 = jnp.float32
BF16 = jnp.bfloat16

D_MODEL = 2048
DIFF_HEADS = 8
DIFF_D = 64
NSA_HEADS = 8
NSA_KV = 2
NSA_GROUP = NSA_HEADS // NSA_KV
NSA_DK = 128
NSA_DV = 128
CMP_LEN = 32
CMP_STRIDE = 16
CMP_HIDDEN = 256
SLC_LEN = 64
SLC_TOPK = 16
WINDOW = 512
N_BUCKETS = 32
MAX_DISTANCE = 128
D_FF = 5632
CONV_W = 3
EPS = 1e-6
NEG = -1e30
FORCE = 1e30
TINY = 1e-30
LOG2E = math.log2(math.e)

LANES = 128
SUBLANES = 8
VMEM_LIMIT = 56 * 1024 * 1024

MAIN_COLS = 5632
N_GATES = NSA_HEADS * 3
COL_DQ, COL_DK, COL_DV, COL_NQ = 0, 1024, 2048, 3072
COL_KC, COL_VC, COL_KS, COL_VS, COL_KW, COL_VW = 4096, 4352, 4608, 4864, 5120, 5376

DIFF_TQ = 512
NSA_TQ = 256
FLASH_SUB = 512
NSA_TK = 512
NSA_WK = WINDOW + NSA_TQ
TS_OFF = 2 * NSA_TK - NSA_TQ
TS_W = TS_OFF + NSA_TK
TW_W = WINDOW + NSA_WK


def _cparams(sem):
    return pltpu.CompilerParams(dimension_semantics=sem, vmem_limit_bytes=VMEM_LIMIT)


def _rms(x, g):
    return x * lax.rsqrt(jnp.mean(x * x, axis=-1, keepdims=True) + EPS) * g


def _gelu_tanh(x):
    return 0.5 * x * (1.0 + jnp.tanh(math.sqrt(2.0 / math.pi) * (x + 0.044715 * (x * x * x))))


def _dot(a, b):
    return jnp.dot(a, b, preferred_element_type=F32)


def _dot_nt(a, b, precision=None):
    return lax.dot_general(a, b, (((1,), (1,)), ((), ())), preferred_element_type=F32,
                           precision=precision)


def _lane_tile(a, n):
    return a if n == 1 else jnp.concatenate([a] * n, axis=1)


def _online_softmax_update(s, v, m_ref, l_ref, acc_ref):
    m_prev, l, acc = m_ref[...], l_ref[...], acc_ref[...]
    dv = v.shape[1]
    ones = jnp.ones((FLASH_SUB, LANES), BF16)
    for c in range(0, s.shape[1], FLASH_SUB):
        sc = s[:, c:c + FLASH_SUB]
        m_new = jnp.maximum(m_prev, jnp.max(sc, axis=1, keepdims=True))
        alpha = jnp.exp2(m_prev - m_new)
        p = jnp.exp2((sc - _lane_tile(m_new, FLASH_SUB // LANES)).astype(BF16))
        pv = _dot(p, jnp.concatenate([v[c:c + FLASH_SUB], ones], axis=1))
        l = alpha * l + pv[:, dv:]
        acc = alpha * acc + pv[:, 0:dv]
        m_prev = m_new
    m_ref[...], l_ref[...], acc_ref[...] = m_prev, l, acc


def _inproj_kernel(x_ref, g_ref, w_ref, wg_ref, proj_ref, gates_ref, ksx_ref, r16_ref, h_ref,
                   stage_ref, *, seq):
    i = pl.program_id(0)
    j = pl.program_id(1)
    tm, tn = proj_ref.shape

    @pl.when(j == 0)
    def _():
        h_ref[...] = _rms(x_ref[...], g_ref[...]).astype(BF16)
        gl = _dot(h_ref[...], wg_ref[...])
        gates_ref[...] = 1.0 / (1.0 + jnp.exp(-gl))

    acc = _dot(h_ref[...], w_ref[...])
    proj_ref[...] = acc.astype(BF16)

    @pl.when(j == COL_KS // tn)
    def _():
        nb = seq // SLC_LEN
        xw = NSA_DK + nb
        pos = lax.rem(i * tm + lax.broadcasted_iota(jnp.int32, (tm, nb), 0), seq)
        blk = lax.shift_right_logical(pos, int(math.log2(SLC_LEN)))
        onehot = jnp.where(blk == lax.broadcasted_iota(jnp.int32, (tm, nb), 1), 1.0, 0.0)
        for g in range(NSA_KV):
            ksx_ref[:, g * xw:g * xw + NSA_DK] = acc[:, g * NSA_DK:(g + 1) * NSA_DK].astype(BF16)
            ksx_ref[:, g * xw + NSA_DK:(g + 1) * xw] = onehot.astype(BF16)

    @pl.when(j == COL_KC // tn)
    def _():
        nr = tm // CMP_STRIDE
        for t in range(2):
            for g in range(NSA_KV):
                col = (t * NSA_KV + g) * NSA_DK
                stage_ref[...] = acc[:, col:col + NSA_DK]
                for p in range(CMP_STRIDE):
                    r16_ref[t, g, :, p * NSA_DK:(p + 1) * NSA_DK] = (
                        stage_ref[pl.ds(p, nr, stride=CMP_STRIDE), :].astype(BF16))


def _inproj(xf, g, w_main, w_gate, tm, tn, seq):
    m = xf.shape[0]
    assert COL_KS % tn == 0 and COL_KC % tn == 0 and tn == 2 * NSA_KV * NSA_DK
    xw = NSA_DK + seq // SLC_LEN
    ng = w_gate.shape[1]
    return pl.pallas_call(
        functools.partial(_inproj_kernel, seq=seq),
        grid=(m // tm, MAIN_COLS // tn),
        in_specs=[
            pl.BlockSpec((tm, D_MODEL), lambda i, j: (i, 0)),
            pl.BlockSpec((1, D_MODEL), lambda i, j: (0, 0)),
            pl.BlockSpec((D_MODEL, tn), lambda i, j: (0, j)),
            pl.BlockSpec((D_MODEL, ng), lambda i, j: (0, 0)),
        ],
        out_specs=[
            pl.BlockSpec((tm, tn), lambda i, j: (i, j)),
            pl.BlockSpec((tm, ng), lambda i, j: (i, 0)),
            pl.BlockSpec((tm, NSA_KV * xw), lambda i, j: (i, 0)),
            pl.BlockSpec((2, NSA_KV, tm // CMP_STRIDE, CMP_STRIDE * NSA_DK),
                         lambda i, j: (0, 0, i, 0)),
        ],
        out_shape=[
            jax.ShapeDtypeStruct((m, MAIN_COLS), BF16),
            jax.ShapeDtypeStruct((m, ng), F32),
            jax.ShapeDtypeStruct((m, NSA_KV * xw), BF16),
            jax.ShapeDtypeStruct((2, NSA_KV, m // CMP_STRIDE, CMP_STRIDE * NSA_DK), BF16),
        ],
        scratch_shapes=[pltpu.VMEM((tm, D_MODEL), BF16), pltpu.VMEM((tm, NSA_DK), F32)],
        compiler_params=_cparams(("arbitrary", "arbitrary")),
        name="inproj",
    )(xf, g, w_main, w_gate)


def _diff_kernel(q_ref, k_ref, v_ref, t_ref, lq1_ref, lk1_ref, lq2_ref, lk2_ref, sub_ref,
                 o_ref, qs_ref, m_ref, l_ref, acc_ref, *, lam_init):
    tq = DIFF_TQ
    i = pl.program_id(2)
    q = q_ref[0].astype(F32) * (DIFF_D ** -0.5 * LOG2E)
    lane = lax.broadcasted_iota(jnp.int32, q.shape, 1)
    qs_ref[...] = jnp.concatenate(
        [jnp.where(lane < DIFF_D, q, 0.0), jnp.where(lane >= DIFF_D, q, 0.0)], axis=0).astype(BF16)
    m_ref[...] = jnp.full(m_ref.shape, NEG, F32)
    l_ref[...] = jnp.zeros(l_ref.shape, F32)
    acc_ref[...] = jnp.zeros(acc_ref.shape, F32)

    def step(kstart, width, bias):
        k = k_ref[0, pl.ds(kstart, width), :]
        s = _dot_nt(qs_ref[...], k)
        if bias is not None:
            s = s + jnp.concatenate([bias, bias], axis=0)
        _online_softmax_update(s, v_ref[0, pl.ds(kstart, width), :], m_ref, l_ref, acc_ref)

    nfar = jnp.maximum(i - 1, 0)

    def far_body(j, carry):
        step(pl.multiple_of(j * (2 * tq), 2 * tq), 2 * tq, None)
        return carry

    lax.fori_loop(0, lax.shift_right_logical(nfar, 1), far_body, 0)

    @pl.when(lax.rem(nfar, 2) == 1)
    def _():
        step(pl.multiple_of((nfar - 1) * tq, tq), tq, None)

    @pl.when(i > 0)
    def _():
        step(pl.multiple_of((i - 1) * tq, tq), 2 * tq, t_ref[0])

    @pl.when(i == 0)
    def _():
        step(0, tq, t_ref[0, :, tq:2 * tq])

    o = acc_ref[...] / jnp.maximum(l_ref[...], TINY)
    lam = (jnp.exp(jnp.sum(lq1_ref[...] * lk1_ref[...], axis=1, keepdims=True))
           - jnp.exp(jnp.sum(lq2_ref[...] * lk2_ref[...], axis=1, keepdims=True)) + lam_init)
    a = o[0:tq] - lam * o[tq:2 * tq]
    o_ref[0] = (_rms(a, sub_ref[...]) * (1.0 - lam_init)).astype(BF16)


def _diff_attention(proj3, t_diff, lq1, lk1, lq2, lk2, subln, lam_init):
    b, s, _ = proj3.shape
    tq = DIFF_TQ
    vec = lambda n: pl.BlockSpec((1, n), lambda bi, h, i: (0, 0))
    return pl.pallas_call(
        functools.partial(_diff_kernel, lam_init=lam_init),
        grid=(b, DIFF_HEADS, s // tq),
        in_specs=[
            pl.BlockSpec((1, tq, LANES), lambda bi, h, i: (bi, i, COL_DQ // LANES + h)),
            pl.BlockSpec((1, s, LANES), lambda bi, h, i: (bi, 0, COL_DK // LANES + h)),
            pl.BlockSpec((1, s, LANES), lambda bi, h, i: (bi, 0, COL_DV // LANES + h)),
            pl.BlockSpec((1, tq, 2 * tq), lambda bi, h, i: (h, 0, 0)),
            vec(DIFF_D), vec(DIFF_D), vec(DIFF_D), vec(DIFF_D), vec(2 * DIFF_D),
        ],
        out_specs=pl.BlockSpec((1, tq, LANES), lambda bi, h, i: (bi, i, h)),
        out_shape=jax.ShapeDtypeStruct((b, s, DIFF_HEADS * 2 * DIFF_D), BF16),
        scratch_shapes=[
            pltpu.VMEM((2 * tq, LANES), BF16),
            pltpu.VMEM((2 * tq, LANES), F32),
            pltpu.VMEM((2 * tq, LANES), F32),
            pltpu.VMEM((2 * tq, LANES), F32),
        ],
        compiler_params=_cparams(("arbitrary", "arbitrary", "arbitrary")),
        name="diff_attn",
    )(proj3, proj3, proj3, t_diff, lq1, lk1, lq2, lk2, subln)


def _compress_kernel(r_ref, pos_ref, w1_ref, w2_ref, o_ref):
    half = CMP_STRIDE * NSA_DK
    r = r_ref[0, 0, 0]
    n = r.shape[0]
    a = _dot(r, w1_ref[0, 0:half, :])
    bm = _dot(r, w1_ref[0, half:2 * half, :])
    posb = jnp.broadcast_to(pos_ref[0], (SUBLANES, 2 * half)).astype(BF16)
    pt = _dot(posb, w1_ref[0])[0:1]
    hid = a + pltpu.roll(bm, n - 1, axis=0) + pt
    o_ref[0, 0, 0] = _dot(_gelu_tanh(hid).astype(BF16), w2_ref[0]).astype(BF16)


def _compress(r, pos, w1, w2):
    _, g, b, n, width = r.shape
    return pl.pallas_call(
        _compress_kernel,
        grid=(2, b, g),
        in_specs=[
            pl.BlockSpec((1, 1, 1, n, width), lambda t, bi, gi: (t, gi, bi, 0, 0)),
            pl.BlockSpec((1, 1, CMP_LEN * NSA_DK), lambda t, bi, gi: (t, 0, 0)),
            pl.BlockSpec((1, CMP_LEN * NSA_DK, CMP_HIDDEN), lambda t, bi, gi: (t, 0, 0)),
            pl.BlockSpec((1, CMP_HIDDEN, NSA_DK), lambda t, bi, gi: (t, 0, 0)),
        ],
        out_specs=pl.BlockSpec((1, 1, 1, n, NSA_DK), lambda t, bi, gi: (t, bi, gi, 0, 0)),
        out_shape=jax.ShapeDtypeStruct((2, b, g, n, NSA_DK), BF16),
        compiler_params=_cparams(("arbitrary", "arbitrary", "arbitrary")),
        name="nsa_compress",
    )(r, pos, w1, w2)


def _nsa_kernel(q_ref, kc_ref, vc_ref, ks_ref, vs_ref, kw_ref, vw_ref, ts_ref, tw_ref, gate_ref,
                selmap_ref, o_ref, qx_ref, m_ref, l_ref, acc_ref, part_ref):
    tq, tk, hg = NSA_TQ, NSA_TK, NSA_GROUP
    rows = hg * tq
    i = pl.program_id(2)
    q0 = i * tq
    qt = q_ref[0]
    q4 = jnp.concatenate([qt[:, h * NSA_DK:(h + 1) * NSA_DK] for h in range(hg)], axis=0)
    q4 = (q4.astype(F32) * (NSA_DK ** -0.5 * LOG2E)).astype(BF16)

    kc = kc_ref[0, 0, 0]
    nc = kc.shape[0]
    cend = lax.broadcasted_iota(jnp.int32, (tq, nc), 1) * CMP_STRIDE + (CMP_LEN - 1)
    hidden = jnp.where(cend <= q0 + lax.broadcasted_iota(jnp.int32, (tq, nc), 0), 0.0, NEG)
    sc = _dot_nt(q4, kc) + jnp.concatenate([hidden] * hg, axis=0)
    mc = jnp.max(sc, axis=1, keepdims=True)
    pc = jnp.exp2(sc - mc)
    norm = jnp.where(mc > 0.5 * NEG,
                     1.0 / jnp.maximum(jnp.sum(pc, axis=1, keepdims=True), TINY), 0.0)
    pc = pc * norm
    o_cmp = _dot(pc.astype(BF16), vc_ref[0, 0, 0])

    pcsum = pc[0:tq] + pc[tq:2 * tq] + pc[2 * tq:3 * tq] + pc[3 * tq:4 * tq]
    imp_t = _dot_nt(selmap_ref[...], pcsum, precision=lax.Precision.HIGHEST)
    nb = imp_t.shape[0]
    blk = lax.broadcasted_iota(jnp.int32, (nb, tq), 0)
    cur = lax.shift_right_logical(q0 + lax.broadcasted_iota(jnp.int32, (nb, tq), 1),
                                  int(math.log2(SLC_LEN)))
    forced = (blk == 0) | (blk == cur) | (blk == cur - 1)
    excluded = -3.0e38
    val = jnp.where(forced | (blk > cur), excluded, imp_t)
    sel = jnp.where(forced, 1.0, 0.0)
    for _ in range(min(SLC_TOPK, nb) - 3):
        best = jnp.max(val, axis=0, keepdims=True)
        first = jnp.min(jnp.where(val == best, blk, nb), axis=0, keepdims=True)
        hit = blk == first
        sel = jnp.where(hit, 1.0, sel)
        val = jnp.where(hit, excluded, val)
    maskbias = jnp.where(sel.T > 0.5, 0.0, NEG).astype(BF16)
    qx_ref[:, 0:NSA_DK] = q4
    qx_ref[:, NSA_DK:NSA_DK + nb] = jnp.concatenate([maskbias] * hg, axis=0)

    kst = pl.multiple_of(jnp.maximum(q0 - WINDOW, 0), tq)
    w0 = pl.multiple_of(WINDOW - (q0 - kst), LANES)
    lw = _dot_nt(q4, kw_ref[0, pl.ds(kst, NSA_WK), :]) + tw_ref[0, :, pl.ds(w0, NSA_WK)]
    pw = jnp.exp2(lw - jnp.max(lw, axis=1, keepdims=True))
    o_win = _dot(pw.astype(BF16), vw_ref[0, pl.ds(kst, NSA_WK), :])
    o_win = o_win / jnp.maximum(jnp.sum(pw, axis=1, keepdims=True), TINY)

    gt = gate_ref[0]
    gcol = lambda c: jnp.concatenate([gt[:, 3 * h + c:3 * h + c + 1] for h in range(hg)], axis=0)
    part_ref[...] = gcol(0) * o_cmp + gcol(2) * o_win

    m_ref[...] = jnp.full(m_ref.shape, NEG, F32)
    l_ref[...] = jnp.zeros(l_ref.shape, F32)
    acc_ref[...] = jnp.zeros(acc_ref.shape, F32)

    def step(j, width, near):
        kstart = pl.multiple_of(j * tk, tk)
        s = _dot_nt(qx_ref[...], ks_ref[0, pl.ds(kstart, width), :])
        if near:
            u0 = pl.multiple_of(TS_OFF - (q0 - j * tk), LANES)
            s = s + ts_ref[0, :, pl.ds(u0, width)]
        _online_softmax_update(s, vs_ref[0, pl.ds(kstart, width), :], m_ref, l_ref, acc_ref)

    jl = lax.shift_right_logical(q0 + tq - 1, int(math.log2(tk)))
    nfar = jnp.maximum(jl - 1, 0)

    def far_body(j, carry):
        step(2 * j, 2 * tk, False)
        return carry

    lax.fori_loop(0, lax.shift_right_logical(nfar, 1), far_body, 0)

    @pl.when(lax.rem(nfar, 2) == 1)
    def _():
        step(nfar - 1, tk, False)

    @pl.when(jl > 0)
    def _():
        step(jl - 1, 2 * tk, True)

    @pl.when(jl == 0)
    def _():
        step(0, tk, True)
    o = part_ref[...] + gcol(1) * (acc_ref[...] / jnp.maximum(l_ref[...], TINY))
    for h in range(hg):
        o_ref[0, :, h * NSA_DV:(h + 1) * NSA_DV] = o[h * tq:(h + 1) * tq].astype(BF16)


def _nsa_attention(proj3, kvc, ksx3, t_sel, t_win, gates3, selmap):
    b, s, _ = proj3.shape
    tq, hg = NSA_TQ, NSA_GROUP
    nc = kvc.shape[3]
    nb = s // SLC_LEN
    qcols = hg * NSA_DK
    return pl.pallas_call(
        _nsa_kernel,
        grid=(b, NSA_KV, s // tq),
        in_specs=[
            pl.BlockSpec((1, tq, qcols), lambda bi, g, i: (bi, i, COL_NQ // qcols + g)),
            pl.BlockSpec((1, 1, 1, nc, NSA_DK), lambda bi, g, i: (0, bi, g, 0, 0)),
            pl.BlockSpec((1, 1, 1, nc, NSA_DV), lambda bi, g, i: (1, bi, g, 0, 0)),
            pl.BlockSpec((1, s, NSA_DK + nb), lambda bi, g, i: (bi, 0, g)),
            pl.BlockSpec((1, s, NSA_DV), lambda bi, g, i: (bi, 0, COL_VS // NSA_DV + g)),
            pl.BlockSpec((1, s, NSA_DK), lambda bi, g, i: (bi, 0, COL_KW // NSA_DK + g)),
            pl.BlockSpec((1, s, NSA_DV), lambda bi, g, i: (bi, 0, COL_VW // NSA_DV + g)),
            pl.BlockSpec((1, hg * tq, TS_W), lambda bi, g, i: (g, 0, 0),
                         pipeline_mode=pl.Buffered(1)),
            pl.BlockSpec((1, hg * tq, TW_W), lambda bi, g, i: (g, 0, 0),
                         pipeline_mode=pl.Buffered(1)),
            pl.BlockSpec((1, tq, LANES), lambda bi, g, i: (bi, i, g)),
            pl.BlockSpec((nb, nc), lambda bi, g, i: (0, 0)),
        ],
        out_specs=pl.BlockSpec((1, tq, hg * NSA_DV), lambda bi, g, i: (bi, i, g)),
        out_shape=jax.ShapeDtypeStruct((b, s, NSA_HEADS * NSA_DV), BF16),
        scratch_shapes=[
            pltpu.VMEM((hg * tq, NSA_DK + nb), BF16),
            pltpu.VMEM((hg * tq, LANES), F32),
            pltpu.VMEM((hg * tq, LANES), F32),
            pltpu.VMEM((hg * tq, NSA_DV), F32),
            pltpu.VMEM((hg * tq, NSA_DV), F32),
        ],
        compiler_params=_cparams(("arbitrary", "arbitrary", "arbitrary")),
        name="nsa_attn",
    )(proj3, kvc, kvc, ksx3, proj3, proj3, proj3, t_sel, t_win, gates3, selmap)


def _outproj_kernel(od_ref, on_ref, x_ref, wd_ref, wn_ref, g_ref, o_ref):
    y = _dot(od_ref[...], wd_ref[...]) + _dot(on_ref[...], wn_ref[...])
    o_ref[...] = x_ref[...] + _rms(y, g_ref[...])


def _outproj(od, on, xf, w_d, w_n, g, tm):
    m = xf.shape[0]
    kd, kn = od.shape[1], on.shape[1]
    return pl.pallas_call(
        _outproj_kernel,
        grid=(m // tm,),
        in_specs=[
            pl.BlockSpec((tm, kd), lambda i: (i, 0)),
            pl.BlockSpec((tm, kn), lambda i: (i, 0)),
            pl.BlockSpec((tm, D_MODEL), lambda i: (i, 0)),
            pl.BlockSpec((kd, D_MODEL), lambda i: (0, 0)),
            pl.BlockSpec((kn, D_MODEL), lambda i: (0, 0)),
            pl.BlockSpec((1, D_MODEL), lambda i: (0, 0)),
        ],
        out_specs=pl.BlockSpec((tm, D_MODEL), lambda i: (i, 0)),
        out_shape=jax.ShapeDtypeStruct((m, D_MODEL), F32),
        compiler_params=_cparams(("arbitrary",)),
        name="outproj",
    )(od, on, xf, w_d, w_n, g)


def _ffn_kernel(x_ref, gpre_ref, wg_ref, wu_ref, cwg_ref, cwu_ref, cbg_ref, cbu_ref, wd_ref,
                gpost_ref, o_ref, h_ref, acc_ref, carry_ref, yg_ref, yu_ref, *, tiles_per_seq):
    i = pl.program_id(0)
    j = pl.program_id(1)
    tm = x_ref.shape[0]
    halo = SUBLANES

    @pl.when(j == 0)
    def _():
        h_ref[...] = _rms(x_ref[...], gpre_ref[...]).astype(BF16)
        acc_ref[...] = jnp.zeros(acc_ref.shape, F32)

    @pl.when(lax.rem(i, tiles_per_seq) == 0)
    def _():
        carry_ref[j] = jnp.zeros(carry_ref.shape[1:], F32)

    def conv(w_ref, cw_ref, cb_ref, slot, y_ref):
        u = _dot(h_ref[...], w_ref[...])
        prev = carry_ref[j, slot]
        carry_ref[j, slot] = u[tm - halo:tm]
        w0, w1, w2 = cw_ref[0:1, :], cw_ref[1:2, :], cw_ref[2:3, :]
        bias = cb_ref[...]

        def taps(z):
            n = z.shape[0]
            return bias + w2 * z + w1 * pltpu.roll(z, 1, axis=0) + w0 * pltpu.roll(z, 2, axis=0)

        y_ref[...] = taps(u)
        y_ref[0:halo, :] = taps(jnp.concatenate([prev, u[0:halo]], axis=0))[halo:2 * halo]

    conv(wg_ref, cwg_ref, cbg_ref, 0, yg_ref)
    conv(wu_ref, cwu_ref, cbu_ref, 1, yu_ref)
    act = (_gelu_tanh(yg_ref[...]) * yu_ref[...]).astype(BF16)
    acc_ref[...] += _dot(act, wd_ref[...])

    @pl.when(j == pl.num_programs(1) - 1)
    def _():
        o_ref[...] = x_ref[...] + _rms(acc_ref[...], gpost_ref[...])


def _ffn(x1, gpre, w_up, conv_w, conv_b, w_down, gpost, tm, tf, seq):
    m = x1.shape[0]
    nj = D_FF // tf
    return pl.pallas_call(
        functools.partial(_ffn_kernel, tiles_per_seq=seq // tm),
        grid=(m // tm, nj),
        in_specs=[
            pl.BlockSpec((tm, D_MODEL), lambda i, j: (i, 0)),
            pl.BlockSpec((1, D_MODEL), lambda i, j: (0, 0)),
            pl.BlockSpec((D_MODEL, tf), lambda i, j: (0, j)),
            pl.BlockSpec((D_MODEL, tf), lambda i, j: (0, j + nj)),
            pl.BlockSpec((CONV_W, tf), lambda i, j: (0, j)),
            pl.BlockSpec((CONV_W, tf), lambda i, j: (0, j + nj)),
            pl.BlockSpec((1, tf), lambda i, j: (0, j)),
            pl.BlockSpec((1, tf), lambda i, j: (0, j + nj)),
            pl.BlockSpec((tf, D_MODEL), lambda i, j: (j, 0)),
            pl.BlockSpec((1, D_MODEL), lambda i, j: (0, 0)),
        ],
        out_specs=pl.BlockSpec((tm, D_MODEL), lambda i, j: (i, 0)),
        out_shape=jax.ShapeDtypeStruct((m, D_MODEL), F32),
        scratch_shapes=[
            pltpu.VMEM((tm, D_MODEL), BF16),
            pltpu.VMEM((tm, D_MODEL), F32),
            pltpu.VMEM((nj, 2, SUBLANES, tf), F32),
            pltpu.VMEM((tm, tf), F32),
            pltpu.VMEM((tm, tf), F32),
        ],
        compiler_params=_cparams(("arbitrary", "arbitrary")),
        name="ffn",
    )(x1, gpre, w_up, w_up, conv_w, conv_w, conv_b, conv_b, w_down, gpost)


def _rel_bucket(delta):
    n = jnp.maximum(delta, 0)
    max_exact = N_BUCKETS // 2
    nf = jnp.maximum(n, 1).astype(F32)
    large = max_exact + (jnp.log(nf / max_exact) / math.log(MAX_DISTANCE / max_exact)
                         * (N_BUCKETS - max_exact)).astype(jnp.int32)
    large = jnp.minimum(large, N_BUCKETS - 1)
    return jnp.where(n < max_exact, n, large)


def _bias_kernel(tab_ref, bkt_ref, o_ref):
    h = pl.program_id(0)
    bkt = bkt_ref[...]
    out = jnp.full(bkt.shape, NEG, F32)
    for bucket in range(N_BUCKETS):
        out = jnp.where(bkt == bucket, tab_ref[bucket, h], out)
    o_ref[0] = out


def _bias_strip(table, dist, hi):
    heads = table.shape[1]
    rows, width = dist.shape
    bkt = jnp.where((dist < 0) | (dist >= hi), -1, _rel_bucket(dist)).astype(jnp.int32)
    tr = LANES
    return pl.pallas_call(
        _bias_kernel,
        grid=(heads, rows // tr),
        in_specs=[
            pl.BlockSpec(memory_space=pltpu.SMEM),
            pl.BlockSpec((tr, width), lambda h, i: (i, 0)),
        ],
        out_specs=pl.BlockSpec((1, tr, width), lambda h, i: (h, i, 0)),
        out_shape=jax.ShapeDtypeStruct((heads, rows, width), F32),
        compiler_params=_cparams(("arbitrary", "arbitrary")),
        name="bias_strip",
    )(table, bkt)


def _selection_map(nb, nc_pad):
    r = SLC_LEN // CMP_STRIDE
    j = jnp.arange(nb)[:, None]
    c = jnp.arange(nc_pad)[None, :]
    off = r * j - c
    out = jnp.zeros((nb, nc_pad), F32)
    for mm in range(r):
        for nn in range(CMP_LEN // CMP_STRIDE):
            out = out + (off == mm + nn).astype(F32)
    return out * (c < nc_pad - 1)


def kernel(x, pre_mix_norm, w_in, lambda_q1, lambda_k1, lambda_q2, lambda_k2, diff_subln,
           cmp_pos_k, cmp_pos_v, cmp_k_w1, cmp_k_w2, cmp_v_w1, cmp_v_w2, rel_bias, w_out,
           post_mix_norm, pre_ffn_norm, w_up, conv_w, conv_b, w_down, post_ffn_norm):
    b, s, d = x.shape
    assert d == D_MODEL and s % DIFF_TQ == 0 and s // SLC_LEN >= SLC_TOPK
    m = b * s
    depth = w_in.shape[0]
    big = 1 << 30
    for l in range(depth):
        xf = x.reshape(m, d)
        w_main = w_in[l][:, :MAIN_COLS].astype(BF16)
        w_gate = jnp.pad(w_in[l][:, MAIN_COLS:].reshape(d, NSA_KV, 3 * NSA_GROUP),
                         ((0, 0), (0, 0), (0, LANES - 3 * NSA_GROUP)))
        w_gate = w_gate.reshape(d, NSA_KV * LANES).astype(BF16)
        proj, gates, ksx, rows16 = _inproj(xf, pre_mix_norm[l][None], w_main, w_gate, tm=1024,
                                           tn=512, seq=s)
        proj3 = proj.reshape(b, s, MAIN_COLS)

        bias_diff = rel_bias[:, :DIFF_HEADS]
        bias_nsa = rel_bias[:, DIFF_HEADS:]
        qi = jnp.arange(DIFF_TQ)[:, None]
        t_diff = _bias_strip((bias_diff - bias_diff[N_BUCKETS - 1]) * LOG2E,
                             qi - jnp.arange(2 * DIFF_TQ)[None, :] + DIFF_TQ, big)
        qn = jnp.arange(NSA_TQ)[:, None]
        t_sel = _bias_strip((bias_nsa - bias_nsa[N_BUCKETS - 1]) * LOG2E,
                            qn - jnp.arange(TS_W)[None, :] + TS_OFF, big)
        t_win = _bias_strip(bias_nsa * LOG2E, qn - jnp.arange(TW_W)[None, :] + WINDOW, WINDOW)
        t_sel = t_sel.reshape(NSA_KV, NSA_GROUP * NSA_TQ, TS_W)
        t_win = t_win.reshape(NSA_KV, NSA_GROUP * NSA_TQ, TW_W)

        lam_init = 0.8 - 0.6 * math.exp(-0.3 * l)
        o_diff = _diff_attention(proj3, t_diff, lambda_q1[l][None], lambda_k1[l][None],
                                 lambda_q2[l][None], lambda_k2[l][None], diff_subln[l][None],
                                 lam_init)

        nrow = s // CMP_STRIDE

        r = rows16.reshape(2, NSA_KV, b, nrow, CMP_STRIDE * NSA_DK)
        pos = jnp.stack([cmp_pos_k[l].reshape(1, -1), cmp_pos_v[l].reshape(1, -1)])
        w1 = jnp.stack([cmp_k_w1[l], cmp_v_w1[l]]).astype(BF16)
        w2 = jnp.stack([cmp_k_w2[l], cmp_v_w2[l]]).astype(BF16)
        kvc = _compress(r, pos, w1, w2)

        nb = s // SLC_LEN
        o_nsa = _nsa_attention(proj3, kvc, ksx.reshape(b, s, -1), t_sel, t_win,
                               gates.reshape(b, s, -1), _selection_map(nb, nrow))

        half = DIFF_HEADS * 2 * DIFF_D
        x1 = _outproj(o_diff.reshape(m, -1), o_nsa.reshape(m, -1), xf,
                      w_out[l][:half].astype(BF16), w_out[l][half:].astype(BF16),
                      post_mix_norm[l][None], tm=512)
        x2 = _ffn(x1, pre_ffn_norm[l][None], w_up[l].astype(BF16), conv_w[l], conv_b[l][None],
                  w_down[l].astype(BF16), post_ffn_norm[l][None], tm=512, tf=512, seq=s)
        x = x2.reshape(b, s, d)
    return x
```

```python
import functools
import math

import jax
import jax.numpy as jnp
from jax import lax
from jax.experimental import pallas as pl
from jax.experimental.pallas import tpu as pltpu

F32 = jnp.float32
BF16 = jnp.bfloat16

D_MODEL = 2048
DIFF_HEADS = 8
DIFF_D = 64
NSA_HEADS = 8
NSA_KV = 2
NSA_GROUP = NSA_HEADS // NSA_KV
NSA_DK = 128
NSA_DV = 128
CMP_LEN = 32
CMP_STRIDE = 16
CMP_HIDDEN = 256
SLC_LEN = 64
SLC_TOPK = 16
WINDOW = 512
N_BUCKETS = 32
MAX_DISTANCE = 128
D_FF = 5632
CONV_W = 3
EPS = 1e-6
NEG = -1e30
FORCE = 1e30
TINY = 1e-30
LOG2E = math.log2(math.e)

LANES = 128
SUBLANES = 8
VMEM_LIMIT = 56 * 1024 * 1024

MAIN_COLS = 5632
N_GATES = NSA_HEADS * 3
COL_DQ, COL_DK, COL_DV, COL_NQ = 0, 1024, 2048, 3072
COL_KC, COL_VC, COL_KS, COL_VS, COL_KW, COL_VW = 4096, 4352, 4608, 4864, 5120, 5376

DIFF_TQ = 512
NSA_TQ = 256
FLASH_SUB = 512
NSA_TK = 512
NSA_WK = WINDOW + NSA_TQ
TS_OFF = 2 * NSA_TK - NSA_TQ
TS_W = TS_OFF + NSA_TK
TW_W = WINDOW + NSA_WK


def _cparams(sem):
    return pltpu.CompilerParams(dimension_semantics=sem, vmem_limit_bytes=VMEM_LIMIT)


def _rms(x, g):
    return x * lax.rsqrt(jnp.mean(x * x, axis=-1, keepdims=True) + EPS) * g


def _gelu_tanh(x):
    return 0.5 * x * (1.0 + jnp.tanh(math.sqrt(2.0 / math.pi) * (x + 0.044715 * (x * x * x))))


def _dot(a, b):
    return jnp.dot(a, b, preferred_element_type=F32)


def _dot_nt(a, b, precision=None):
    return lax.dot_general(a, b, (((1,), (1,)), ((), ())), preferred_element_type=F32,
                           precision=precision)


def _lane_tile(a, n):
    return a if n == 1 else jnp.concatenate([a] * n, axis=1)


def _online_softmax_update(s, v, m_ref, l_ref, acc_ref, bias=None):
    m_prev, l, acc = m_ref[...], l_ref[...], acc_ref[...]
    dv = v.shape[1]
    ones = jnp.ones((FLASH_SUB, LANES), BF16)
    for c in range(0, s.shape[1], FLASH_SUB):
        sc = s[:, c:c + FLASH_SUB]
        extra = None if bias is None else bias(c)
        if extra is not None:
            sc = sc + extra
        m_new = jnp.maximum(m_prev, jnp.max(sc, axis=1, keepdims=True))
        alpha = jnp.exp2(m_prev - m_new)
        p = jnp.exp2((sc - _lane_tile(m_new, FLASH_SUB // LANES)).astype(BF16))
        pv = _dot(p, jnp.concatenate([v[c:c + FLASH_SUB], ones], axis=1))
        l = alpha * l + pv[:, dv:]
        acc = alpha * acc + pv[:, 0:dv]
        m_prev = m_new
    m_ref[...], l_ref[...], acc_ref[...] = m_prev, l, acc


def _inproj_kernel(x_ref, g_ref, w_ref, wg_ref, proj_ref, gates_ref, ksx_ref, r16_ref, h_ref,
                   stage_ref, *, seq):
    i = pl.program_id(0)
    j = pl.program_id(1)
    tm, tn = proj_ref.shape

    @pl.when(j == 0)
    def _():
        h_ref[...] = _rms(x_ref[...], g_ref[...]).astype(BF16)
        gl = _dot(h_ref[...], wg_ref[...])
        gates_ref[...] = 1.0 / (1.0 + jnp.exp(-gl))

    acc = _dot(h_ref[...], w_ref[...])
    proj_ref[...] = acc.astype(BF16)

    @pl.when(j == COL_KS // tn)
    def _():
        nb = seq // SLC_LEN
        xw = NSA_DK + nb
        pos = lax.rem(i * tm + lax.broadcasted_iota(jnp.int32, (tm, nb), 0), seq)
        blk = lax.shift_right_logical(pos, int(math.log2(SLC_LEN)))
        onehot = jnp.where(blk == lax.broadcasted_iota(jnp.int32, (tm, nb), 1), 1.0, 0.0)
        for g in range(NSA_KV):
            ksx_ref[:, g * xw:g * xw + NSA_DK] = acc[:, g * NSA_DK:(g + 1) * NSA_DK].astype(BF16)
            ksx_ref[:, g * xw + NSA_DK:(g + 1) * xw] = onehot.astype(BF16)

    @pl.when(j == COL_KC // tn)
    def _():
        nr = tm // CMP_STRIDE
        for t in range(2):
            for g in range(NSA_KV):
                col = (t * NSA_KV + g) * NSA_DK
                stage_ref[...] = acc[:, col:col + NSA_DK]
                for p in range(CMP_STRIDE):
                    r16_ref[t, g, :, p * NSA_DK:(p + 1) * NSA_DK] = (
                        stage_ref[pl.ds(p, nr, stride=CMP_STRIDE), :].astype(BF16))


def _inproj(xf, g, w_main, w_gate, tm, tn, seq):
    m = xf.shape[0]
    assert COL_KS % tn == 0 and COL_KC % tn == 0 and tn == 2 * NSA_KV * NSA_DK
    xw = NSA_DK + seq // SLC_LEN
    ng = w_gate.shape[1]
    return pl.pallas_call(
        functools.partial(_inproj_kernel, seq=seq),
        grid=(m // tm, MAIN_COLS // tn),
        in_specs=[
            pl.BlockSpec((tm, D_MODEL), lambda i, j: (i, 0)),
            pl.BlockSpec((1, D_MODEL), lambda i, j: (0, 0)),
            pl.BlockSpec((D_MODEL, tn), lambda i, j: (0, j)),
            pl.BlockSpec((D_MODEL, ng), lambda i, j: (0, 0)),
        ],
        out_specs=[
            pl.BlockSpec((tm, tn), lambda i, j: (i, j)),
            pl.BlockSpec((tm, ng), lambda i, j: (i, 0)),
            pl.BlockSpec((tm, NSA_KV * xw), lambda i, j: (i, 0)),
            pl.BlockSpec((2, NSA_KV, tm // CMP_STRIDE, CMP_STRIDE * NSA_DK),
                         lambda i, j: (0, 0, i, 0)),
        ],
        out_shape=[
            jax.ShapeDtypeStruct((m, MAIN_COLS), BF16),
            jax.ShapeDtypeStruct((m, ng), F32),
            jax.ShapeDtypeStruct((m, NSA_KV * xw), BF16),
            jax.ShapeDtypeStruct((2, NSA_KV, m // CMP_STRIDE, CMP_STRIDE * NSA_DK), BF16),
        ],
        scratch_shapes=[pltpu.VMEM((tm, D_MODEL), BF16), pltpu.VMEM((tm, NSA_DK), F32)],
        compiler_params=_cparams(("arbitrary", "arbitrary")),
        name="inproj",
    )(xf, g, w_main, w_gate)


def _diff_kernel(q_ref, k_ref, v_ref, t_ref, lq1_ref, lk1_ref, lq2_ref, lk2_ref, sub_ref,
                 o_ref, qs_ref, m_ref, l_ref, acc_ref, *, lam_init):
    tq = DIFF_TQ
    i = pl.program_id(2)
    q = q_ref[0].astype(F32) * (DIFF_D ** -0.5 * LOG2E)
    lane = lax.broadcasted_iota(jnp.int32, q.shape, 1)
    qs_ref[...] = jnp.concatenate(
        [jnp.where(lane < DIFF_D, q, 0.0), jnp.where(lane >= DIFF_D, q, 0.0)], axis=0).astype(BF16)
    m_ref[...] = jnp.full(m_ref.shape, NEG, F32)
    l_ref[...] = jnp.zeros(l_ref.shape, F32)
    acc_ref[...] = jnp.zeros(acc_ref.shape, F32)

    def step(kstart, width, t_col):
        def bias(c):
            if t_col is None or t_col + c < 0:
                return None
            t = t_ref[0, :, t_col + c:t_col + c + FLASH_SUB]
            return jnp.concatenate([t, t], axis=0)

        k = k_ref[0, pl.ds(kstart, width), :]
        s = _dot_nt(qs_ref[...], k)
        _online_softmax_update(s, v_ref[0, pl.ds(kstart, width), :], m_ref, l_ref, acc_ref, bias)

    nfar = jnp.maximum(i - 1, 0)

    def far_body(j, carry):
        step(pl.multiple_of(j * (2 * tq), 2 * tq), 2 * tq, None)
        return carry

    lax.fori_loop(0, lax.shift_right_logical(nfar, 1), far_body, 0)
    odd = lax.rem(nfar, 2) == 1

    @pl.when(odd)
    def _():
        step(pl.multiple_of((i - 2) * tq, tq), 3 * tq, -tq)

    @pl.when(jnp.logical_and(i > 0, jnp.logical_not(odd)))
    def _():
        step(pl.multiple_of((i - 1) * tq, tq), 2 * tq, 0)

    @pl.when(i == 0)
    def _():
        step(0, tq, tq)

    o = acc_ref[...] / jnp.maximum(l_ref[...], TINY)
    lam = (jnp.exp(jnp.sum(lq1_ref[...] * lk1_ref[...], axis=1, keepdims=True))
           - jnp.exp(jnp.sum(lq2_ref[...] * lk2_ref[...], axis=1, keepdims=True)) + lam_init)
    a = o[0:tq] - lam * o[tq:2 * tq]
    o_ref[0] = (_rms(a, sub_ref[...]) * (1.0 - lam_init)).astype(BF16)


def _diff_attention(proj3, t_diff, lq1, lk1, lq2, lk2, subln, lam_init):
    b, s, _ = proj3.shape
    tq = DIFF_TQ
    vec = lambda n: pl.BlockSpec((1, n), lambda bi, h, i: (0, 0))
    return pl.pallas_call(
        functools.partial(_diff_kernel, lam_init=lam_init),
        grid=(b, DIFF_HEADS, s // tq),
        in_specs=[
            pl.BlockSpec((1, tq, LANES), lambda bi, h, i: (bi, i, COL_DQ // LANES + h)),
            pl.BlockSpec((1, s, LANES), lambda bi, h, i: (bi, 0, COL_DK // LANES + h)),
            pl.BlockSpec((1, s, LANES), lambda bi, h, i: (bi, 0, COL_DV // LANES + h)),
            pl.BlockSpec((1, tq, 2 * tq), lambda bi, h, i: (h, 0, 0)),
            vec(DIFF_D), vec(DIFF_D), vec(DIFF_D), vec(DIFF_D), vec(2 * DIFF_D),
        ],
        out_specs=pl.BlockSpec((1, tq, LANES), lambda bi, h, i: (bi, i, h)),
        out_shape=jax.ShapeDtypeStruct((b, s, DIFF_HEADS * 2 * DIFF_D), BF16),
        scratch_shapes=[
            pltpu.VMEM((2 * tq, LANES), BF16),
            pltpu.VMEM((2 * tq, LANES), F32),
            pltpu.VMEM((2 * tq, LANES), F32),
            pltpu.VMEM((2 * tq, LANES), F32),
        ],
        compiler_params=_cparams(("arbitrary", "arbitrary", "arbitrary")),
        name="diff_attn",
    )(proj3, proj3, proj3, t_diff, lq1, lk1, lq2, lk2, subln)


def _compress_kernel(r_ref, pos_ref, w1_ref, w2_ref, o_ref):
    half = CMP_STRIDE * NSA_DK
    r = r_ref[0, 0, 0]
    n = r.shape[0]
    a = _dot(r, w1_ref[0, 0:half, :])
    bm = _dot(r, w1_ref[0, half:2 * half, :])
    posb = jnp.broadcast_to(pos_ref[0], (SUBLANES, 2 * half)).astype(BF16)
    pt = _dot(posb, w1_ref[0])[0:1]
    hid = a + pltpu.roll(bm, n - 1, axis=0) + pt
    o_ref[0, 0, 0] = _dot(_gelu_tanh(hid).astype(BF16), w2_ref[0]).astype(BF16)


def _compress(r, pos, w1, w2):
    _, g, b, n, width = r.shape
    return pl.pallas_call(
        _compress_kernel,
        grid=(2, b, g),
        in_specs=[
            pl.BlockSpec((1, 1, 1, n, width), lambda t, bi, gi: (t, gi, bi, 0, 0)),
            pl.BlockSpec((1, 1, CMP_LEN * NSA_DK), lambda t, bi, gi: (t, 0, 0)),
            pl.BlockSpec((1, CMP_LEN * NSA_DK, CMP_HIDDEN), lambda t, bi, gi: (t, 0, 0)),
            pl.BlockSpec((1, CMP_HIDDEN, NSA_DK), lambda t, bi, gi: (t, 0, 0)),
        ],
        out_specs=pl.BlockSpec((1, 1, 1, n, NSA_DK), lambda t, bi, gi: (t, bi, gi, 0, 0)),
        out_shape=jax.ShapeDtypeStruct((2, b, g, n, NSA_DK), BF16),
        compiler_params=_cparams(("arbitrary", "arbitrary", "arbitrary")),
        name="nsa_compress",
    )(r, pos, w1, w2)


def _nsa_kernel(q_ref, kc_ref, vc_ref, ks_ref, vs_ref, kw_ref, vw_ref, ts_ref, tw_ref, gate_ref,
                selmap_ref, o_ref, qx_ref, m_ref, l_ref, acc_ref, part_ref):
    tq, tk, hg = NSA_TQ, NSA_TK, NSA_GROUP
    rows = hg * tq
    i = pl.program_id(2)
    q0 = i * tq
    qt = q_ref[0]
    q4 = jnp.concatenate([qt[:, h * NSA_DK:(h + 1) * NSA_DK] for h in range(hg)], axis=0)
    q4 = (q4.astype(F32) * (NSA_DK ** -0.5 * LOG2E)).astype(BF16)

    kc = kc_ref[0, 0, 0]
    nc = kc.shape[0]
    cend = lax.broadcasted_iota(jnp.int32, (tq, nc), 1) * CMP_STRIDE + (CMP_LEN - 1)
    hidden = jnp.where(cend <= q0 + lax.broadcasted_iota(jnp.int32, (tq, nc), 0), 0.0, NEG)
    sc = _dot_nt(q4, kc) + jnp.concatenate([hidden] * hg, axis=0)
    mc = jnp.max(sc, axis=1, keepdims=True)
    pc = jnp.exp2(sc - mc)
    norm = jnp.where(mc > 0.5 * NEG,
                     1.0 / jnp.maximum(jnp.sum(pc, axis=1, keepdims=True), TINY), 0.0)
    pc = pc * norm
    o_cmp = _dot(pc.astype(BF16), vc_ref[0, 0, 0])

    pcsum = pc[0:tq] + pc[tq:2 * tq] + pc[2 * tq:3 * tq] + pc[3 * tq:4 * tq]
    imp_t = _dot_nt(selmap_ref[...], pcsum, precision=lax.Precision.HIGHEST)
    nb = imp_t.shape[0]
    blk = lax.broadcasted_iota(jnp.int32, (nb, tq), 0)
    cur = lax.shift_right_logical(q0 + lax.broadcasted_iota(jnp.int32, (nb, tq), 1),
                                  int(math.log2(SLC_LEN)))
    forced = (blk == 0) | (blk == cur) | (blk == cur - 1)
    excluded = -3.0e38
    val = jnp.where(forced | (blk > cur), excluded, imp_t)
    sel = jnp.where(forced, 1.0, 0.0)
    for _ in range(min(SLC_TOPK, nb) - 3):
        best = jnp.max(val, axis=0, keepdims=True)
        first = jnp.min(jnp.where(val == best, blk, nb), axis=0, keepdims=True)
        hit = blk == first
        sel = jnp.where(hit, 1.0, sel)
        val = jnp.where(hit, excluded, val)
    maskbias = jnp.where(sel.T > 0.5, 0.0, NEG).astype(BF16)
    qx_ref[:, 0:NSA_DK] = q4
    qx_ref[:, NSA_DK:NSA_DK + nb] = jnp.concatenate([maskbias] * hg, axis=0)

    kst = pl.multiple_of(jnp.maximum(q0 - WINDOW, 0), tq)
    w0 = pl.multiple_of(WINDOW - (q0 - kst), LANES)
    lw = _dot_nt(q4, kw_ref[0, pl.ds(kst, NSA_WK), :]) + tw_ref[0, :, pl.ds(w0, NSA_WK)]
    pw = jnp.exp2((lw - jnp.max(lw, axis=1, keepdims=True)).astype(BF16))
    vw1 = jnp.concatenate([vw_ref[0, pl.ds(kst, NSA_WK), :], jnp.ones((NSA_WK, LANES), BF16)],
                          axis=1)
    o_win = _dot(pw, vw1)
    o_win = o_win[:, 0:NSA_DV] / jnp.maximum(o_win[:, NSA_DV:], TINY)

    gt = gate_ref[0]
    gcol = lambda c: jnp.concatenate([gt[:, 3 * h + c:3 * h + c + 1] for h in range(hg)], axis=0)
    part_ref[...] = gcol(0) * o_cmp + gcol(2) * o_win

    m_ref[...] = jnp.full(m_ref.shape, NEG, F32)
    l_ref[...] = jnp.zeros(l_ref.shape, F32)
    acc_ref[...] = jnp.zeros(acc_ref.shape, F32)

    def step(j, width, first_near):
        kstart = pl.multiple_of(j * tk, tk)

        def bias(c):
            if first_near is None or c < first_near * tk:
                return None
            u0 = pl.multiple_of(TS_OFF - (q0 - j * tk) + c, LANES)
            return ts_ref[0, :, pl.ds(u0, FLASH_SUB)]

        s = _dot_nt(qx_ref[...], ks_ref[0, pl.ds(kstart, width), :])
        _online_softmax_update(s, vs_ref[0, pl.ds(kstart, width), :], m_ref, l_ref, acc_ref, bias)

    jl = lax.shift_right_logical(q0 + tq - 1, int(math.log2(tk)))
    nfar = jnp.maximum(jl - 1, 0)

    def far_body(j, carry):
        step(2 * j, 2 * tk, None)
        return carry

    lax.fori_loop(0, lax.shift_right_logical(nfar, 1), far_body, 0)
    odd = lax.rem(nfar, 2) == 1

    @pl.when(odd)
    def _():
        step(jl - 2, 3 * tk, 1)

    @pl.when(jnp.logical_and(jl > 0, jnp.logical_not(odd)))
    def _():
        step(jl - 1, 2 * tk, 0)

    @pl.when(jl == 0)
    def _():
        step(0, tk, 0)
    o = part_ref[...] + gcol(1) * (acc_ref[...] / jnp.maximum(l_ref[...], TINY))
    for h in range(hg):
        o_ref[0, :, h * NSA_DV:(h + 1) * NSA_DV] = o[h * tq:(h + 1) * tq].astype(BF16)


def _nsa_attention(proj3, kvc, ksx3, t_sel, t_win, gates3, selmap):
    b, s, _ = proj3.shape
    tq, hg = NSA_TQ, NSA_GROUP
    nc = kvc.shape[3]
    nb = s // SLC_LEN
    qcols = hg * NSA_DK
    return pl.pallas_call(
        _nsa_kernel,
        grid=(b, NSA_KV, s // tq),
        in_specs=[
            pl.BlockSpec((1, tq, qcols), lambda bi, g, i: (bi, i, COL_NQ // qcols + g)),
            pl.BlockSpec((1, 1, 1, nc, NSA_DK), lambda bi, g, i: (0, bi, g, 0, 0)),
            pl.BlockSpec((1, 1, 1, nc, NSA_DV), lambda bi, g, i: (1, bi, g, 0, 0)),
            pl.BlockSpec((1, s, NSA_DK + nb), lambda bi, g, i: (bi, 0, g)),
            pl.BlockSpec((1, s, NSA_DV), lambda bi, g, i: (bi, 0, COL_VS // NSA_DV + g)),
            pl.BlockSpec((1, s, NSA_DK), lambda bi, g, i: (bi, 0, COL_KW // NSA_DK + g)),
            pl.BlockSpec((1, s, NSA_DV), lambda bi, g, i: (bi, 0, COL_VW // NSA_DV + g)),
            pl.BlockSpec((1, hg * tq, TS_W), lambda bi, g, i: (g, 0, 0),
                         pipeline_mode=pl.Buffered(1)),
            pl.BlockSpec((1, hg * tq, TW_W), lambda bi, g, i: (g, 0, 0),
                         pipeline_mode=pl.Buffered(1)),
            pl.BlockSpec((1, tq, LANES), lambda bi, g, i: (bi, i, g)),
            pl.BlockSpec((nb, nc), lambda bi, g, i: (0, 0)),
        ],
        out_specs=pl.BlockSpec((1, tq, hg * NSA_DV), lambda bi, g, i: (bi, i, g)),
        out_shape=jax.ShapeDtypeStruct((b, s, NSA_HEADS * NSA_DV), BF16),
        scratch_shapes=[
            pltpu.VMEM((hg * tq, NSA_DK + nb), BF16),
            pltpu.VMEM((hg * tq, LANES), F32),
            pltpu.VMEM((hg * tq, LANES), F32),
            pltpu.VMEM((hg * tq, NSA_DV), F32),
            pltpu.VMEM((hg * tq, NSA_DV), F32),
        ],
        compiler_params=_cparams(("arbitrary", "arbitrary", "arbitrary")),
        name="nsa_attn",
    )(proj3, kvc, kvc, ksx3, proj3, proj3, proj3, t_sel, t_win, gates3, selmap)


def _outproj_kernel(od_ref, on_ref, x_ref, wd_ref, wn_ref, g_ref, o_ref):
    y = _dot(od_ref[...], wd_ref[...]) + _dot(on_ref[...], wn_ref[...])
    o_ref[...] = x_ref[...] + _rms(y, g_ref[...])


def _outproj(od, on, xf, w_d, w_n, g, tm):
    m = xf.shape[0]
    kd, kn = od.shape[1], on.shape[1]
    return pl.pallas_call(
        _outproj_kernel,
        grid=(m // tm,),
        in_specs=[
            pl.BlockSpec((tm, kd), lambda i: (i, 0)),
            pl.BlockSpec((tm, kn), lambda i: (i, 0)),
            pl.BlockSpec((tm, D_MODEL), lambda i: (i, 0)),
            pl.BlockSpec((kd, D_MODEL), lambda i: (0, 0)),
            pl.BlockSpec((kn, D_MODEL), lambda i: (0, 0)),
            pl.BlockSpec((1, D_MODEL), lambda i: (0, 0)),
        ],
        out_specs=pl.BlockSpec((tm, D_MODEL), lambda i: (i, 0)),
        out_shape=jax.ShapeDtypeStruct((m, D_MODEL), F32),
        compiler_params=_cparams(("arbitrary",)),
        name="outproj",
    )(od, on, xf, w_d, w_n, g)


def _ffn_kernel(x_ref, gpre_ref, wg_ref, wu_ref, cwg_ref, cwu_ref, cbg_ref, cbu_ref, wd_ref,
                gpost_ref, o_ref, h_ref, acc_ref, carry_ref, yg_ref, yu_ref, *, tiles_per_seq):
    i = pl.program_id(0)
    j = pl.program_id(1)
    tm = x_ref.shape[0]
    halo = SUBLANES

    @pl.when(j == 0)
    def _():
        h_ref[...] = _rms(x_ref[...], gpre_ref[...]).astype(BF16)
        acc_ref[...] = jnp.zeros(acc_ref.shape, F32)

    @pl.when(lax.rem(i, tiles_per_seq) == 0)
    def _():
        carry_ref[j] = jnp.zeros(carry_ref.shape[1:], F32)

    def conv(w_ref, cw_ref, cb_ref, slot, y_ref):
        u = _dot(h_ref[...], w_ref[...])
        prev = carry_ref[j, slot]
        carry_ref[j, slot] = u[tm - halo:tm]
        w0, w1, w2 = cw_ref[0:1, :], cw_ref[1:2, :], cw_ref[2:3, :]
        bias = cb_ref[...]

        def taps(z):
            n = z.shape[0]
            return bias + w2 * z + w1 * pltpu.roll(z, 1, axis=0) + w0 * pltpu.roll(z, 2, axis=0)

        y_ref[...] = taps(u)
        y_ref[0:halo, :] = taps(jnp.concatenate([prev, u[0:halo]], axis=0))[halo:2 * halo]

    conv(wg_ref, cwg_ref, cbg_ref, 0, yg_ref)
    conv(wu_ref, cwu_ref, cbu_ref, 1, yu_ref)
    act = (_gelu_tanh(yg_ref[...]) * yu_ref[...]).astype(BF16)
    acc_ref[...] += _dot(act, wd_ref[...])

    @pl.when(j == pl.num_programs(1) - 1)
    def _():
        o_ref[...] = x_ref[...] + _rms(acc_ref[...], gpost_ref[...])


def _ffn(x1, gpre, w_up, conv_w, conv_b, w_down, gpost, tm, tf, seq):
    m = x1.shape[0]
    nj = D_FF // tf
    return pl.pallas_call(
        functools.partial(_ffn_kernel, tiles_per_seq=seq // tm),
        grid=(m // tm, nj),
        in_specs=[
            pl.BlockSpec((tm, D_MODEL), lambda i, j: (i, 0)),
            pl.BlockSpec((1, D_MODEL), lambda i, j: (0, 0)),
            pl.BlockSpec((D_MODEL, tf), lambda i, j: (0, j)),
            pl.BlockSpec((D_MODEL, tf), lambda i, j: (0, j + nj)),
            pl.BlockSpec((CONV_W, tf), lambda i, j: (0, j)),
            pl.BlockSpec((CONV_W, tf), lambda i, j: (0, j + nj)),
            pl.BlockSpec((1, tf), lambda i, j: (0, j)),
            pl.BlockSpec((1, tf), lambda i, j: (0, j + nj)),
            pl.BlockSpec((tf, D_MODEL), lambda i, j: (j, 0)),
            pl.BlockSpec((1, D_MODEL), lambda i, j: (0, 0)),
        ],
        out_specs=pl.BlockSpec((tm, D_MODEL), lambda i, j: (i, 0)),
        out_shape=jax.ShapeDtypeStruct((m, D_MODEL), F32),
        scratch_shapes=[
            pltpu.VMEM((tm, D_MODEL), BF16),
            pltpu.VMEM((tm, D_MODEL), F32),
            pltpu.VMEM((nj, 2, SUBLANES, tf), F32),
            pltpu.VMEM((tm, tf), F32),
            pltpu.VMEM((tm, tf), F32),
        ],
        compiler_params=_cparams(("arbitrary", "arbitrary")),
        name="ffn",
    )(x1, gpre, w_up, w_up, conv_w, conv_w, conv_b, conv_b, w_down, gpost)


def _rel_bucket(delta):
    n = jnp.maximum(delta, 0)
    max_exact = N_BUCKETS // 2
    nf = jnp.maximum(n, 1).astype(F32)
    large = max_exact + (jnp.log(nf / max_exact) / math.log(MAX_DISTANCE / max_exact)
                         * (N_BUCKETS - max_exact)).astype(jnp.int32)
    large = jnp.minimum(large, N_BUCKETS - 1)
    return jnp.where(n < max_exact, n, large)


def _bias_kernel(tab_ref, bkt_ref, o_ref):
    h = pl.program_id(0)
    i = pl.program_id(1)
    bkt = bkt_ref[...]
    row = jnp.full(bkt.shape, NEG, F32)
    for bucket in range(N_BUCKETS):
        row = jnp.where(bkt == bucket, tab_ref[bucket, h], row)
    tr, width = o_ref.shape[1:]
    rows = jnp.concatenate([row] * (tr // SUBLANES), axis=0)
    o_ref[0] = pltpu.roll(rows, i * tr, axis=1, stride=1, stride_axis=0)[:, 0:width]


def _bias_strip(table, rows, width, off, hi):
    heads = table.shape[1]
    tr = LANES
    wp = width + rows
    t = jnp.arange(wp)
    dist = off - jnp.where(t < width, t, t - wp)
    bkt = jnp.where((dist < 0) | (dist >= hi), -1, _rel_bucket(dist)).astype(jnp.int32)
    bkt = jnp.broadcast_to(bkt[None, :], (SUBLANES, wp))
    return pl.pallas_call(
        _bias_kernel,
        grid=(heads, rows // tr),
        in_specs=[
            pl.BlockSpec(memory_space=pltpu.SMEM),
            pl.BlockSpec((SUBLANES, wp), lambda h, i: (0, 0)),
        ],
        out_specs=pl.BlockSpec((1, tr, width), lambda h, i: (h, i, 0)),
        out_shape=jax.ShapeDtypeStruct((heads, rows, width), F32),
        compiler_params=_cparams(("arbitrary", "arbitrary")),
        name="bias_strip",
    )(table, bkt)


def _selection_map(nb, nc_pad):
    r = SLC_LEN // CMP_STRIDE
    j = jnp.arange(nb)[:, None]
    c = jnp.arange(nc_pad)[None, :]
    off = r * j - c
    out = jnp.zeros((nb, nc_pad), F32)
    for mm in range(r):
        for nn in range(CMP_LEN // CMP_STRIDE):
            out = out + (off == mm + nn).astype(F32)
    return out * (c < nc_pad - 1)


def kernel(x, pre_mix_norm, w_in, lambda_q1, lambda_k1, lambda_q2, lambda_k2, diff_subln,
           cmp_pos_k, cmp_pos_v, cmp_k_w1, cmp_k_w2, cmp_v_w1, cmp_v_w2, rel_bias, w_out,
           post_mix_norm, pre_ffn_norm, w_up, conv_w, conv_b, w_down, post_ffn_norm):
    b, s, d = x.shape
    assert d == D_MODEL and s % DIFF_TQ == 0 and s // SLC_LEN >= SLC_TOPK
    m = b * s
    depth = w_in.shape[0]
    big = 1 << 30
    for l in range(depth):
        xf = x.reshape(m, d)
        w_main = w_in[l][:, :MAIN_COLS].astype(BF16)
        w_gate = jnp.pad(w_in[l][:, MAIN_COLS:].reshape(d, NSA_KV, 3 * NSA_GROUP),
                         ((0, 0), (0, 0), (0, LANES - 3 * NSA_GROUP)))
        w_gate = w_gate.reshape(d, NSA_KV * LANES).astype(BF16)
        proj, gates, ksx, rows16 = _inproj(xf, pre_mix_norm[l][None], w_main, w_gate, tm=1024,
                                           tn=512, seq=s)
        proj3 = proj.reshape(b, s, MAIN_COLS)

        bias_diff = rel_bias[:, :DIFF_HEADS]
        bias_nsa = rel_bias[:, DIFF_HEADS:]
        t_diff = _bias_strip((bias_diff - bias_diff[N_BUCKETS - 1]) * LOG2E,
                             DIFF_TQ, 2 * DIFF_TQ, DIFF_TQ, big)
        t_sel = _bias_strip((bias_nsa - bias_nsa[N_BUCKETS - 1]) * LOG2E,
                            NSA_TQ, TS_W, TS_OFF, big)
        t_win = _bias_strip(bias_nsa * LOG2E, NSA_TQ, TW_W, WINDOW, WINDOW)
        t_sel = t_sel.reshape(NSA_KV, NSA_GROUP * NSA_TQ, TS_W)
        t_win = t_win.reshape(NSA_KV, NSA_GROUP * NSA_TQ, TW_W)

        lam_init = 0.8 - 0.6 * math.exp(-0.3 * l)
        o_diff = _diff_attention(proj3, t_diff, lambda_q1[l][None], lambda_k1[l][None],
                                 lambda_q2[l][None], lambda_k2[l][None], diff_subln[l][None],
                                 lam_init)

        nrow = s // CMP_STRIDE

        r = rows16.reshape(2, NSA_KV, b, nrow, CMP_STRIDE * NSA_DK)
        pos = jnp.stack([cmp_pos_k[l].reshape(1, -1), cmp_pos_v[l].reshape(1, -1)])
        w1 = jnp.stack([cmp_k_w1[l], cmp_v_w1[l]]).astype(BF16)
        w2 = jnp.stack([cmp_k_w2[l], cmp_v_w2[l]]).astype(BF16)
        kvc = _compress(r, pos, w1, w2)

        nb = s // SLC_LEN
        o_nsa = _nsa_attention(proj3, kvc, ksx.reshape(b, s, -1), t_sel, t_win,
                               gates.reshape(b, s, -1), _selection_map(nb, nrow))

        half = DIFF_HEADS * 2 * DIFF_D
        x1 = _outproj(o_diff.reshape(m, -1), o_nsa.reshape(m, -1), xf,
                      w_out[l][:half].astype(BF16), w_out[l][half:].astype(BF16),
                      post_mix_norm[l][None], tm=512)
        x2 = _ffn(x1, pre_ffn_norm[l][None], w_up[l].astype(BF16), conv_w[l], conv_b[l][None],
                  w_down[l].astype(BF16), post_ffn_norm[l][None], tm=512, tf=512, seq=s)
        x = x2.reshape(b, s, d)
    return x
```

```python
import functools
import math

import jax
import jax.numpy as jnp
from jax import lax
from jax.experimental import pallas as pl
from jax.experimental.pallas import tpu as pltpu

F32 = jnp.float32
BF16 = jnp.bfloat16

D_MODEL = 2048
DIFF_HEADS = 8
DIFF_D = 64
NSA_HEADS = 8
NSA_KV = 2
NSA_GROUP = NSA_HEADS // NSA_KV
NSA_DK = 128
NSA_DV = 128
CMP_LEN = 32
CMP_STRIDE = 16
CMP_HIDDEN = 256
SLC_LEN = 64
SLC_TOPK = 16
WINDOW = 512
N_BUCKETS = 32
MAX_DISTANCE = 128
D_FF = 5632
CONV_W = 3
EPS = 1e-6
NEG = -1e30
FORCE = 1e30
TINY = 1e-30
LOG2E = math.log2(math.e)

LANES = 128
SUBLANES = 8
VMEM_LIMIT = 56 * 1024 * 1024

MAIN_COLS = 5632
N_GATES = NSA_HEADS * 3
COL_DQ, COL_DK, COL_DV, COL_NQ = 0, 1024, 2048, 3072
COL_KC, COL_VC, COL_KS, COL_VS, COL_KW, COL_VW = 4096, 4352, 4608, 4864, 5120, 5376

DIFF_TQ = 512
NSA_TQ = 256
FLASH_SUB = 512
NSA_TK = 512
NSA_WK = WINDOW + NSA_TQ
TS_OFF = 2 * NSA_TK - NSA_TQ
TS_W = TS_OFF + NSA_TK
TW_W = WINDOW + NSA_WK


def _cparams(sem):
    return pltpu.CompilerParams(dimension_semantics=sem, vmem_limit_bytes=VMEM_LIMIT)


def _rms(x, g):
    return x * lax.rsqrt(jnp.mean(x * x, axis=-1, keepdims=True) + EPS) * g


def _gelu_tanh(x):
    return 0.5 * x * (1.0 + jnp.tanh(math.sqrt(2.0 / math.pi) * (x + 0.044715 * (x * x * x))))


def _dot(a, b):
    return jnp.dot(a, b, preferred_element_type=F32)


def _dot_nt(a, b, precision=None):
    return lax.dot_general(a, b, (((1,), (1,)), ((), ())), preferred_element_type=F32,
                           precision=precision)


def _lane_tile(a, n):
    return a if n == 1 else jnp.concatenate([a] * n, axis=1)


def _online_softmax_update(s, v, m_ref, l_ref, acc_ref, bias=None):
    m_prev, l, acc = m_ref[...], l_ref[...], acc_ref[...]
    dv = v.shape[1]
    ones = jnp.ones((FLASH_SUB, LANES), BF16)
    for c in range(0, s.shape[1], FLASH_SUB):
        sc = s[:, c:c + FLASH_SUB]
        extra = None if bias is None else bias(c)
        if extra is not None:
            sc = sc + extra
        m_new = jnp.maximum(m_prev, jnp.max(sc, axis=1, keepdims=True))
        alpha = jnp.exp2(m_prev - m_new)
        p = jnp.exp2((sc - _lane_tile(m_new, FLASH_SUB // LANES)).astype(BF16))
        pv = _dot(p, jnp.concatenate([v[c:c + FLASH_SUB], ones], axis=1))
        l = alpha * l + pv[:, dv:]
        acc = alpha * acc + pv[:, 0:dv]
        m_prev = m_new
    m_ref[...], l_ref[...], acc_ref[...] = m_prev, l, acc


def _inproj_kernel(x_ref, g_ref, w_ref, wg_ref, proj_ref, gates_ref, ksx_ref, r16_ref, h_ref,
                   stage_ref, *, seq):
    i = pl.program_id(0)
    j = pl.program_id(1)
    tm, tn = proj_ref.shape

    @pl.when(j == 0)
    def _():
        h_ref[...] = _rms(x_ref[...], g_ref[...]).astype(BF16)
        gl = _dot(h_ref[...], wg_ref[...])
        gates_ref[...] = 1.0 / (1.0 + jnp.exp(-gl))

    acc = _dot(h_ref[...], w_ref[...])
    proj_ref[...] = acc.astype(BF16)

    @pl.when(j == COL_KS // tn)
    def _():
        nb = seq // SLC_LEN
        xw = NSA_DK + nb
        pos = lax.rem(i * tm + lax.broadcasted_iota(jnp.int32, (tm, nb), 0), seq)
        blk = lax.shift_right_logical(pos, int(math.log2(SLC_LEN)))
        onehot = jnp.where(blk == lax.broadcasted_iota(jnp.int32, (tm, nb), 1), 1.0, 0.0)
        for g in range(NSA_KV):
            ksx_ref[:, g * xw:g * xw + NSA_DK] = acc[:, g * NSA_DK:(g + 1) * NSA_DK].astype(BF16)
            ksx_ref[:, g * xw + NSA_DK:(g + 1) * xw] = onehot.astype(BF16)

    @pl.when(j == COL_KC // tn)
    def _():
        nr = tm // CMP_STRIDE
        for t in range(2):
            for g in range(NSA_KV):
                col = (t * NSA_KV + g) * NSA_DK
                stage_ref[...] = acc[:, col:col + NSA_DK]
                for p in range(CMP_STRIDE):
                    r16_ref[t, g, :, p * NSA_DK:(p + 1) * NSA_DK] = (
                        stage_ref[pl.ds(p, nr, stride=CMP_STRIDE), :].astype(BF16))


def _inproj(xf, g, w_main, w_gate, tm, tn, seq):
    m = xf.shape[0]
    assert COL_KS % tn == 0 and COL_KC % tn == 0 and tn == 2 * NSA_KV * NSA_DK
    xw = NSA_DK + seq // SLC_LEN
    ng = w_gate.shape[1]
    return pl.pallas_call(
        functools.partial(_inproj_kernel, seq=seq),
        grid=(m // tm, MAIN_COLS // tn),
        in_specs=[
            pl.BlockSpec((tm, D_MODEL), lambda i, j: (i, 0)),
            pl.BlockSpec((1, D_MODEL), lambda i, j: (0, 0)),
            pl.BlockSpec((D_MODEL, tn), lambda i, j: (0, j)),
            pl.BlockSpec((D_MODEL, ng), lambda i, j: (0, 0)),
        ],
        out_specs=[
            pl.BlockSpec((tm, tn), lambda i, j: (i, j)),
            pl.BlockSpec((tm, ng), lambda i, j: (i, 0)),
            pl.BlockSpec((tm, NSA_KV * xw), lambda i, j: (i, 0)),
            pl.BlockSpec((2, NSA_KV, tm // CMP_STRIDE, CMP_STRIDE * NSA_DK),
                         lambda i, j: (0, 0, i, 0)),
        ],
        out_shape=[
            jax.ShapeDtypeStruct((m, MAIN_COLS), BF16),
            jax.ShapeDtypeStruct((m, ng), F32),
            jax.ShapeDtypeStruct((m, NSA_KV * xw), BF16),
            jax.ShapeDtypeStruct((2, NSA_KV, m // CMP_STRIDE, CMP_STRIDE * NSA_DK), BF16),
        ],
        scratch_shapes=[pltpu.VMEM((tm, D_MODEL), BF16), pltpu.VMEM((tm, NSA_DK), F32)],
        compiler_params=_cparams(("arbitrary", "arbitrary")),
        name="inproj",
    )(xf, g, w_main, w_gate)


def _diff_kernel(q_ref, k_ref, v_ref, t_ref, lq1_ref, lk1_ref, lq2_ref, lk2_ref, sub_ref,
                 o_ref, qs_ref, m_ref, l_ref, acc_ref, *, lam_init):
    tq = DIFF_TQ
    i = pl.program_id(2)
    q = q_ref[0].astype(F32) * (DIFF_D ** -0.5 * LOG2E)
    lane = lax.broadcasted_iota(jnp.int32, q.shape, 1)
    qs_ref[...] = jnp.concatenate(
        [jnp.where(lane < DIFF_D, q, 0.0), jnp.where(lane >= DIFF_D, q, 0.0)], axis=0).astype(BF16)
    m_ref[...] = jnp.full(m_ref.shape, NEG, F32)
    l_ref[...] = jnp.zeros(l_ref.shape, F32)
    acc_ref[...] = jnp.zeros(acc_ref.shape, F32)

    def step(kstart, width, t_col):
        def bias(c):
            if t_col is None or t_col + c < 0:
                return None
            t = t_ref[0, :, t_col + c:t_col + c + FLASH_SUB]
            return jnp.concatenate([t, t], axis=0)

        k = k_ref[0, pl.ds(kstart, width), :]
        s = _dot_nt(qs_ref[...], k)
        _online_softmax_update(s, v_ref[0, pl.ds(kstart, width), :], m_ref, l_ref, acc_ref, bias)

    nfar = jnp.maximum(i - 1, 0)

    def far_body(j, carry):
        step(pl.multiple_of(j * (2 * tq), 2 * tq), 2 * tq, None)
        return carry

    lax.fori_loop(0, lax.shift_right_logical(nfar, 1), far_body, 0)
    odd = lax.rem(nfar, 2) == 1

    @pl.when(odd)
    def _():
        step(pl.multiple_of((i - 2) * tq, tq), 3 * tq, -tq)

    @pl.when(jnp.logical_and(i > 0, jnp.logical_not(odd)))
    def _():
        step(pl.multiple_of((i - 1) * tq, tq), 2 * tq, 0)

    @pl.when(i == 0)
    def _():
        step(0, tq, tq)

    o = acc_ref[...] / jnp.maximum(l_ref[...], TINY)
    lam = (jnp.exp(jnp.sum(lq1_ref[...] * lk1_ref[...], axis=1, keepdims=True))
           - jnp.exp(jnp.sum(lq2_ref[...] * lk2_ref[...], axis=1, keepdims=True)) + lam_init)
    a = o[0:tq] - lam * o[tq:2 * tq]
    o_ref[0] = (_rms(a, sub_ref[...]) * (1.0 - lam_init)).astype(BF16)


def _diff_attention(proj3, t_diff, lq1, lk1, lq2, lk2, subln, lam_init):
    b, s, _ = proj3.shape
    tq = DIFF_TQ
    vec = lambda n: pl.BlockSpec((1, n), lambda bi, h, i: (0, 0))
    return pl.pallas_call(
        functools.partial(_diff_kernel, lam_init=lam_init),
        grid=(b, DIFF_HEADS, s // tq),
        in_specs=[
            pl.BlockSpec((1, tq, LANES), lambda bi, h, i: (bi, i, COL_DQ // LANES + h)),
            pl.BlockSpec((1, s, LANES), lambda bi, h, i: (bi, 0, COL_DK // LANES + h)),
            pl.BlockSpec((1, s, LANES), lambda bi, h, i: (bi, 0, COL_DV // LANES + h)),
            pl.BlockSpec((1, tq, 2 * tq), lambda bi, h, i: (h, 0, 0)),
            vec(DIFF_D), vec(DIFF_D), vec(DIFF_D), vec(DIFF_D), vec(2 * DIFF_D),
        ],
        out_specs=pl.BlockSpec((1, tq, LANES), lambda bi, h, i: (bi, i, h)),
        out_shape=jax.ShapeDtypeStruct((b, s, DIFF_HEADS * 2 * DIFF_D), BF16),
        scratch_shapes=[
            pltpu.VMEM((2 * tq, LANES), BF16),
            pltpu.VMEM((2 * tq, LANES), F32),
            pltpu.VMEM((2 * tq, LANES), F32),
            pltpu.VMEM((2 * tq, LANES), F32),
        ],
        compiler_params=_cparams(("arbitrary", "arbitrary", "arbitrary")),
        name="diff_attn",
    )(proj3, proj3, proj3, t_diff, lq1, lk1, lq2, lk2, subln)


def _compress_kernel(r_ref, pos_ref, w1_ref, w2_ref, o_ref):
    half = CMP_STRIDE * NSA_DK
    r = r_ref[0, 0, 0]
    n = r.shape[0]
    a = _dot(r, w1_ref[0, 0:half, :])
    bm = _dot(r, w1_ref[0, half:2 * half, :])
    posb = jnp.broadcast_to(pos_ref[0], (SUBLANES, 2 * half)).astype(BF16)
    pt = _dot(posb, w1_ref[0])[0:1]
    hid = a + pltpu.roll(bm, n - 1, axis=0) + pt
    o_ref[0, 0, 0] = _dot(_gelu_tanh(hid).astype(BF16), w2_ref[0]).astype(BF16)


def _compress(r, pos, w1, w2):
    _, g, b, n, width = r.shape
    return pl.pallas_call(
        _compress_kernel,
        grid=(2, b, g),
        in_specs=[
            pl.BlockSpec((1, 1, 1, n, width), lambda t, bi, gi: (t, gi, bi, 0, 0)),
            pl.BlockSpec((1, 1, CMP_LEN * NSA_DK), lambda t, bi, gi: (t, 0, 0)),
            pl.BlockSpec((1, CMP_LEN * NSA_DK, CMP_HIDDEN), lambda t, bi, gi: (t, 0, 0)),
            pl.BlockSpec((1, CMP_HIDDEN, NSA_DK), lambda t, bi, gi: (t, 0, 0)),
        ],
        out_specs=pl.BlockSpec((1, 1, 1, n, NSA_DK), lambda t, bi, gi: (t, bi, gi, 0, 0)),
        out_shape=jax.ShapeDtypeStruct((2, b, g, n, NSA_DK), BF16),
        compiler_params=_cparams(("arbitrary", "arbitrary", "arbitrary")),
        name="nsa_compress",
    )(r, pos, w1, w2)


def _nsa_kernel(q_ref, kc_ref, vc_ref, ks_ref, vs_ref, kw_ref, vw_ref, ts_ref, tw_ref, gate_ref,
                selmap_ref, o_ref, qx_ref, m_ref, l_ref, acc_ref, part_ref):
    tq, tk, hg = NSA_TQ, NSA_TK, NSA_GROUP
    rows = hg * tq
    i = pl.program_id(2)
    q0 = i * tq
    qt = q_ref[0]
    q4 = jnp.concatenate([qt[:, h * NSA_DK:(h + 1) * NSA_DK] for h in range(hg)], axis=0)
    q4 = (q4.astype(F32) * (NSA_DK ** -0.5 * LOG2E)).astype(BF16)

    kc = kc_ref[0, 0, 0]
    nc = kc.shape[0]
    cend = lax.broadcasted_iota(jnp.int32, (tq, nc), 1) * CMP_STRIDE + (CMP_LEN - 1)
    hidden = jnp.where(cend <= q0 + lax.broadcasted_iota(jnp.int32, (tq, nc), 0), 0.0, NEG)
    sc = _dot_nt(q4, kc) + jnp.concatenate([hidden] * hg, axis=0)
    mc = jnp.max(sc, axis=1, keepdims=True)
    pc = jnp.exp2(sc - mc)
    norm = jnp.where(mc > 0.5 * NEG,
                     1.0 / jnp.maximum(jnp.sum(pc, axis=1, keepdims=True), TINY), 0.0)
    pc = pc * norm
    o_cmp = _dot(pc.astype(BF16), vc_ref[0, 0, 0])

    pcsum = pc[0:tq] + pc[tq:2 * tq] + pc[2 * tq:3 * tq] + pc[3 * tq:4 * tq]
    imp_t = _dot_nt(selmap_ref[...], pcsum, precision=lax.Precision.HIGHEST)
    nb = imp_t.shape[0]
    blk = lax.broadcasted_iota(jnp.int32, (nb, tq), 0)
    cur = lax.shift_right_logical(q0 + lax.broadcasted_iota(jnp.int32, (nb, tq), 1),
                                  int(math.log2(SLC_LEN)))
    forced = (blk == 0) | (blk == cur) | (blk == cur - 1)
    excluded = -3.0e38
    val = jnp.where(forced | (blk > cur), excluded, imp_t)
    sel = jnp.where(forced, 1.0, 0.0)
    for _ in range(min(SLC_TOPK, nb) - 3):
        best = jnp.max(val, axis=0, keepdims=True)
        first = jnp.min(jnp.where(val == best, blk, nb), axis=0, keepdims=True)
        hit = blk == first
        sel = jnp.where(hit, 1.0, sel)
        val = jnp.where(hit, excluded, val)
    maskbias = jnp.where(sel.T > 0.5, 0.0, NEG).astype(BF16)
    qx_ref[:, 0:NSA_DK] = q4
    qx_ref[:, NSA_DK:NSA_DK + nb] = jnp.concatenate([maskbias] * hg, axis=0)

    kst = pl.multiple_of(jnp.maximum(q0 - WINDOW, 0), tq)
    w0 = pl.multiple_of(WINDOW - (q0 - kst), LANES)
    lw = _dot_nt(q4, kw_ref[0, pl.ds(kst, NSA_WK), :]) + tw_ref[0, :, pl.ds(w0, NSA_WK)]
    pw = jnp.exp2((lw - jnp.max(lw, axis=1, keepdims=True)).astype(BF16))
    vw1 = jnp.concatenate([vw_ref[0, pl.ds(kst, NSA_WK), :], jnp.ones((NSA_WK, LANES), BF16)],
                          axis=1)
    o_win = _dot(pw, vw1)
    o_win = o_win[:, 0:NSA_DV] / jnp.maximum(o_win[:, NSA_DV:], TINY)

    gt = gate_ref[0]
    gcol = lambda c: jnp.concatenate([gt[:, 3 * h + c:3 * h + c + 1] for h in range(hg)], axis=0)
    part_ref[...] = gcol(0) * o_cmp + gcol(2) * o_win

    m_ref[...] = jnp.full(m_ref.shape, NEG, F32)
    l_ref[...] = jnp.zeros(l_ref.shape, F32)
    acc_ref[...] = jnp.zeros(acc_ref.shape, F32)

    def step(j, width, first_near):
        kstart = pl.multiple_of(j * tk, tk)

        def bias(c):
            if first_near is None or c < first_near * tk:
                return None
            u0 = pl.multiple_of(TS_OFF - (q0 - j * tk) + c, LANES)
            return ts_ref[0, :, pl.ds(u0, FLASH_SUB)]

        s = _dot_nt(qx_ref[...], ks_ref[0, pl.ds(kstart, width), :])
        _online_softmax_update(s, vs_ref[0, pl.ds(kstart, width), :], m_ref, l_ref, acc_ref, bias)

    jl = lax.shift_right_logical(q0 + tq - 1, int(math.log2(tk)))
    nfar = jnp.maximum(jl - 1, 0)

    def far_body(j, carry):
        step(2 * j, 2 * tk, None)
        return carry

    lax.fori_loop(0, lax.shift_right_logical(nfar, 1), far_body, 0)
    odd = lax.rem(nfar, 2) == 1

    @pl.when(odd)
    def _():
        step(jl - 2, 3 * tk, 1)

    @pl.when(jnp.logical_and(jl > 0, jnp.logical_not(odd)))
    def _():
        step(jl - 1, 2 * tk, 0)

    @pl.when(jl == 0)
    def _():
        step(0, tk, 0)
    o = part_ref[...] + gcol(1) * (acc_ref[...] / jnp.maximum(l_ref[...], TINY))
    for h in range(hg):
        o_ref[0, :, h * NSA_DV:(h + 1) * NSA_DV] = o[h * tq:(h + 1) * tq].astype(BF16)


def _nsa_attention(proj3, kvc, ksx3, t_sel, t_win, gates3, selmap):
    b, s, _ = proj3.shape
    tq, hg = NSA_TQ, NSA_GROUP
    nc = kvc.shape[3]
    nb = s // SLC_LEN
    qcols = hg * NSA_DK
    return pl.pallas_call(
        _nsa_kernel,
        grid=(b, NSA_KV, s // tq),
        in_specs=[
            pl.BlockSpec((1, tq, qcols), lambda bi, g, i: (bi, i, COL_NQ // qcols + g)),
            pl.BlockSpec((1, 1, 1, nc, NSA_DK), lambda bi, g, i: (0, bi, g, 0, 0)),
            pl.BlockSpec((1, 1, 1, nc, NSA_DV), lambda bi, g, i: (1, bi, g, 0, 0)),
            pl.BlockSpec((1, s, NSA_DK + nb), lambda bi, g, i: (bi, 0, g)),
            pl.BlockSpec((1, s, NSA_DV), lambda bi, g, i: (bi, 0, COL_VS // NSA_DV + g)),
            pl.BlockSpec((1, s, NSA_DK), lambda bi, g, i: (bi, 0, COL_KW // NSA_DK + g)),
            pl.BlockSpec((1, s, NSA_DV), lambda bi, g, i: (bi, 0, COL_VW // NSA_DV + g)),
            pl.BlockSpec((1, hg * tq, TS_W), lambda bi, g, i: (g, 0, 0),
                         pipeline_mode=pl.Buffered(1)),
            pl.BlockSpec((1, hg * tq, TW_W), lambda bi, g, i: (g, 0, 0),
                         pipeline_mode=pl.Buffered(1)),
            pl.BlockSpec((1, tq, LANES), lambda bi, g, i: (bi, i, g)),
            pl.BlockSpec((nb, nc), lambda bi, g, i: (0, 0)),
        ],
        out_specs=pl.BlockSpec((1, tq, hg * NSA_DV), lambda bi, g, i: (bi, i, g)),
        out_shape=jax.ShapeDtypeStruct((b, s, NSA_HEADS * NSA_DV), BF16),
        scratch_shapes=[
            pltpu.VMEM((hg * tq, NSA_DK + nb), BF16),
            pltpu.VMEM((hg * tq, LANES), F32),
            pltpu.VMEM((hg * tq, LANES), F32),
            pltpu.VMEM((hg * tq, NSA_DV), F32),
            pltpu.VMEM((hg * tq, NSA_DV), F32),
        ],
        compiler_params=_cparams(("arbitrary", "arbitrary", "arbitrary")),
        name="nsa_attn",
    )(proj3, kvc, kvc, ksx3, proj3, proj3, proj3, t_sel, t_win, gates3, selmap)


def _outproj_kernel(od_ref, on_ref, x_ref, wd_ref, wn_ref, g_ref, o_ref):
    y = _dot(od_ref[...], wd_ref[...]) + _dot(on_ref[...], wn_ref[...])
    o_ref[...] = x_ref[...] + _rms(y, g_ref[...])


def _outproj(od, on, xf, w_d, w_n, g, tm):
    m = xf.shape[0]
    kd, kn = od.shape[1], on.shape[1]
    return pl.pallas_call(
        _outproj_kernel,
        grid=(m // tm,),
        in_specs=[
            pl.BlockSpec((tm, kd), lambda i: (i, 0)),
            pl.BlockSpec((tm, kn), lambda i: (i, 0)),
            pl.BlockSpec((tm, D_MODEL), lambda i: (i, 0)),
            pl.BlockSpec((kd, D_MODEL), lambda i: (0, 0)),
            pl.BlockSpec((kn, D_MODEL), lambda i: (0, 0)),
            pl.BlockSpec((1, D_MODEL), lambda i: (0, 0)),
        ],
        out_specs=pl.BlockSpec((tm, D_MODEL), lambda i: (i, 0)),
        out_shape=jax.ShapeDtypeStruct((m, D_MODEL), F32),
        compiler_params=_cparams(("arbitrary",)),
        name="outproj",
    )(od, on, xf, w_d, w_n, g)


def _ffn_kernel(x_ref, gpre_ref, wg_ref, wu_ref, cwg_ref, cwu_ref, cbg_ref, cbu_ref, wd_ref,
                gpost_ref, o_ref, h_ref, acc_ref, carry_ref, yg_ref, yu_ref, act_ref, *,
                tiles_per_seq):
    i = pl.program_id(0)
    j = pl.program_id(1)
    nj = pl.num_programs(1) - 1
    tm = x_ref.shape[0]
    halo = SUBLANES

    def conv(w_ref, cw_ref, cb_ref, slot, y_ref):
        u = _dot(h_ref[...], w_ref[...])
        prev = carry_ref[j, slot]
        carry_ref[j, slot] = u[tm - halo:tm]
        w0, w1, w2 = cw_ref[0:1, :], cw_ref[1:2, :], cw_ref[2:3, :]
        bias = cb_ref[...]

        def taps(z):
            return bias + w2 * z + w1 * pltpu.roll(z, 1, axis=0) + w0 * pltpu.roll(z, 2, axis=0)

        y_ref[...] = taps(u)
        y_ref[0:halo, :] = taps(jnp.concatenate([prev, u[0:halo]], axis=0))[halo:2 * halo]

    def down_stage():
        acc_ref[...] += _dot(act_ref[...], wd_ref[...])

    def up_stage(with_down):
        if with_down:
            down_stage()
        conv(wg_ref, cwg_ref, cbg_ref, 0, yg_ref)
        conv(wu_ref, cwu_ref, cbu_ref, 1, yu_ref)
        act_ref[...] = (_gelu_tanh(yg_ref[...]) * yu_ref[...]).astype(BF16)

    @pl.when(jnp.logical_and(lax.rem(i, tiles_per_seq) == 0, j < nj))
    def _():
        carry_ref[j] = jnp.zeros(carry_ref.shape[1:], F32)

    @pl.when(j == 0)
    def _():
        h_ref[...] = _rms(x_ref[...], gpre_ref[...]).astype(BF16)
        acc_ref[...] = jnp.zeros(acc_ref.shape, F32)
        up_stage(False)

    @pl.when(jnp.logical_and(j > 0, j < nj))
    def _():
        up_stage(True)

    @pl.when(j == nj)
    def _():
        down_stage()
        o_ref[...] = x_ref[...] + _rms(acc_ref[...], gpost_ref[...])


def _ffn(x1, gpre, w_up, conv_w, conv_b, w_down, gpost, tm, tf, seq):
    m = x1.shape[0]
    nj = D_FF // tf
    gate = lambda i, j: (0, jnp.minimum(j, nj - 1))
    up = lambda i, j: (0, jnp.minimum(j, nj - 1) + nj)
    return pl.pallas_call(
        functools.partial(_ffn_kernel, tiles_per_seq=seq // tm),
        grid=(m // tm, nj + 1),
        in_specs=[
            pl.BlockSpec((tm, D_MODEL), lambda i, j: (i, 0)),
            pl.BlockSpec((1, D_MODEL), lambda i, j: (0, 0)),
            pl.BlockSpec((D_MODEL, tf), gate),
            pl.BlockSpec((D_MODEL, tf), up),
            pl.BlockSpec((CONV_W, tf), gate),
            pl.BlockSpec((CONV_W, tf), up),
            pl.BlockSpec((1, tf), gate),
            pl.BlockSpec((1, tf), up),
            pl.BlockSpec((tf, D_MODEL), lambda i, j: (jnp.maximum(j - 1, 0), 0)),
            pl.BlockSpec((1, D_MODEL), lambda i, j: (0, 0)),
        ],
        out_specs=pl.BlockSpec((tm, D_MODEL), lambda i, j: (i, 0)),
        out_shape=jax.ShapeDtypeStruct((m, D_MODEL), F32),
        scratch_shapes=[
            pltpu.VMEM((tm, D_MODEL), BF16),
            pltpu.VMEM((tm, D_MODEL), F32),
            pltpu.VMEM((nj, 2, SUBLANES, tf), F32),
            pltpu.VMEM((tm, tf), F32),
            pltpu.VMEM((tm, tf), F32),
            pltpu.VMEM((tm, tf), BF16),
        ],
        compiler_params=_cparams(("arbitrary", "arbitrary")),
        name="ffn",
    )(x1, gpre, w_up, w_up, conv_w, conv_w, conv_b, conv_b, w_down, gpost)


def _rel_bucket(delta):
    n = jnp.maximum(delta, 0)
    max_exact = N_BUCKETS // 2
    nf = jnp.maximum(n, 1).astype(F32)
    large = max_exact + (jnp.log(nf / max_exact) / math.log(MAX_DISTANCE / max_exact)
                         * (N_BUCKETS - max_exact)).astype(jnp.int32)
    large = jnp.minimum(large, N_BUCKETS - 1)
    return jnp.where(n < max_exact, n, large)


def _bias_kernel(tab_ref, bkt_ref, o_ref):
    h = pl.program_id(0)
    i = pl.program_id(1)
    bkt = bkt_ref[...]
    row = jnp.full(bkt.shape, NEG, F32)
    for bucket in range(N_BUCKETS):
        row = jnp.where(bkt == bucket, tab_ref[bucket, h], row)
    tr, width = o_ref.shape[1:]
    rows = jnp.concatenate([row] * (tr // SUBLANES), axis=0)
    o_ref[0] = pltpu.roll(rows, i * tr, axis=1, stride=1, stride_axis=0)[:, 0:width]


def _bias_strip(table, rows, width, off, hi):
    heads = table.shape[1]
    tr = LANES
    wp = width + rows
    t = jnp.arange(wp)
    dist = off - jnp.where(t < width, t, t - wp)
    bkt = jnp.where((dist < 0) | (dist >= hi), -1, _rel_bucket(dist)).astype(jnp.int32)
    bkt = jnp.broadcast_to(bkt[None, :], (SUBLANES, wp))
    return pl.pallas_call(
        _bias_kernel,
        grid=(heads, rows // tr),
        in_specs=[
            pl.BlockSpec(memory_space=pltpu.SMEM),
            pl.BlockSpec((SUBLANES, wp), lambda h, i: (0, 0)),
        ],
        out_specs=pl.BlockSpec((1, tr, width), lambda h, i: (h, i, 0)),
        out_shape=jax.ShapeDtypeStruct((heads, rows, width), F32),
        compiler_params=_cparams(("arbitrary", "arbitrary")),
        name="bias_strip",
    )(table, bkt)


def _selection_map(nb, nc_pad):
    r = SLC_LEN // CMP_STRIDE
    j = jnp.arange(nb)[:, None]
    c = jnp.arange(nc_pad)[None, :]
    off = r * j - c
    out = jnp.zeros((nb, nc_pad), F32)
    for mm in range(r):
        for nn in range(CMP_LEN // CMP_STRIDE):
            out = out + (off == mm + nn).astype(F32)
    return out * (c < nc_pad - 1)


def kernel(x, pre_mix_norm, w_in, lambda_q1, lambda_k1, lambda_q2, lambda_k2, diff_subln,
           cmp_pos_k, cmp_pos_v, cmp_k_w1, cmp_k_w2, cmp_v_w1, cmp_v_w2, rel_bias, w_out,
           post_mix_norm, pre_ffn_norm, w_up, conv_w, conv_b, w_down, post_ffn_norm):
    b, s, d = x.shape
    assert d == D_MODEL and s % DIFF_TQ == 0 and s // SLC_LEN >= SLC_TOPK
    m = b * s
    depth = w_in.shape[0]
    big = 1 << 30
    for l in range(depth):
        xf = x.reshape(m, d)
        w_main = w_in[l][:, :MAIN_COLS].astype(BF16)
        w_gate = jnp.pad(w_in[l][:, MAIN_COLS:].reshape(d, NSA_KV, 3 * NSA_GROUP),
                         ((0, 0), (0, 0), (0, LANES - 3 * NSA_GROUP)))
        w_gate = w_gate.reshape(d, NSA_KV * LANES).astype(BF16)
        proj, gates, ksx, rows16 = _inproj(xf, pre_mix_norm[l][None], w_main, w_gate, tm=1024,
                                           tn=512, seq=s)
        proj3 = proj.reshape(b, s, MAIN_COLS)

        bias_diff = rel_bias[:, :DIFF_HEADS]
        bias_nsa = rel_bias[:, DIFF_HEADS:]
        t_diff = _bias_strip((bias_diff - bias_diff[N_BUCKETS - 1]) * LOG2E,
                             DIFF_TQ, 2 * DIFF_TQ, DIFF_TQ, big)
        t_sel = _bias_strip((bias_nsa - bias_nsa[N_BUCKETS - 1]) * LOG2E,
                            NSA_TQ, TS_W, TS_OFF, big)
        t_win = _bias_strip(bias_nsa * LOG2E, NSA_TQ, TW_W, WINDOW, WINDOW)
        t_sel = t_sel.reshape(NSA_KV, NSA_GROUP * NSA_TQ, TS_W)
        t_win = t_win.reshape(NSA_KV, NSA_GROUP * NSA_TQ, TW_W)

        lam_init = 0.8 - 0.6 * math.exp(-0.3 * l)
        o_diff = _diff_attention(proj3, t_diff, lambda_q1[l][None], lambda_k1[l][None],
                                 lambda_q2[l][None], lambda_k2[l][None], diff_subln[l][None],
                                 lam_init)

        nrow = s // CMP_STRIDE

        r = rows16.reshape(2, NSA_KV, b, nrow, CMP_STRIDE * NSA_DK)
        pos = jnp.stack([cmp_pos_k[l].reshape(1, -1), cmp_pos_v[l].reshape(1, -1)])
        w1 = jnp.stack([cmp_k_w1[l], cmp_v_w1[l]]).astype(BF16)
        w2 = jnp.stack([cmp_k_w2[l], cmp_v_w2[l]]).astype(BF16)
        kvc = _compress(r, pos, w1, w2)

        nb = s // SLC_LEN
        o_nsa = _nsa_attention(proj3, kvc, ksx.reshape(b, s, -1), t_sel, t_win,
                               gates.reshape(b, s, -1), _selection_map(nb, nrow))

        half = DIFF_HEADS * 2 * DIFF_D
        x1 = _outproj(o_diff.reshape(m, -1), o_nsa.reshape(m, -1), xf,
                      w_out[l][:half].astype(BF16), w_out[l][half:].astype(BF16),
                      post_mix_norm[l][None], tm=512)
        x2 = _ffn(x1, pre_ffn_norm[l][None], w_up[l].astype(BF16), conv_w[l], conv_b[l][None],
                  w_down[l].astype(BF16), post_ffn_norm[l][None], tm=512, tf=512, seq=s)
        x = x2.reshape(b, s, d)
    return x
```

```python
import functools
import math

import jax
import jax.numpy as jnp
from jax import lax
from jax.experimental import pallas as pl
from jax.experimental.pallas import tpu as pltpu

F32 = jnp.float32
BF16 = jnp.bfloat16

D_MODEL = 2048
DIFF_HEADS = 8
DIFF_D = 64
NSA_HEADS = 8
NSA_KV = 2
NSA_GROUP = NSA_HEADS // NSA_KV
NSA_DK = 128
NSA_DV = 128
CMP_LEN = 32
CMP_STRIDE = 16
CMP_HIDDEN = 256
SLC_LEN = 64
SLC_TOPK = 16
WINDOW = 512
N_BUCKETS = 32
MAX_DISTANCE = 128
D_FF = 5632
CONV_W = 3
EPS = 1e-6
NEG = -1e30
FORCE = 1e30
TINY = 1e-30
LOG2E = math.log2(math.e)

LANES = 128
SUBLANES = 8
VMEM_LIMIT = 56 * 1024 * 1024
FFN_VMEM_LIMIT = 61 * 1024 * 1024

MAIN_COLS = 5632
N_GATES = NSA_HEADS * 3
COL_DQ, COL_DK, COL_DV, COL_NQ = 0, 1024, 2048, 3072
COL_KC, COL_VC, COL_KS, COL_VS, COL_KW, COL_VW = 4096, 4352, 4608, 4864, 5120, 5376

DIFF_TQ = 512
NSA_TQ = 256
FLASH_SUB = 512
NSA_TK = 512
NSA_WK = WINDOW + NSA_TQ
TS_OFF = 2 * NSA_TK - NSA_TQ
TS_W = TS_OFF + NSA_TK
TW_W = WINDOW + NSA_WK


def _cparams(sem, vmem_limit=VMEM_LIMIT):
    return pltpu.CompilerParams(dimension_semantics=sem, vmem_limit_bytes=vmem_limit)


def _rms(x, g):
    return x * lax.rsqrt(jnp.mean(x * x, axis=-1, keepdims=True) + EPS) * g


def _gelu_tanh(x):
    return 0.5 * x * (1.0 + jnp.tanh(math.sqrt(2.0 / math.pi) * (x + 0.044715 * (x * x * x))))


def _dot(a, b):
    return jnp.dot(a, b, preferred_element_type=F32)


def _dot_nt(a, b, precision=None):
    return lax.dot_general(a, b, (((1,), (1,)), ((), ())), preferred_element_type=F32,
                           precision=precision)


def _lane_tile(a, n):
    return a if n == 1 else jnp.concatenate([a] * n, axis=1)


def _online_softmax_update(s, v, m_ref, l_ref, acc_ref, bias=None):
    m_prev, l, acc = m_ref[...], l_ref[...], acc_ref[...]
    dv = v.shape[1]
    ones = jnp.ones((FLASH_SUB, LANES), BF16)
    for c in range(0, s.shape[1], FLASH_SUB):
        sc = s[:, c:c + FLASH_SUB]
        extra = None if bias is None else bias(c)
        if extra is not None:
            sc = sc + extra
        m_new = jnp.maximum(m_prev, jnp.max(sc, axis=1, keepdims=True))
        alpha = jnp.exp2(m_prev - m_new)
        p = jnp.exp2((sc - _lane_tile(m_new, FLASH_SUB // LANES)).astype(BF16))
        pv = _dot(p, jnp.concatenate([v[c:c + FLASH_SUB], ones], axis=1))
        l = alpha * l + pv[:, dv:]
        acc = alpha * acc + pv[:, 0:dv]
        m_prev = m_new
    m_ref[...], l_ref[...], acc_ref[...] = m_prev, l, acc


def _inproj_kernel(x_ref, g_ref, w_ref, wg_ref, proj_ref, gates_ref, ksx_ref, r16_ref, h_ref,
                   stage_ref, *, seq):
    i = pl.program_id(0)
    j = pl.program_id(1)
    tm, tn = proj_ref.shape

    @pl.when(j == 0)
    def _():
        h_ref[...] = _rms(x_ref[...], g_ref[...]).astype(BF16)
        gl = _dot(h_ref[...], wg_ref[...])
        gates_ref[...] = 1.0 / (1.0 + jnp.exp(-gl))

    acc = _dot(h_ref[...], w_ref[...])
    proj_ref[...] = acc.astype(BF16)

    @pl.when(j == COL_KS // tn)
    def _():
        nb = seq // SLC_LEN
        xw = NSA_DK + nb
        pos = lax.rem(i * tm + lax.broadcasted_iota(jnp.int32, (tm, nb), 0), seq)
        blk = lax.shift_right_logical(pos, int(math.log2(SLC_LEN)))
        onehot = jnp.where(blk == lax.broadcasted_iota(jnp.int32, (tm, nb), 1), 1.0, 0.0)
        for g in range(NSA_KV):
            ksx_ref[:, g * xw:g * xw + NSA_DK] = acc[:, g * NSA_DK:(g + 1) * NSA_DK].astype(BF16)
            ksx_ref[:, g * xw + NSA_DK:(g + 1) * xw] = onehot.astype(BF16)

    @pl.when(j == COL_KC // tn)
    def _():
        nr = tm // CMP_STRIDE
        for t in range(2):
            for g in range(NSA_KV):
                col = (t * NSA_KV + g) * NSA_DK
                stage_ref[...] = acc[:, col:col + NSA_DK]
                for p in range(CMP_STRIDE):
                    r16_ref[t, g, :, p * NSA_DK:(p + 1) * NSA_DK] = (
                        stage_ref[pl.ds(p, nr, stride=CMP_STRIDE), :].astype(BF16))


def _inproj(xf, g, w_main, w_gate, tm, tn, seq):
    m = xf.shape[0]
    assert COL_KS % tn == 0 and COL_KC % tn == 0 and tn == 2 * NSA_KV * NSA_DK
    xw = NSA_DK + seq // SLC_LEN
    ng = w_gate.shape[1]
    return pl.pallas_call(
        functools.partial(_inproj_kernel, seq=seq),
        grid=(m // tm, MAIN_COLS // tn),
        in_specs=[
            pl.BlockSpec((tm, D_MODEL), lambda i, j: (i, 0)),
            pl.BlockSpec((1, D_MODEL), lambda i, j: (0, 0)),
            pl.BlockSpec((D_MODEL, tn), lambda i, j: (0, j)),
            pl.BlockSpec((D_MODEL, ng), lambda i, j: (0, 0)),
        ],
        out_specs=[
            pl.BlockSpec((tm, tn), lambda i, j: (i, j)),
            pl.BlockSpec((tm, ng), lambda i, j: (i, 0)),
            pl.BlockSpec((tm, NSA_KV * xw), lambda i, j: (i, 0)),
            pl.BlockSpec((2, NSA_KV, tm // CMP_STRIDE, CMP_STRIDE * NSA_DK),
                         lambda i, j: (0, 0, i, 0)),
        ],
        out_shape=[
            jax.ShapeDtypeStruct((m, MAIN_COLS), BF16),
            jax.ShapeDtypeStruct((m, ng), F32),
            jax.ShapeDtypeStruct((m, NSA_KV * xw), BF16),
            jax.ShapeDtypeStruct((2, NSA_KV, m // CMP_STRIDE, CMP_STRIDE * NSA_DK), BF16),
        ],
        scratch_shapes=[pltpu.VMEM((tm, D_MODEL), BF16), pltpu.VMEM((tm, NSA_DK), F32)],
        compiler_params=_cparams(("arbitrary", "arbitrary")),
        name="inproj",
    )(xf, g, w_main, w_gate)


def _diff_kernel(q_ref, k_ref, v_ref, t_ref, lq1_ref, lk1_ref, lq2_ref, lk2_ref, sub_ref,
                 o_ref, qs_ref, m_ref, l_ref, acc_ref, *, lam_init):
    tq = DIFF_TQ
    i = pl.program_id(2)
    q = q_ref[0].astype(F32) * (DIFF_D ** -0.5 * LOG2E)
    lane = lax.broadcasted_iota(jnp.int32, q.shape, 1)
    qs_ref[...] = jnp.concatenate(
        [jnp.where(lane < DIFF_D, q, 0.0), jnp.where(lane >= DIFF_D, q, 0.0)], axis=0).astype(BF16)
    m_ref[...] = jnp.full(m_ref.shape, NEG, F32)
    l_ref[...] = jnp.zeros(l_ref.shape, F32)
    acc_ref[...] = jnp.zeros(acc_ref.shape, F32)

    def step(kstart, width, t_col):
        def bias(c):
            if t_col is None or t_col + c < 0:
                return None
            t = t_ref[0, :, t_col + c:t_col + c + FLASH_SUB]
            return jnp.concatenate([t, t], axis=0)

        k = k_ref[0, pl.ds(kstart, width), :]
        s = _dot_nt(qs_ref[...], k)
        _online_softmax_update(s, v_ref[0, pl.ds(kstart, width), :], m_ref, l_ref, acc_ref, bias)

    nfar = jnp.maximum(i - 1, 0)

    def far_body(j, carry):
        step(pl.multiple_of(j * (2 * tq), 2 * tq), 2 * tq, None)
        return carry

    lax.fori_loop(0, lax.shift_right_logical(nfar, 1), far_body, 0)
    odd = lax.rem(nfar, 2) == 1

    @pl.when(odd)
    def _():
        step(pl.multiple_of((i - 2) * tq, tq), 3 * tq, -tq)

    @pl.when(jnp.logical_and(i > 0, jnp.logical_not(odd)))
    def _():
        step(pl.multiple_of((i - 1) * tq, tq), 2 * tq, 0)

    @pl.when(i == 0)
    def _():
        step(0, tq, tq)

    o = acc_ref[...] / jnp.maximum(l_ref[...], TINY)
    lam = (jnp.exp(jnp.sum(lq1_ref[...] * lk1_ref[...], axis=1, keepdims=True))
           - jnp.exp(jnp.sum(lq2_ref[...] * lk2_ref[...], axis=1, keepdims=True)) + lam_init)
    a = o[0:tq] - lam * o[tq:2 * tq]
    o_ref[0] = (_rms(a, sub_ref[...]) * (1.0 - lam_init)).astype(BF16)


def _diff_attention(proj3, t_diff, lq1, lk1, lq2, lk2, subln, lam_init):
    b, s, _ = proj3.shape
    tq = DIFF_TQ
    vec = lambda n: pl.BlockSpec((1, n), lambda bi, h, i: (0, 0))
    return pl.pallas_call(
        functools.partial(_diff_kernel, lam_init=lam_init),
        grid=(b, DIFF_HEADS, s // tq),
        in_specs=[
            pl.BlockSpec((1, tq, LANES), lambda bi, h, i: (bi, i, COL_DQ // LANES + h)),
            pl.BlockSpec((1, s, LANES), lambda bi, h, i: (bi, 0, COL_DK // LANES + h)),
            pl.BlockSpec((1, s, LANES), lambda bi, h, i: (bi, 0, COL_DV // LANES + h)),
            pl.BlockSpec((1, tq, 2 * tq), lambda bi, h, i: (h, 0, 0)),
            vec(DIFF_D), vec(DIFF_D), vec(DIFF_D), vec(DIFF_D), vec(2 * DIFF_D),
        ],
        out_specs=pl.BlockSpec((1, tq, LANES), lambda bi, h, i: (bi, i, h)),
        out_shape=jax.ShapeDtypeStruct((b, s, DIFF_HEADS * 2 * DIFF_D), BF16),
        scratch_shapes=[
            pltpu.VMEM((2 * tq, LANES), BF16),
            pltpu.VMEM((2 * tq, LANES), F32),
            pltpu.VMEM((2 * tq, LANES), F32),
            pltpu.VMEM((2 * tq, LANES), F32),
        ],
        compiler_params=_cparams(("arbitrary", "arbitrary", "arbitrary")),
        name="diff_attn",
    )(proj3, proj3, proj3, t_diff, lq1, lk1, lq2, lk2, subln)


def _compress_kernel(r_ref, pos_ref, w1_ref, w2_ref, o_ref):
    half = CMP_STRIDE * NSA_DK
    r = r_ref[0, 0, 0]
    n = r.shape[0]
    a = _dot(r, w1_ref[0, 0:half, :])
    bm = _dot(r, w1_ref[0, half:2 * half, :])
    posb = jnp.broadcast_to(pos_ref[0], (SUBLANES, 2 * half)).astype(BF16)
    pt = _dot(posb, w1_ref[0])[0:1]
    hid = a + pltpu.roll(bm, n - 1, axis=0) + pt
    o_ref[0, 0, 0] = _dot(_gelu_tanh(hid).astype(BF16), w2_ref[0]).astype(BF16)


def _compress(r, pos, w1, w2):
    _, g, b, n, width = r.shape
    return pl.pallas_call(
        _compress_kernel,
        grid=(2, b, g),
        in_specs=[
            pl.BlockSpec((1, 1, 1, n, width), lambda t, bi, gi: (t, gi, bi, 0, 0)),
            pl.BlockSpec((1, 1, CMP_LEN * NSA_DK), lambda t, bi, gi: (t, 0, 0)),
            pl.BlockSpec((1, CMP_LEN * NSA_DK, CMP_HIDDEN), lambda t, bi, gi: (t, 0, 0)),
            pl.BlockSpec((1, CMP_HIDDEN, NSA_DK), lambda t, bi, gi: (t, 0, 0)),
        ],
        out_specs=pl.BlockSpec((1, 1, 1, n, NSA_DK), lambda t, bi, gi: (t, bi, gi, 0, 0)),
        out_shape=jax.ShapeDtypeStruct((2, b, g, n, NSA_DK), BF16),
        compiler_params=_cparams(("arbitrary", "arbitrary", "arbitrary")),
        name="nsa_compress",
    )(r, pos, w1, w2)


def _nsa_kernel(q_ref, kc_ref, vc_ref, ks_ref, vs_ref, kw_ref, vw_ref, ts_ref, tw_ref, gate_ref,
                selmap_ref, o_ref, qx_ref, m_ref, l_ref, acc_ref, part_ref):
    tq, tk, hg = NSA_TQ, NSA_TK, NSA_GROUP
    rows = hg * tq
    i = pl.program_id(2)
    q0 = i * tq
    qt = q_ref[0]
    q4 = jnp.concatenate([qt[:, h * NSA_DK:(h + 1) * NSA_DK] for h in range(hg)], axis=0)
    q4 = (q4.astype(F32) * (NSA_DK ** -0.5 * LOG2E)).astype(BF16)

    kc = kc_ref[0, 0, 0]
    nc = kc.shape[0]
    cend = lax.broadcasted_iota(jnp.int32, (tq, nc), 1) * CMP_STRIDE + (CMP_LEN - 1)
    hidden = jnp.where(cend <= q0 + lax.broadcasted_iota(jnp.int32, (tq, nc), 0), 0.0, NEG)
    sc = _dot_nt(q4, kc) + jnp.concatenate([hidden] * hg, axis=0)
    mc = jnp.max(sc, axis=1, keepdims=True)
    pc = jnp.exp2(sc - mc)
    norm = jnp.where(mc > 0.5 * NEG,
                     1.0 / jnp.maximum(jnp.sum(pc, axis=1, keepdims=True), TINY), 0.0)
    pc = pc * norm
    o_cmp = _dot(pc.astype(BF16), vc_ref[0, 0, 0])

    pcsum = pc[0:tq] + pc[tq:2 * tq] + pc[2 * tq:3 * tq] + pc[3 * tq:4 * tq]
    imp_t = _dot_nt(selmap_ref[...], pcsum, precision=lax.Precision.HIGHEST)
    nb = imp_t.shape[0]
    blk = lax.broadcasted_iota(jnp.int32, (nb, tq), 0)
    cur = lax.shift_right_logical(q0 + lax.broadcasted_iota(jnp.int32, (nb, tq), 1),
                                  int(math.log2(SLC_LEN)))
    forced = (blk == 0) | (blk == cur) | (blk == cur - 1)
    excluded = -3.0e38
    val = jnp.where(forced | (blk > cur), excluded, imp_t)
    sel = jnp.where(forced, 1.0, 0.0)
    for _ in range(min(SLC_TOPK, nb) - 3):
        best = jnp.max(val, axis=0, keepdims=True)
        first = jnp.min(jnp.where(val == best, blk, nb), axis=0, keepdims=True)
        hit = blk == first
        sel = jnp.where(hit, 1.0, sel)
        val = jnp.where(hit, excluded, val)
    maskbias = jnp.where(sel.T > 0.5, 0.0, NEG).astype(BF16)
    qx_ref[:, 0:NSA_DK] = q4
    qx_ref[:, NSA_DK:NSA_DK + nb] = jnp.concatenate([maskbias] * hg, axis=0)

    kst = pl.multiple_of(jnp.maximum(q0 - WINDOW, 0), tq)
    w0 = pl.multiple_of(WINDOW - (q0 - kst), LANES)
    lw = _dot_nt(q4, kw_ref[0, pl.ds(kst, NSA_WK), :]) + tw_ref[0, :, pl.ds(w0, NSA_WK)]
    pw = jnp.exp2((lw - jnp.max(lw, axis=1, keepdims=True)).astype(BF16))
    vw1 = jnp.concatenate([vw_ref[0, pl.ds(kst, NSA_WK), :], jnp.ones((NSA_WK, LANES), BF16)],
                          axis=1)
    o_win = _dot(pw, vw1)
    o_win = o_win[:, 0:NSA_DV] / jnp.maximum(o_win[:, NSA_DV:], TINY)

    gt = gate_ref[0]
    gcol = lambda c: jnp.concatenate([gt[:, 3 * h + c:3 * h + c + 1] for h in range(hg)], axis=0)
    part_ref[...] = gcol(0) * o_cmp + gcol(2) * o_win

    m_ref[...] = jnp.full(m_ref.shape, NEG, F32)
    l_ref[...] = jnp.zeros(l_ref.shape, F32)
    acc_ref[...] = jnp.zeros(acc_ref.shape, F32)

    def step(j, width, first_near):
        kstart = pl.multiple_of(j * tk, tk)

        def bias(c):
            if first_near is None or c < first_near * tk:
                return None
            u0 = pl.multiple_of(TS_OFF - (q0 - j * tk) + c, LANES)
            return ts_ref[0, :, pl.ds(u0, FLASH_SUB)]

        s = _dot_nt(qx_ref[...], ks_ref[0, pl.ds(kstart, width), :])
        _online_softmax_update(s, vs_ref[0, pl.ds(kstart, width), :], m_ref, l_ref, acc_ref, bias)

    jl = lax.shift_right_logical(q0 + tq - 1, int(math.log2(tk)))
    nfar = jnp.maximum(jl - 1, 0)

    def far_body(j, carry):
        step(2 * j, 2 * tk, None)
        return carry

    lax.fori_loop(0, lax.shift_right_logical(nfar, 1), far_body, 0)
    odd = lax.rem(nfar, 2) == 1

    @pl.when(odd)
    def _():
        step(jl - 2, 3 * tk, 1)

    @pl.when(jnp.logical_and(jl > 0, jnp.logical_not(odd)))
    def _():
        step(jl - 1, 2 * tk, 0)

    @pl.when(jl == 0)
    def _():
        step(0, tk, 0)
    o = part_ref[...] + gcol(1) * (acc_ref[...] / jnp.maximum(l_ref[...], TINY))
    for h in range(hg):
        o_ref[0, :, h * NSA_DV:(h + 1) * NSA_DV] = o[h * tq:(h + 1) * tq].astype(BF16)


def _nsa_attention(proj3, kvc, ksx3, t_sel, t_win, gates3, selmap):
    b, s, _ = proj3.shape
    tq, hg = NSA_TQ, NSA_GROUP
    nc = kvc.shape[3]
    nb = s // SLC_LEN
    qcols = hg * NSA_DK
    return pl.pallas_call(
        _nsa_kernel,
        grid=(b, NSA_KV, s // tq),
        in_specs=[
            pl.BlockSpec((1, tq, qcols), lambda bi, g, i: (bi, i, COL_NQ // qcols + g)),
            pl.BlockSpec((1, 1, 1, nc, NSA_DK), lambda bi, g, i: (0, bi, g, 0, 0)),
            pl.BlockSpec((1, 1, 1, nc, NSA_DV), lambda bi, g, i: (1, bi, g, 0, 0)),
            pl.BlockSpec((1, s, NSA_DK + nb), lambda bi, g, i: (bi, 0, g)),
            pl.BlockSpec((1, s, NSA_DV), lambda bi, g, i: (bi, 0, COL_VS // NSA_DV + g)),
            pl.BlockSpec((1, s, NSA_DK), lambda bi, g, i: (bi, 0, COL_KW // NSA_DK + g)),
            pl.BlockSpec((1, s, NSA_DV), lambda bi, g, i: (bi, 0, COL_VW // NSA_DV + g)),
            pl.BlockSpec((1, hg * tq, TS_W), lambda bi, g, i: (g, 0, 0),
                         pipeline_mode=pl.Buffered(1)),
            pl.BlockSpec((1, hg * tq, TW_W), lambda bi, g, i: (g, 0, 0),
                         pipeline_mode=pl.Buffered(1)),
            pl.BlockSpec((1, tq, LANES), lambda bi, g, i: (bi, i, g)),
            pl.BlockSpec((nb, nc), lambda bi, g, i: (0, 0)),
        ],
        out_specs=pl.BlockSpec((1, tq, hg * NSA_DV), lambda bi, g, i: (bi, i, g)),
        out_shape=jax.ShapeDtypeStruct((b, s, NSA_HEADS * NSA_DV), BF16),
        scratch_shapes=[
            pltpu.VMEM((hg * tq, NSA_DK + nb), BF16),
            pltpu.VMEM((hg * tq, LANES), F32),
            pltpu.VMEM((hg * tq, LANES), F32),
            pltpu.VMEM((hg * tq, NSA_DV), F32),
            pltpu.VMEM((hg * tq, NSA_DV), F32),
        ],
        compiler_params=_cparams(("arbitrary", "arbitrary", "arbitrary")),
        name="nsa_attn",
    )(proj3, kvc, kvc, ksx3, proj3, proj3, proj3, t_sel, t_win, gates3, selmap)


def _outproj_kernel(od_ref, on_ref, x_ref, wd_ref, wn_ref, g_ref, o_ref):
    y = _dot(od_ref[...], wd_ref[...]) + _dot(on_ref[...], wn_ref[...])
    o_ref[...] = x_ref[...] + _rms(y, g_ref[...])


def _outproj(od, on, xf, w_d, w_n, g, tm):
    m = xf.shape[0]
    kd, kn = od.shape[1], on.shape[1]
    return pl.pallas_call(
        _outproj_kernel,
        grid=(m // tm,),
        in_specs=[
            pl.BlockSpec((tm, kd), lambda i: (i, 0)),
            pl.BlockSpec((tm, kn), lambda i: (i, 0)),
            pl.BlockSpec((tm, D_MODEL), lambda i: (i, 0)),
            pl.BlockSpec((kd, D_MODEL), lambda i: (0, 0)),
            pl.BlockSpec((kn, D_MODEL), lambda i: (0, 0)),
            pl.BlockSpec((1, D_MODEL), lambda i: (0, 0)),
        ],
        out_specs=pl.BlockSpec((tm, D_MODEL), lambda i: (i, 0)),
        out_shape=jax.ShapeDtypeStruct((m, D_MODEL), F32),
        compiler_params=_cparams(("arbitrary",)),
        name="outproj",
    )(od, on, xf, w_d, w_n, g)


def _ffn_kernel(x_ref, gpre_ref, wg_ref, wu_ref, cwg_ref, cwu_ref, cbg_ref, cbu_ref, wd_ref,
                gpost_ref, o_ref, h_ref, carry_ref, yg_ref, yu_ref, *, tiles_per_seq):
    i = pl.program_id(0)
    j = pl.program_id(1)
    tm = x_ref.shape[0]
    halo = SUBLANES

    @pl.when(j == 0)
    def _():
        h_ref[...] = _rms(x_ref[...], gpre_ref[...]).astype(BF16)
        o_ref[...] = jnp.zeros(o_ref.shape, F32)

    @pl.when(lax.rem(i, tiles_per_seq) == 0)
    def _():
        carry_ref[j] = jnp.zeros(carry_ref.shape[1:], F32)

    def conv(w_ref, cw_ref, cb_ref, slot, y_ref):
        u = _dot(h_ref[...], w_ref[...])
        prev = carry_ref[j, slot]
        carry_ref[j, slot] = u[tm - halo:tm]
        w0, w1, w2 = cw_ref[0:1, :], cw_ref[1:2, :], cw_ref[2:3, :]
        bias = cb_ref[...]

        def taps(z):
            return bias + w2 * z + w1 * pltpu.roll(z, 1, axis=0) + w0 * pltpu.roll(z, 2, axis=0)

        y_ref[...] = taps(u)
        y_ref[0:halo, :] = taps(jnp.concatenate([prev, u[0:halo]], axis=0))[halo:2 * halo]

    conv(wg_ref, cwg_ref, cbg_ref, 0, yg_ref)
    conv(wu_ref, cwu_ref, cbu_ref, 1, yu_ref)
    act = (_gelu_tanh(yg_ref[...]) * yu_ref[...]).astype(BF16)
    o_ref[...] += _dot(act, wd_ref[...])

    @pl.when(j == pl.num_programs(1) - 1)
    def _():
        o_ref[...] = x_ref[...] + _rms(o_ref[...], gpost_ref[...])


def _ffn(x1, gpre, w_up, conv_w, conv_b, w_down, gpost, tm, tf, seq):
    m = x1.shape[0]
    nj = D_FF // tf
    return pl.pallas_call(
        functools.partial(_ffn_kernel, tiles_per_seq=seq // tm),
        grid=(m // tm, nj),
        in_specs=[
            pl.BlockSpec((tm, D_MODEL), lambda i, j: (i, 0)),
            pl.BlockSpec((1, D_MODEL), lambda i, j: (0, 0)),
            pl.BlockSpec((D_MODEL, tf), lambda i, j: (0, j)),
            pl.BlockSpec((D_MODEL, tf), lambda i, j: (0, j + nj)),
            pl.BlockSpec((CONV_W, tf), lambda i, j: (0, j)),
            pl.BlockSpec((CONV_W, tf), lambda i, j: (0, j + nj)),
            pl.BlockSpec((1, tf), lambda i, j: (0, j)),
            pl.BlockSpec((1, tf), lambda i, j: (0, j + nj)),
            pl.BlockSpec((tf, D_MODEL), lambda i, j: (j, 0)),
            pl.BlockSpec((1, D_MODEL), lambda i, j: (0, 0)),
        ],
        out_specs=pl.BlockSpec((tm, D_MODEL), lambda i, j: (i, 0)),
        out_shape=jax.ShapeDtypeStruct((m, D_MODEL), F32),
        scratch_shapes=[
            pltpu.VMEM((tm, D_MODEL), BF16),
            pltpu.VMEM((nj, 2, SUBLANES, tf), F32),
            pltpu.VMEM((tm, tf), F32),
            pltpu.VMEM((tm, tf), F32),
        ],
        compiler_params=_cparams(("arbitrary", "arbitrary"), FFN_VMEM_LIMIT),
        name="ffn",
    )(x1, gpre, w_up, w_up, conv_w, conv_w, conv_b, conv_b, w_down, gpost)


def _rel_bucket(delta):
    n = jnp.maximum(delta, 0)
    max_exact = N_BUCKETS // 2
    nf = jnp.maximum(n, 1).astype(F32)
    large = max_exact + (jnp.log(nf / max_exact) / math.log(MAX_DISTANCE / max_exact)
                         * (N_BUCKETS - max_exact)).astype(jnp.int32)
    large = jnp.minimum(large, N_BUCKETS - 1)
    return jnp.where(n < max_exact, n, large)


def _bias_kernel(tab_ref, bkt_ref, o_ref):
    h = pl.program_id(0)
    i = pl.program_id(1)
    bkt = bkt_ref[...]
    row = jnp.full(bkt.shape, NEG, F32)
    for bucket in range(N_BUCKETS):
        row = jnp.where(bkt == bucket, tab_ref[bucket, h], row)
    tr, width = o_ref.shape[1:]
    rows = jnp.concatenate([row] * (tr // SUBLANES), axis=0)
    o_ref[0] = pltpu.roll(rows, i * tr, axis=1, stride=1, stride_axis=0)[:, 0:width]


def _bias_strip(table, rows, width, off, hi):
    heads = table.shape[1]
    tr = LANES
    wp = width + rows
    t = jnp.arange(wp)
    dist = off - jnp.where(t < width, t, t - wp)
    bkt = jnp.where((dist < 0) | (dist >= hi), -1, _rel_bucket(dist)).astype(jnp.int32)
    bkt = jnp.broadcast_to(bkt[None, :], (SUBLANES, wp))
    return pl.pallas_call(
        _bias_kernel,
        grid=(heads, rows // tr),
        in_specs=[
            pl.BlockSpec(memory_space=pltpu.SMEM),
            pl.BlockSpec((SUBLANES, wp), lambda h, i: (0, 0)),
        ],
        out_specs=pl.BlockSpec((1, tr, width), lambda h, i: (h, i, 0)),
        out_shape=jax.ShapeDtypeStruct((heads, rows, width), F32),
        compiler_params=_cparams(("arbitrary", "arbitrary")),
        name="bias_strip",
    )(table, bkt)


def _selection_map(nb, nc_pad):
    r = SLC_LEN // CMP_STRIDE
    j = jnp.arange(nb)[:, None]
    c = jnp.arange(nc_pad)[None, :]
    off = r * j - c
    out = jnp.zeros((nb, nc_pad), F32)
    for mm in range(r):
        for nn in range(CMP_LEN // CMP_STRIDE):
            out = out + (off == mm + nn).astype(F32)
    return out * (c < nc_pad - 1)


def kernel(x, pre_mix_norm, w_in, lambda_q1, lambda_k1, lambda_q2, lambda_k2, diff_subln,
           cmp_pos_k, cmp_pos_v, cmp_k_w1, cmp_k_w2, cmp_v_w1, cmp_v_w2, rel_bias, w_out,
           post_mix_norm, pre_ffn_norm, w_up, conv_w, conv_b, w_down, post_ffn_norm):
    b, s, d = x.shape
    assert d == D_MODEL and s % DIFF_TQ == 0 and s // SLC_LEN >= SLC_TOPK
    m = b * s
    depth = w_in.shape[0]
    big = 1 << 30
    for l in range(depth):
        xf = x.reshape(m, d)
        w_main = w_in[l].astype(BF16)
        w_gate = jnp.pad(w_in[l][:, MAIN_COLS:].reshape(d, NSA_KV, 3 * NSA_GROUP),
                         ((0, 0), (0, 0), (0, LANES - 3 * NSA_GROUP)))
        w_gate = w_gate.reshape(d, NSA_KV * LANES).astype(BF16)
        proj, gates, ksx, rows16 = _inproj(xf, pre_mix_norm[l][None], w_main, w_gate, tm=1024,
                                           tn=512, seq=s)
        proj3 = proj.reshape(b, s, MAIN_COLS)

        bias_diff = rel_bias[:, :DIFF_HEADS]
        bias_nsa = rel_bias[:, DIFF_HEADS:]
        t_diff = _bias_strip((bias_diff - bias_diff[N_BUCKETS - 1]) * LOG2E,
                             DIFF_TQ, 2 * DIFF_TQ, DIFF_TQ, big)
        t_sel = _bias_strip((bias_nsa - bias_nsa[N_BUCKETS - 1]) * LOG2E,
                            NSA_TQ, TS_W, TS_OFF, big)
        t_win = _bias_strip(bias_nsa * LOG2E, NSA_TQ, TW_W, WINDOW, WINDOW)
        t_sel = t_sel.reshape(NSA_KV, NSA_GROUP * NSA_TQ, TS_W)
        t_win = t_win.reshape(NSA_KV, NSA_GROUP * NSA_TQ, TW_W)

        lam_init = 0.8 - 0.6 * math.exp(-0.3 * l)
        o_diff = _diff_attention(proj3, t_diff, lambda_q1[l][None], lambda_k1[l][None],
                                 lambda_q2[l][None], lambda_k2[l][None], diff_subln[l][None],
                                 lam_init)

        nrow = s // CMP_STRIDE

        r = rows16.reshape(2, NSA_KV, b, nrow, CMP_STRIDE * NSA_DK)
        pos = jnp.stack([cmp_pos_k[l].reshape(1, -1), cmp_pos_v[l].reshape(1, -1)])
        w1 = jnp.stack([cmp_k_w1[l], cmp_v_w1[l]]).astype(BF16)
        w2 = jnp.stack([cmp_k_w2[l], cmp_v_w2[l]]).astype(BF16)
        kvc = _compress(r, pos, w1, w2)

        nb = s // SLC_LEN
        o_nsa = _nsa_attention(proj3, kvc, ksx.reshape(b, s, -1), t_sel, t_win,
                               gates.reshape(b, s, -1), _selection_map(nb, nrow))

        half = DIFF_HEADS * 2 * DIFF_D
        x1 = _outproj(o_diff.reshape(m, -1), o_nsa.reshape(m, -1), xf,
                      w_out[l][:half].astype(BF16), w_out[l][half:].astype(BF16),
                      post_mix_norm[l][None], tm=512)
        x2 = _ffn(x1, pre_ffn_norm[l][None], w_up[l].astype(BF16), conv_w[l], conv_b[l][None],
                  w_down[l].astype(BF16), post_ffn_norm[l][None], tm=1024, tf=512, seq=s)
        x = x2.reshape(b, s, d)
    return x
```

```python
import functools
import math

import jax
import jax.numpy as jnp
from jax import lax
from jax.experimental import pallas as pl
from jax.experimental.pallas import tpu as pltpu

F32 = jnp.float32
BF16 = jnp.bfloat16

D_MODEL = 2048
DIFF_HEADS = 8
DIFF_D = 64
NSA_HEADS = 8
NSA_KV = 2
NSA_GROUP = NSA_HEADS // NSA_KV
NSA_DK = 128
NSA_DV = 128
CMP_LEN = 32
CMP_STRIDE = 16
CMP_HIDDEN = 256
SLC_LEN = 64
SLC_TOPK = 16
WINDOW = 512
N_BUCKETS = 32
MAX_DISTANCE = 128
D_FF = 5632
CONV_W = 3
EPS = 1e-6
NEG = -1e30
FORCE = 1e30
TINY = 1e-30
LOG2E = math.log2(math.e)

LANES = 128
SUBLANES = 8
VMEM_LIMIT = 56 * 1024 * 1024
FFN_VMEM_LIMIT = 61 * 1024 * 1024

MAIN_COLS = 5632
N_GATES = NSA_HEADS * 3
COL_DQ, COL_DK, COL_DV, COL_NQ = 0, 1024, 2048, 3072
COL_KC, COL_VC, COL_KS, COL_VS, COL_KW, COL_VW = 4096, 4352, 4608, 4864, 5120, 5376

DIFF_TQ = 512
NSA_TQ = 256
FLASH_SUB = 512
FAR_WIDE = 2
NSA_TK = 512
NSA_WK = WINDOW + NSA_TQ
TS_OFF = 2 * NSA_TK - NSA_TQ
TS_W = TS_OFF + NSA_TK
TW_W = WINDOW + NSA_WK


def _cparams(sem, vmem_limit=VMEM_LIMIT):
    return pltpu.CompilerParams(dimension_semantics=sem, vmem_limit_bytes=vmem_limit)


def _rms(x, g):
    return x * lax.rsqrt(jnp.mean(x * x, axis=-1, keepdims=True) + EPS) * g


def _gelu_tanh(x):
    return 0.5 * x * (1.0 + jnp.tanh(math.sqrt(2.0 / math.pi) * (x + 0.044715 * (x * x * x))))


def _dot(a, b):
    return jnp.dot(a, b, preferred_element_type=F32)


def _dot_nt(a, b, precision=None):
    return lax.dot_general(a, b, (((1,), (1,)), ((), ())), preferred_element_type=F32,
                           precision=precision)


def _lane_tile(a, n):
    return a if n == 1 else jnp.concatenate([a] * n, axis=1)


def _online_softmax_update(s, v, m_ref, l_ref, acc_ref, bias=None):
    m_prev, l, acc = m_ref[...], l_ref[...], acc_ref[...]
    dv = v.shape[1]
    ones = jnp.ones((FLASH_SUB, LANES), BF16)
    for c in range(0, s.shape[1], FLASH_SUB):
        sc = s[:, c:c + FLASH_SUB]
        extra = None if bias is None else bias(c)
        if extra is not None:
            sc = sc + extra
        m_new = jnp.maximum(m_prev, jnp.max(sc, axis=1, keepdims=True))
        alpha = jnp.exp2(m_prev - m_new)
        p = jnp.exp2((sc - _lane_tile(m_new, FLASH_SUB // LANES)).astype(BF16))
        pv = _dot(p, jnp.concatenate([v[c:c + FLASH_SUB], ones], axis=1))
        l = alpha * l + pv[:, dv:]
        acc = alpha * acc + pv[:, 0:dv]
        m_prev = m_new
    m_ref[...], l_ref[...], acc_ref[...] = m_prev, l, acc


def _inproj_kernel(x_ref, g_ref, w_ref, wg_ref, proj_ref, gates_ref, ksx_ref, r16_ref, h_ref,
                   stage_ref, *, seq):
    i = pl.program_id(0)
    j = pl.program_id(1)
    tm, tn = proj_ref.shape

    @pl.when(j == 0)
    def _():
        h_ref[...] = _rms(x_ref[...], g_ref[...]).astype(BF16)
        gl = _dot(h_ref[...], wg_ref[...])
        gates_ref[...] = 1.0 / (1.0 + jnp.exp(-gl))

    acc = _dot(h_ref[...], w_ref[...])
    proj_ref[...] = acc.astype(BF16)

    @pl.when(j == COL_KS // tn)
    def _():
        nb = seq // SLC_LEN
        xw = NSA_DK + nb
        pos = lax.rem(i * tm + lax.broadcasted_iota(jnp.int32, (tm, nb), 0), seq)
        blk = lax.shift_right_logical(pos, int(math.log2(SLC_LEN)))
        onehot = jnp.where(blk == lax.broadcasted_iota(jnp.int32, (tm, nb), 1), 1.0, 0.0)
        for g in range(NSA_KV):
            ksx_ref[:, g * xw:g * xw + NSA_DK] = acc[:, g * NSA_DK:(g + 1) * NSA_DK].astype(BF16)
            ksx_ref[:, g * xw + NSA_DK:(g + 1) * xw] = onehot.astype(BF16)

    @pl.when(j == COL_KC // tn)
    def _():
        nr = tm // CMP_STRIDE
        for t in range(2):
            for g in range(NSA_KV):
                col = (t * NSA_KV + g) * NSA_DK
                stage_ref[...] = acc[:, col:col + NSA_DK]
                for p in range(CMP_STRIDE):
                    r16_ref[t, g, :, p * NSA_DK:(p + 1) * NSA_DK] = (
                        stage_ref[pl.ds(p, nr, stride=CMP_STRIDE), :].astype(BF16))


def _inproj(xf, g, w_main, w_gate, tm, tn, seq):
    m = xf.shape[0]
    assert COL_KS % tn == 0 and COL_KC % tn == 0 and tn == 2 * NSA_KV * NSA_DK
    xw = NSA_DK + seq // SLC_LEN
    ng = w_gate.shape[1]
    return pl.pallas_call(
        functools.partial(_inproj_kernel, seq=seq),
        grid=(m // tm, MAIN_COLS // tn),
        in_specs=[
            pl.BlockSpec((tm, D_MODEL), lambda i, j: (i, 0)),
            pl.BlockSpec((1, D_MODEL), lambda i, j: (0, 0)),
            pl.BlockSpec((D_MODEL, tn), lambda i, j: (0, j)),
            pl.BlockSpec((D_MODEL, ng), lambda i, j: (0, 0)),
        ],
        out_specs=[
            pl.BlockSpec((tm, tn), lambda i, j: (i, j)),
            pl.BlockSpec((tm, ng), lambda i, j: (i, 0)),
            pl.BlockSpec((tm, NSA_KV * xw), lambda i, j: (i, 0)),
            pl.BlockSpec((2, NSA_KV, tm // CMP_STRIDE, CMP_STRIDE * NSA_DK),
                         lambda i, j: (0, 0, i, 0)),
        ],
        out_shape=[
            jax.ShapeDtypeStruct((m, MAIN_COLS), BF16),
            jax.ShapeDtypeStruct((m, ng), F32),
            jax.ShapeDtypeStruct((m, NSA_KV * xw), BF16),
            jax.ShapeDtypeStruct((2, NSA_KV, m // CMP_STRIDE, CMP_STRIDE * NSA_DK), BF16),
        ],
        scratch_shapes=[pltpu.VMEM((tm, D_MODEL), BF16), pltpu.VMEM((tm, NSA_DK), F32)],
        compiler_params=_cparams(("arbitrary", "arbitrary")),
        name="inproj",
    )(xf, g, w_main, w_gate)


def _diff_kernel(q_ref, k_ref, v_ref, t_ref, lq1_ref, lk1_ref, lq2_ref, lk2_ref, sub_ref,
                 o_ref, qs_ref, m_ref, l_ref, acc_ref, *, lam_init):
    tq = DIFF_TQ
    lam = (jnp.exp(jnp.sum(lq1_ref[...] * lk1_ref[...], axis=1, keepdims=True))
           - jnp.exp(jnp.sum(lq2_ref[...] * lk2_ref[...], axis=1, keepdims=True)) + lam_init)

    def query_tile(i, carry):
        _diff_query_tile(i, lam, q_ref, k_ref, v_ref, t_ref, sub_ref, o_ref, qs_ref, m_ref, l_ref,
                         acc_ref, lam_init)
        return carry

    lax.fori_loop(0, q_ref.shape[1] // tq, query_tile, 0)


def _diff_query_tile(i, lam, q_ref, k_ref, v_ref, t_ref, sub_ref, o_ref, qs_ref, m_ref, l_ref,
                     acc_ref, lam_init):
    tq = DIFF_TQ
    rows = pl.ds(pl.multiple_of(i * tq, tq), tq)
    q = q_ref[0, rows, :].astype(F32) * (DIFF_D ** -0.5 * LOG2E)
    lane = lax.broadcasted_iota(jnp.int32, q.shape, 1)
    qs_ref[...] = jnp.concatenate(
        [jnp.where(lane < DIFF_D, q, 0.0), jnp.where(lane >= DIFF_D, q, 0.0)], axis=0).astype(BF16)
    m_ref[...] = jnp.full(m_ref.shape, NEG, F32)
    l_ref[...] = jnp.zeros(l_ref.shape, F32)
    acc_ref[...] = jnp.zeros(acc_ref.shape, F32)

    def step(kstart, width, t_col):
        def bias(c):
            if t_col is None or t_col + c < 0:
                return None
            t = t_ref[0, :, t_col + c:t_col + c + FLASH_SUB]
            return jnp.concatenate([t, t], axis=0)

        k = k_ref[0, pl.ds(kstart, width), :]
        s = _dot_nt(qs_ref[...], k)
        _online_softmax_update(s, v_ref[0, pl.ds(kstart, width), :], m_ref, l_ref, acc_ref, bias)

    nfar = jnp.maximum(i - 1, 0)
    nwide = nfar // FAR_WIDE

    def far_body(j, carry):
        step(pl.multiple_of(j * (FAR_WIDE * tq), FAR_WIDE * tq), FAR_WIDE * tq, None)
        return carry

    lax.fori_loop(0, nwide, far_body, 0)
    left = nfar - nwide * FAR_WIDE
    odd = lax.rem(left, 2) == 1

    if FAR_WIDE > 2:
        @pl.when(left >= 2)
        def _():
            step(pl.multiple_of(nwide * (FAR_WIDE * tq), 2 * tq), 2 * tq, None)

    @pl.when(odd)
    def _():
        step(pl.multiple_of((i - 2) * tq, tq), 3 * tq, -tq)

    @pl.when(jnp.logical_and(i > 0, jnp.logical_not(odd)))
    def _():
        step(pl.multiple_of((i - 1) * tq, tq), 2 * tq, 0)

    @pl.when(i == 0)
    def _():
        step(0, tq, tq)

    o = acc_ref[...] / jnp.maximum(l_ref[...], TINY)
    a = o[0:tq] - lam * o[tq:2 * tq]
    o_ref[0, rows, :] = (_rms(a, sub_ref[...]) * (1.0 - lam_init)).astype(BF16)


def _diff_attention(proj3, t_diff, lq1, lk1, lq2, lk2, subln, lam_init):
    b, s, _ = proj3.shape
    tq = DIFF_TQ
    vec = lambda n: pl.BlockSpec((1, n), lambda bi, h: (0, 0))
    return pl.pallas_call(
        functools.partial(_diff_kernel, lam_init=lam_init),
        grid=(b, DIFF_HEADS),
        in_specs=[
            pl.BlockSpec((1, s, LANES), lambda bi, h: (bi, 0, COL_DQ // LANES + h)),
            pl.BlockSpec((1, s, LANES), lambda bi, h: (bi, 0, COL_DK // LANES + h)),
            pl.BlockSpec((1, s, LANES), lambda bi, h: (bi, 0, COL_DV // LANES + h)),
            pl.BlockSpec((1, tq, 2 * tq), lambda bi, h: (h, 0, 0)),
            vec(DIFF_D), vec(DIFF_D), vec(DIFF_D), vec(DIFF_D), vec(2 * DIFF_D),
        ],
        out_specs=pl.BlockSpec((1, s, LANES), lambda bi, h: (bi, 0, h)),
        out_shape=jax.ShapeDtypeStruct((b, s, DIFF_HEADS * 2 * DIFF_D), BF16),
        scratch_shapes=[
            pltpu.VMEM((2 * tq, LANES), BF16),
            pltpu.VMEM((2 * tq, LANES), F32),
            pltpu.VMEM((2 * tq, LANES), F32),
            pltpu.VMEM((2 * tq, LANES), F32),
        ],
        compiler_params=_cparams(("arbitrary", "arbitrary")),
        name="diff_attn",
    )(proj3, proj3, proj3, t_diff, lq1, lk1, lq2, lk2, subln)


def _compress_kernel(r_ref, pos_ref, w1_ref, w2_ref, o_ref):
    half = CMP_STRIDE * NSA_DK
    r = r_ref[0, 0, 0]
    n = r.shape[0]
    a = _dot(r, w1_ref[0, 0:half, :])
    bm = _dot(r, w1_ref[0, half:2 * half, :])
    posb = jnp.broadcast_to(pos_ref[0], (SUBLANES, 2 * half)).astype(BF16)
    pt = _dot(posb, w1_ref[0])[0:1]
    hid = a + pltpu.roll(bm, n - 1, axis=0) + pt
    o_ref[0, 0, 0] = _dot(_gelu_tanh(hid).astype(BF16), w2_ref[0]).astype(BF16)


def _compress(r, pos, w1, w2):
    _, g, b, n, width = r.shape
    return pl.pallas_call(
        _compress_kernel,
        grid=(2, b, g),
        in_specs=[
            pl.BlockSpec((1, 1, 1, n, width), lambda t, bi, gi: (t, gi, bi, 0, 0)),
            pl.BlockSpec((1, 1, CMP_LEN * NSA_DK), lambda t, bi, gi: (t, 0, 0)),
            pl.BlockSpec((1, CMP_LEN * NSA_DK, CMP_HIDDEN), lambda t, bi, gi: (t, 0, 0)),
            pl.BlockSpec((1, CMP_HIDDEN, NSA_DK), lambda t, bi, gi: (t, 0, 0)),
        ],
        out_specs=pl.BlockSpec((1, 1, 1, n, NSA_DK), lambda t, bi, gi: (t, bi, gi, 0, 0)),
        out_shape=jax.ShapeDtypeStruct((2, b, g, n, NSA_DK), BF16),
        compiler_params=_cparams(("arbitrary", "arbitrary", "arbitrary")),
        name="nsa_compress",
    )(r, pos, w1, w2)


def _nsa_kernel(q_ref, kc_ref, vc_ref, ks_ref, vs_ref, kw_ref, vw_ref, ts_ref, tw_ref, gate_ref,
                selmap_ref, o_ref, qx_ref, m_ref, l_ref, acc_ref, part_ref):
    tq, tk, hg = NSA_TQ, NSA_TK, NSA_GROUP
    rows = hg * tq
    i = pl.program_id(2)
    q0 = i * tq
    qt = q_ref[0]
    q4 = jnp.concatenate([qt[:, h * NSA_DK:(h + 1) * NSA_DK] for h in range(hg)], axis=0)
    q4 = (q4.astype(F32) * (NSA_DK ** -0.5 * LOG2E)).astype(BF16)

    kc = kc_ref[0, 0, 0]
    nc = kc.shape[0]
    cend = lax.broadcasted_iota(jnp.int32, (tq, nc), 1) * CMP_STRIDE + (CMP_LEN - 1)
    hidden = jnp.where(cend <= q0 + lax.broadcasted_iota(jnp.int32, (tq, nc), 0), 0.0, NEG)
    sc = _dot_nt(q4, kc) + jnp.concatenate([hidden] * hg, axis=0)
    mc = jnp.max(sc, axis=1, keepdims=True)
    pc = jnp.exp2(sc - mc)
    norm = jnp.where(mc > 0.5 * NEG,
                     1.0 / jnp.maximum(jnp.sum(pc, axis=1, keepdims=True), TINY), 0.0)
    pc = pc * norm
    o_cmp = _dot(pc.astype(BF16), vc_ref[0, 0, 0])

    pcsum = pc[0:tq] + pc[tq:2 * tq] + pc[2 * tq:3 * tq] + pc[3 * tq:4 * tq]
    p_hi = pcsum.astype(BF16)
    p_res = pcsum - p_hi.astype(F32)
    p_mid = p_res.astype(BF16)
    p_lo = (p_res - p_mid.astype(F32)).astype(BF16)
    selmap = selmap_ref[...]
    imp_t = _dot_nt(selmap, p_hi) + _dot_nt(selmap, p_mid) + _dot_nt(selmap, p_lo)
    nb = imp_t.shape[0]
    blk = lax.broadcasted_iota(jnp.int32, (nb, tq), 0)
    cur = lax.shift_right_logical(q0 + lax.broadcasted_iota(jnp.int32, (nb, tq), 1),
                                  int(math.log2(SLC_LEN)))
    forced = (blk == 0) | (blk == cur) | (blk == cur - 1)
    excluded = -3.0e38
    val = jnp.where(forced | (blk > cur), excluded, imp_t)
    for _ in range(min(SLC_TOPK, nb) - 3):
        best = jnp.max(val, axis=0, keepdims=True)
        first = jnp.min(jnp.where(val == best, blk, nb), axis=0, keepdims=True)
        val = jnp.where(blk == first, excluded, val)
    maskbias = jnp.where(val.T < 0.5 * excluded, 0.0, NEG).astype(BF16)
    qx_ref[:, 0:NSA_DK] = q4
    qx_ref[:, NSA_DK:NSA_DK + nb] = jnp.concatenate([maskbias] * hg, axis=0)

    kst = pl.multiple_of(jnp.maximum(q0 - WINDOW, 0), tq)
    w0 = pl.multiple_of(WINDOW - (q0 - kst), LANES)
    lw = _dot_nt(q4, kw_ref[0, pl.ds(kst, NSA_WK), :]) + tw_ref[0, :, pl.ds(w0, NSA_WK)]
    pw = jnp.exp2((lw - jnp.max(lw, axis=1, keepdims=True)).astype(BF16))
    vw1 = jnp.concatenate([vw_ref[0, pl.ds(kst, NSA_WK), :], jnp.ones((NSA_WK, LANES), BF16)],
                          axis=1)
    o_win = _dot(pw, vw1)
    o_win = o_win[:, 0:NSA_DV] / jnp.maximum(o_win[:, NSA_DV:], TINY)

    gt = gate_ref[0]
    gcol = lambda c: jnp.concatenate([gt[:, 3 * h + c:3 * h + c + 1] for h in range(hg)], axis=0)
    part_ref[...] = gcol(0) * o_cmp + gcol(2) * o_win

    m_ref[...] = jnp.full(m_ref.shape, NEG, F32)
    l_ref[...] = jnp.zeros(l_ref.shape, F32)
    acc_ref[...] = jnp.zeros(acc_ref.shape, F32)

    def step(j, width, first_near):
        kstart = pl.multiple_of(j * tk, tk)

        def bias(c):
            if first_near is None or c < first_near * tk:
                return None
            u0 = pl.multiple_of(TS_OFF - (q0 - j * tk) + c, LANES)
            return ts_ref[0, :, pl.ds(u0, FLASH_SUB)]

        s = _dot_nt(qx_ref[...], ks_ref[0, pl.ds(kstart, width), :])
        _online_softmax_update(s, vs_ref[0, pl.ds(kstart, width), :], m_ref, l_ref, acc_ref, bias)

    jl = lax.shift_right_logical(q0 + tq - 1, int(math.log2(tk)))
    nfar = jnp.maximum(jl - 1, 0)
    nwide = nfar // FAR_WIDE

    def far_body(j, carry):
        step(FAR_WIDE * j, FAR_WIDE * tk, None)
        return carry

    lax.fori_loop(0, nwide, far_body, 0)
    left = nfar - nwide * FAR_WIDE
    odd = lax.rem(left, 2) == 1

    if FAR_WIDE > 2:
        @pl.when(left >= 2)
        def _():
            step(nwide * FAR_WIDE, 2 * tk, None)

    @pl.when(odd)
    def _():
        step(jl - 2, 3 * tk, 1)

    @pl.when(jnp.logical_and(jl > 0, jnp.logical_not(odd)))
    def _():
        step(jl - 1, 2 * tk, 0)

    @pl.when(jl == 0)
    def _():
        step(0, tk, 0)
    o = part_ref[...] + gcol(1) * (acc_ref[...] / jnp.maximum(l_ref[...], TINY))
    for h in range(hg):
        o_ref[0, :, h * NSA_DV:(h + 1) * NSA_DV] = o[h * tq:(h + 1) * tq].astype(BF16)


def _nsa_attention(proj3, kvc, ksx3, t_sel, t_win, gates3, selmap):
    b, s, _ = proj3.shape
    tq, hg = NSA_TQ, NSA_GROUP
    nc = kvc.shape[3]
    nb = s // SLC_LEN
    qcols = hg * NSA_DK
    return pl.pallas_call(
        _nsa_kernel,
        grid=(b, NSA_KV, s // tq),
        in_specs=[
            pl.BlockSpec((1, tq, qcols), lambda bi, g, i: (bi, i, COL_NQ // qcols + g)),
            pl.BlockSpec((1, 1, 1, nc, NSA_DK), lambda bi, g, i: (0, bi, g, 0, 0)),
            pl.BlockSpec((1, 1, 1, nc, NSA_DV), lambda bi, g, i: (1, bi, g, 0, 0)),
            pl.BlockSpec((1, s, NSA_DK + nb), lambda bi, g, i: (bi, 0, g)),
            pl.BlockSpec((1, s, NSA_DV), lambda bi, g, i: (bi, 0, COL_VS // NSA_DV + g)),
            pl.BlockSpec((1, s, NSA_DK), lambda bi, g, i: (bi, 0, COL_KW // NSA_DK + g)),
            pl.BlockSpec((1, s, NSA_DV), lambda bi, g, i: (bi, 0, COL_VW // NSA_DV + g)),
            pl.BlockSpec((1, hg * tq, TS_W), lambda bi, g, i: (g, 0, 0),
                         pipeline_mode=pl.Buffered(1)),
            pl.BlockSpec((1, hg * tq, TW_W), lambda bi, g, i: (g, 0, 0),
                         pipeline_mode=pl.Buffered(1)),
            pl.BlockSpec((1, tq, LANES), lambda bi, g, i: (bi, i, g)),
            pl.BlockSpec((nb, nc), lambda bi, g, i: (0, 0)),
        ],
        out_specs=pl.BlockSpec((1, tq, hg * NSA_DV), lambda bi, g, i: (bi, i, g)),
        out_shape=jax.ShapeDtypeStruct((b, s, NSA_HEADS * NSA_DV), BF16),
        scratch_shapes=[
            pltpu.VMEM((hg * tq, NSA_DK + nb), BF16),
            pltpu.VMEM((hg * tq, LANES), F32),
            pltpu.VMEM((hg * tq, LANES), F32),
            pltpu.VMEM((hg * tq, NSA_DV), F32),
            pltpu.VMEM((hg * tq, NSA_DV), F32),
        ],
        compiler_params=_cparams(("arbitrary", "arbitrary", "arbitrary")),
        name="nsa_attn",
    )(proj3, kvc, kvc, ksx3, proj3, proj3, proj3, t_sel, t_win, gates3, selmap)


def _outproj_kernel(od_ref, on_ref, x_ref, wd_ref, wn_ref, g_ref, o_ref):
    y = _dot(od_ref[...], wd_ref[...]) + _dot(on_ref[...], wn_ref[...])
    o_ref[...] = x_ref[...] + _rms(y, g_ref[...])


def _outproj(od, on, xf, w_d, w_n, g, tm):
    m = xf.shape[0]
    kd, kn = od.shape[1], on.shape[1]
    return pl.pallas_call(
        _outproj_kernel,
        grid=(m // tm,),
        in_specs=[
            pl.BlockSpec((tm, kd), lambda i: (i, 0)),
            pl.BlockSpec((tm, kn), lambda i: (i, 0)),
            pl.BlockSpec((tm, D_MODEL), lambda i: (i, 0)),
            pl.BlockSpec((kd, D_MODEL), lambda i: (0, 0)),
            pl.BlockSpec((kn, D_MODEL), lambda i: (0, 0)),
            pl.BlockSpec((1, D_MODEL), lambda i: (0, 0)),
        ],
        out_specs=pl.BlockSpec((tm, D_MODEL), lambda i: (i, 0)),
        out_shape=jax.ShapeDtypeStruct((m, D_MODEL), F32),
        compiler_params=_cparams(("arbitrary",)),
        name="outproj",
    )(od, on, xf, w_d, w_n, g)


def _ffn_kernel(x_ref, gpre_ref, wg_ref, wu_ref, cwg_ref, cwu_ref, cbg_ref, cbu_ref, wd_ref,
                gpost_ref, o_ref, h_ref, carry_ref, yg_ref, yu_ref, *, tiles_per_seq):
    i = pl.program_id(0)
    j = pl.program_id(1)
    tm = x_ref.shape[0]
    halo = SUBLANES

    @pl.when(j == 0)
    def _():
        h_ref[...] = _rms(x_ref[...], gpre_ref[...]).astype(BF16)
        o_ref[...] = jnp.zeros(o_ref.shape, F32)

    @pl.when(lax.rem(i, tiles_per_seq) == 0)
    def _():
        carry_ref[j] = jnp.zeros(carry_ref.shape[1:], F32)

    def conv(w_ref, cw_ref, cb_ref, slot, y_ref):
        u = _dot(h_ref[...], w_ref[...])
        prev = carry_ref[j, slot]
        carry_ref[j, slot] = u[tm - halo:tm]
        w0, w1, w2 = cw_ref[0:1, :], cw_ref[1:2, :], cw_ref[2:3, :]
        bias = cb_ref[...]

        def taps(z):
            return bias + w2 * z + w1 * pltpu.roll(z, 1, axis=0) + w0 * pltpu.roll(z, 2, axis=0)

        y_ref[...] = taps(u)
        y_ref[0:halo, :] = taps(jnp.concatenate([prev, u[0:halo]], axis=0))[halo:2 * halo]

    conv(wg_ref, cwg_ref, cbg_ref, 0, yg_ref)
    conv(wu_ref, cwu_ref, cbu_ref, 1, yu_ref)
    act = (_gelu_tanh(yg_ref[...]) * yu_ref[...]).astype(BF16)
    o_ref[...] += _dot(act, wd_ref[...])

    @pl.when(j == pl.num_programs(1) - 1)
    def _():
        o_ref[...] = x_ref[...] + _rms(o_ref[...], gpost_ref[...])


def _ffn(x1, gpre, w_up, conv_w, conv_b, w_down, gpost, tm, tf, seq):
    m = x1.shape[0]
    nj = D_FF // tf
    return pl.pallas_call(
        functools.partial(_ffn_kernel, tiles_per_seq=seq // tm),
        grid=(m // tm, nj),
        in_specs=[
            pl.BlockSpec((tm, D_MODEL), lambda i, j: (i, 0)),
            pl.BlockSpec((1, D_MODEL), lambda i, j: (0, 0)),
            pl.BlockSpec((D_MODEL, tf), lambda i, j: (0, j)),
            pl.BlockSpec((D_MODEL, tf), lambda i, j: (0, j + nj)),
            pl.BlockSpec((CONV_W, tf), lambda i, j: (0, j)),
            pl.BlockSpec((CONV_W, tf), lambda i, j: (0, j + nj)),
            pl.BlockSpec((1, tf), lambda i, j: (0, j)),
            pl.BlockSpec((1, tf), lambda i, j: (0, j + nj)),
            pl.BlockSpec((tf, D_MODEL), lambda i, j: (j, 0)),
            pl.BlockSpec((1, D_MODEL), lambda i, j: (0, 0)),
        ],
        out_specs=pl.BlockSpec((tm, D_MODEL), lambda i, j: (i, 0)),
        out_shape=jax.ShapeDtypeStruct((m, D_MODEL), F32),
        scratch_shapes=[
            pltpu.VMEM((tm, D_MODEL), BF16),
            pltpu.VMEM((nj, 2, SUBLANES, tf), F32),
            pltpu.VMEM((tm, tf), F32),
            pltpu.VMEM((tm, tf), F32),
        ],
        compiler_params=_cparams(("arbitrary", "arbitrary"), FFN_VMEM_LIMIT),
        name="ffn",
    )(x1, gpre, w_up, w_up, conv_w, conv_w, conv_b, conv_b, w_down, gpost)


def _rel_bucket(delta):
    n = jnp.maximum(delta, 0)
    max_exact = N_BUCKETS // 2
    nf = jnp.maximum(n, 1).astype(F32)
    large = max_exact + (jnp.log(nf / max_exact) / math.log(MAX_DISTANCE / max_exact)
                         * (N_BUCKETS - max_exact)).astype(jnp.int32)
    large = jnp.minimum(large, N_BUCKETS - 1)
    return jnp.where(n < max_exact, n, large)


def _bias_kernel(tab_ref, bkt_ref, o_ref):
    h = pl.program_id(0)
    i = pl.program_id(1)
    bkt = bkt_ref[...]
    row = jnp.full(bkt.shape, NEG, F32)
    for bucket in range(N_BUCKETS):
        row = jnp.where(bkt == bucket, tab_ref[bucket, h], row)
    tr, width = o_ref.shape[1:]
    rows = jnp.concatenate([row] * (tr // SUBLANES), axis=0)
    o_ref[0] = pltpu.roll(rows, i * tr, axis=1, stride=1, stride_axis=0)[:, 0:width]


def _bias_strip(table, rows, width, off, hi):
    heads = table.shape[1]
    tr = LANES
    wp = width + rows
    t = jnp.arange(wp)
    dist = off - jnp.where(t < width, t, t - wp)
    bkt = jnp.where((dist < 0) | (dist >= hi), -1, _rel_bucket(dist)).astype(jnp.int32)
    bkt = jnp.broadcast_to(bkt[None, :], (SUBLANES, wp))
    return pl.pallas_call(
        _bias_kernel,
        grid=(heads, rows // tr),
        in_specs=[
            pl.BlockSpec(memory_space=pltpu.SMEM),
            pl.BlockSpec((SUBLANES, wp), lambda h, i: (0, 0)),
        ],
        out_specs=pl.BlockSpec((1, tr, width), lambda h, i: (h, i, 0)),
        out_shape=jax.ShapeDtypeStruct((heads, rows, width), F32),
        compiler_params=_cparams(("arbitrary", "arbitrary")),
        name="bias_strip",
    )(table, bkt)


def _selection_map(nb, nc_pad):
    r = SLC_LEN // CMP_STRIDE
    j = jnp.arange(nb)[:, None]
    c = jnp.arange(nc_pad)[None, :]
    off = r * j - c
    out = jnp.zeros((nb, nc_pad), F32)
    for mm in range(r):
        for nn in range(CMP_LEN // CMP_STRIDE):
            out = out + (off == mm + nn).astype(F32)
    return out * (c < nc_pad - 1)


def kernel(x, pre_mix_norm, w_in, lambda_q1, lambda_k1, lambda_q2, lambda_k2, diff_subln,
           cmp_pos_k, cmp_pos_v, cmp_k_w1, cmp_k_w2, cmp_v_w1, cmp_v_w2, rel_bias, w_out,
           post_mix_norm, pre_ffn_norm, w_up, conv_w, conv_b, w_down, post_ffn_norm):
    b, s, d = x.shape
    assert d == D_MODEL and s % DIFF_TQ == 0 and s // SLC_LEN >= SLC_TOPK
    m = b * s
    depth = w_in.shape[0]
    big = 1 << 30
    for l in range(depth):
        xf = x.reshape(m, d)
        w_main = w_in[l].astype(BF16)
        w_gate = jnp.pad(w_in[l][:, MAIN_COLS:].reshape(d, NSA_KV, 3 * NSA_GROUP),
                         ((0, 0), (0, 0), (0, LANES - 3 * NSA_GROUP)))
        w_gate = w_gate.reshape(d, NSA_KV * LANES).astype(BF16)
        proj, gates, ksx, rows16 = _inproj(xf, pre_mix_norm[l][None], w_main, w_gate, tm=1024,
                                           tn=512, seq=s)
        proj3 = proj.reshape(b, s, MAIN_COLS)

        bias_diff = rel_bias[:, :DIFF_HEADS]
        bias_nsa = rel_bias[:, DIFF_HEADS:]
        t_diff = _bias_strip((bias_diff - bias_diff[N_BUCKETS - 1]) * LOG2E,
                             DIFF_TQ, 2 * DIFF_TQ, DIFF_TQ, big)
        t_sel = _bias_strip((bias_nsa - bias_nsa[N_BUCKETS - 1]) * LOG2E,
                            NSA_TQ, TS_W, TS_OFF, big)
        t_win = _bias_strip(bias_nsa * LOG2E, NSA_TQ, TW_W, WINDOW, WINDOW)
        t_sel = t_sel.reshape(NSA_KV, NSA_GROUP * NSA_TQ, TS_W)
        t_win = t_win.reshape(NSA_KV, NSA_GROUP * NSA_TQ, TW_W)

        lam_init = 0.8 - 0.6 * math.exp(-0.3 * l)
        o_diff = _diff_attention(proj3, t_diff, lambda_q1[l][None], lambda_k1[l][None],
                                 lambda_q2[l][None], lambda_k2[l][None], diff_subln[l][None],
                                 lam_init)

        nrow = s // CMP_STRIDE

        r = rows16.reshape(2, NSA_KV, b, nrow, CMP_STRIDE * NSA_DK)
        pos = jnp.stack([cmp_pos_k[l].reshape(1, -1), cmp_pos_v[l].reshape(1, -1)])
        w1 = jnp.stack([cmp_k_w1[l], cmp_v_w1[l]]).astype(BF16)
        w2 = jnp.stack([cmp_k_w2[l], cmp_v_w2[l]]).astype(BF16)
        kvc = _compress(r, pos, w1, w2)

        nb = s // SLC_LEN
        o_nsa = _nsa_attention(proj3, kvc, ksx.reshape(b, s, -1), t_sel, t_win,
                               gates.reshape(b, s, -1), _selection_map(nb, nrow).astype(BF16))

        half = DIFF_HEADS * 2 * DIFF_D
        x1 = _outproj(o_diff.reshape(m, -1), o_nsa.reshape(m, -1), xf,
                      w_out[l][:half].astype(BF16), w_out[l][half:].astype(BF16),
                      post_mix_norm[l][None], tm=512)
        x2 = _ffn(x1, pre_ffn_norm[l][None], w_up[l].astype(BF16), conv_w[l], conv_b[l][None],
                  w_down[l].astype(BF16), post_ffn_norm[l][None], tm=1024, tf=512, seq=s)
        x = x2.reshape(b, s, d)
    return x
```

```python
import functools
import math

import jax
import jax.numpy as jnp
from jax import lax
from jax.experimental import pallas as pl
from jax.experimental.pallas import tpu as pltpu

F32 = jnp.float32
BF16 = jnp.bfloat16

D_MODEL = 2048
DIFF_HEADS = 8
DIFF_D = 64
NSA_HEADS = 8
NSA_KV = 2
NSA_GROUP = NSA_HEADS // NSA_KV
NSA_DK = 128
NSA_DV = 128
CMP_LEN = 32
CMP_STRIDE = 16
CMP_HIDDEN = 256
SLC_LEN = 64
SLC_TOPK = 16
WINDOW = 512
N_BUCKETS = 32
MAX_DISTANCE = 128
D_FF = 5632
CONV_W = 3
EPS = 1e-6
NEG = -1e30
FORCE = 1e30
TINY = 1e-30
LOG2E = math.log2(math.e)

LANES = 128
SUBLANES = 8
VMEM_LIMIT = 56 * 1024 * 1024
FFN_VMEM_LIMIT = 61 * 1024 * 1024

MAIN_COLS = 5632
N_GATES = NSA_HEADS * 3
COL_DQ, COL_DK, COL_DV, COL_NQ = 0, 1024, 2048, 3072
COL_KC, COL_VC, COL_KS, COL_VS, COL_KW, COL_VW = 4096, 4352, 4608, 4864, 5120, 5376

DIFF_TQ = 512
NSA_TQ = 256
FLASH_SUB = 512
FAR_WIDE = 2
NSA_TK = 512
NSA_WK = WINDOW + NSA_TQ
TS_OFF = 2 * NSA_TK - NSA_TQ
TS_W = TS_OFF + NSA_TK
TW_W = WINDOW + NSA_WK


def _cparams(sem, vmem_limit=VMEM_LIMIT):
    return pltpu.CompilerParams(dimension_semantics=sem, vmem_limit_bytes=vmem_limit)


def _rms(x, g):
    return x * lax.rsqrt(jnp.mean(x * x, axis=-1, keepdims=True) + EPS) * g


def _gelu_tanh(x):
    return 0.5 * x * (1.0 + jnp.tanh(math.sqrt(2.0 / math.pi) * (x + 0.044715 * (x * x * x))))


def _dot(a, b):
    return jnp.dot(a, b, preferred_element_type=F32)


def _dot_nt(a, b, precision=None):
    return lax.dot_general(a, b, (((1,), (1,)), ((), ())), preferred_element_type=F32,
                           precision=precision)


def _lane_tile(a, n):
    return a if n == 1 else jnp.concatenate([a] * n, axis=1)


def _online_softmax_update(s, v, m_ref, l_ref, acc_ref, bias=None):
    m_prev, l, acc = m_ref[...], l_ref[...], acc_ref[...]
    dv = v.shape[1]
    ones = jnp.ones((FLASH_SUB, LANES), BF16)
    for c in range(0, s.shape[1], FLASH_SUB):
        sc = s[:, c:c + FLASH_SUB]
        extra = None if bias is None else bias(c)
        if extra is not None:
            sc = sc + extra
        m_new = jnp.maximum(m_prev, jnp.max(sc, axis=1, keepdims=True))
        alpha = jnp.exp2(m_prev - m_new)
        p = jnp.exp2((sc - _lane_tile(m_new, FLASH_SUB // LANES)).astype(BF16))
        pv = _dot(p, jnp.concatenate([v[c:c + FLASH_SUB], ones], axis=1))
        l = alpha * l + pv[:, dv:]
        acc = alpha * acc + pv[:, 0:dv]
        m_prev = m_new
    m_ref[...], l_ref[...], acc_ref[...] = m_prev, l, acc


def _inproj_kernel(x_ref, g_ref, w_ref, wg_ref, proj_ref, gates_ref, ksx_ref, r16_ref, h_ref,
                   stage_ref, *, seq):
    i = pl.program_id(0)
    j = pl.program_id(1)
    tm, tn = proj_ref.shape

    @pl.when(j == 0)
    def _():
        h_ref[...] = _rms(x_ref[...], g_ref[...]).astype(BF16)
        gl = _dot(h_ref[...], wg_ref[...])
        gates_ref[...] = 1.0 / (1.0 + jnp.exp(-gl))

    acc = _dot(h_ref[...], w_ref[...])
    proj_ref[...] = acc.astype(BF16)

    @pl.when(j == COL_KS // tn)
    def _():
        nb = seq // SLC_LEN
        xw = NSA_DK + nb
        pos = lax.rem(i * tm + lax.broadcasted_iota(jnp.int32, (tm, nb), 0), seq)
        blk = lax.shift_right_logical(pos, int(math.log2(SLC_LEN)))
        onehot = jnp.where(blk == lax.broadcasted_iota(jnp.int32, (tm, nb), 1), 1.0, 0.0)
        for g in range(NSA_KV):
            ksx_ref[:, g * xw:g * xw + NSA_DK] = acc[:, g * NSA_DK:(g + 1) * NSA_DK].astype(BF16)
            ksx_ref[:, g * xw + NSA_DK:(g + 1) * xw] = onehot.astype(BF16)

    @pl.when(j == COL_KC // tn)
    def _():
        nr = tm // CMP_STRIDE
        for t in range(2):
            for g in range(NSA_KV):
                col = (t * NSA_KV + g) * NSA_DK
                stage_ref[...] = acc[:, col:col + NSA_DK]
                for p in range(CMP_STRIDE):
                    r16_ref[t, g, :, p * NSA_DK:(p + 1) * NSA_DK] = (
                        stage_ref[pl.ds(p, nr, stride=CMP_STRIDE), :].astype(BF16))


def _inproj(xf, g, w_main, w_gate, tm, tn, seq):
    m = xf.shape[0]
    assert COL_KS % tn == 0 and COL_KC % tn == 0 and tn == 2 * NSA_KV * NSA_DK
    xw = NSA_DK + seq // SLC_LEN
    ng = w_gate.shape[1]
    return pl.pallas_call(
        functools.partial(_inproj_kernel, seq=seq),
        grid=(m // tm, MAIN_COLS // tn),
        in_specs=[
            pl.BlockSpec((tm, D_MODEL), lambda i, j: (i, 0)),
            pl.BlockSpec((1, D_MODEL), lambda i, j: (0, 0)),
            pl.BlockSpec((D_MODEL, tn), lambda i, j: (0, j)),
            pl.BlockSpec((D_MODEL, ng), lambda i, j: (0, 0)),
        ],
        out_specs=[
            pl.BlockSpec((tm, tn), lambda i, j: (i, j)),
            pl.BlockSpec((tm, ng), lambda i, j: (i, 0)),
            pl.BlockSpec((tm, NSA_KV * xw), lambda i, j: (i, 0)),
            pl.BlockSpec((2, NSA_KV, tm // CMP_STRIDE, CMP_STRIDE * NSA_DK),
                         lambda i, j: (0, 0, i, 0)),
        ],
        out_shape=[
            jax.ShapeDtypeStruct((m, MAIN_COLS), BF16),
            jax.ShapeDtypeStruct((m, ng), F32),
            jax.ShapeDtypeStruct((m, NSA_KV * xw), BF16),
            jax.ShapeDtypeStruct((2, NSA_KV, m // CMP_STRIDE, CMP_STRIDE * NSA_DK), BF16),
        ],
        scratch_shapes=[pltpu.VMEM((tm, D_MODEL), BF16), pltpu.VMEM((tm, NSA_DK), F32)],
        compiler_params=_cparams(("arbitrary", "arbitrary")),
        name="inproj",
    )(xf, g, w_main, w_gate)


def _diff_kernel(q_ref, k_ref, v_ref, t_ref, lq1_ref, lk1_ref, lq2_ref, lk2_ref, sub_ref,
                 o_ref, qs_ref, m_ref, l_ref, acc_ref, *, lam_init):
    tq = DIFF_TQ
    lam = (jnp.exp(jnp.sum(lq1_ref[...] * lk1_ref[...], axis=1, keepdims=True))
           - jnp.exp(jnp.sum(lq2_ref[...] * lk2_ref[...], axis=1, keepdims=True)) + lam_init)

    def query_tile(i, carry):
        _diff_query_tile(i, lam, q_ref, k_ref, v_ref, t_ref, sub_ref, o_ref, qs_ref, m_ref, l_ref,
                         acc_ref, lam_init)
        return carry

    lax.fori_loop(0, q_ref.shape[1] // tq, query_tile, 0)


def _diff_query_tile(i, lam, q_ref, k_ref, v_ref, t_ref, sub_ref, o_ref, qs_ref, m_ref, l_ref,
                     acc_ref, lam_init):
    tq = DIFF_TQ
    rows = pl.ds(pl.multiple_of(i * tq, tq), tq)
    q = q_ref[0, rows, :].astype(F32) * (DIFF_D ** -0.5 * LOG2E)
    lane = lax.broadcasted_iota(jnp.int32, q.shape, 1)
    qs_ref[...] = jnp.concatenate(
        [jnp.where(lane < DIFF_D, q, 0.0), jnp.where(lane >= DIFF_D, q, 0.0)], axis=0).astype(BF16)
    m_ref[...] = jnp.full(m_ref.shape, NEG, F32)
    l_ref[...] = jnp.zeros(l_ref.shape, F32)
    acc_ref[...] = jnp.zeros(acc_ref.shape, F32)

    def step(kstart, width, t_col):
        def bias(c):
            if t_col is None or t_col + c < 0:
                return None
            t = t_ref[0, :, t_col + c:t_col + c + FLASH_SUB]
            return jnp.concatenate([t, t], axis=0)

        k = k_ref[0, pl.ds(kstart, width), :]
        s = _dot_nt(qs_ref[...], k)
        _online_softmax_update(s, v_ref[0, pl.ds(kstart, width), :], m_ref, l_ref, acc_ref, bias)

    nfar = jnp.maximum(i - 1, 0)
    nwide = nfar // FAR_WIDE

    def far_body(j, carry):
        step(pl.multiple_of(j * (FAR_WIDE * tq), FAR_WIDE * tq), FAR_WIDE * tq, None)
        return carry

    lax.fori_loop(0, nwide, far_body, 0)
    left = nfar - nwide * FAR_WIDE
    odd = lax.rem(left, 2) == 1

    if FAR_WIDE > 2:
        @pl.when(left >= 2)
        def _():
            step(pl.multiple_of(nwide * (FAR_WIDE * tq), 2 * tq), 2 * tq, None)

    @pl.when(odd)
    def _():
        step(pl.multiple_of((i - 2) * tq, tq), 3 * tq, -tq)

    @pl.when(jnp.logical_and(i > 0, jnp.logical_not(odd)))
    def _():
        step(pl.multiple_of((i - 1) * tq, tq), 2 * tq, 0)

    @pl.when(i == 0)
    def _():
        step(0, tq, tq)

    o = acc_ref[...] / jnp.maximum(l_ref[...], TINY)
    a = o[0:tq] - lam * o[tq:2 * tq]
    o_ref[0, rows, :] = (_rms(a, sub_ref[...]) * (1.0 - lam_init)).astype(BF16)


def _diff_attention(proj3, t_diff, lq1, lk1, lq2, lk2, subln, lam_init):
    b, s, _ = proj3.shape
    tq = DIFF_TQ
    vec = lambda n: pl.BlockSpec((1, n), lambda bi, h: (0, 0))
    return pl.pallas_call(
        functools.partial(_diff_kernel, lam_init=lam_init),
        grid=(b, DIFF_HEADS),
        in_specs=[
            pl.BlockSpec((1, s, LANES), lambda bi, h: (bi, 0, COL_DQ // LANES + h)),
            pl.BlockSpec((1, s, LANES), lambda bi, h: (bi, 0, COL_DK // LANES + h)),
            pl.BlockSpec((1, s, LANES), lambda bi, h: (bi, 0, COL_DV // LANES + h)),
            pl.BlockSpec((1, tq, 2 * tq), lambda bi, h: (h, 0, 0)),
            vec(DIFF_D), vec(DIFF_D), vec(DIFF_D), vec(DIFF_D), vec(2 * DIFF_D),
        ],
        out_specs=pl.BlockSpec((1, s, LANES), lambda bi, h: (bi, 0, h)),
        out_shape=jax.ShapeDtypeStruct((b, s, DIFF_HEADS * 2 * DIFF_D), BF16),
        scratch_shapes=[
            pltpu.VMEM((2 * tq, LANES), BF16),
            pltpu.VMEM((2 * tq, LANES), F32),
            pltpu.VMEM((2 * tq, LANES), F32),
            pltpu.VMEM((2 * tq, LANES), F32),
        ],
        compiler_params=_cparams(("arbitrary", "arbitrary")),
        name="diff_attn",
    )(proj3, proj3, proj3, t_diff, lq1, lk1, lq2, lk2, subln)


def _compress_kernel(r_ref, pos_ref, w1_ref, w2_ref, o_ref):
    half = CMP_STRIDE * NSA_DK
    r = r_ref[0, 0, 0]
    n = r.shape[0]
    a = _dot(r, w1_ref[0, 0:half, :])
    bm = _dot(r, w1_ref[0, half:2 * half, :])
    posb = jnp.broadcast_to(pos_ref[0], (SUBLANES, 2 * half)).astype(BF16)
    pt = _dot(posb, w1_ref[0])[0:1]
    hid = a + pltpu.roll(bm, n - 1, axis=0) + pt
    o_ref[0, 0, 0] = _dot(_gelu_tanh(hid).astype(BF16), w2_ref[0]).astype(BF16)


def _compress(r, pos, w1, w2):
    _, g, b, n, width = r.shape
    return pl.pallas_call(
        _compress_kernel,
        grid=(2, b, g),
        in_specs=[
            pl.BlockSpec((1, 1, 1, n, width), lambda t, bi, gi: (t, gi, bi, 0, 0)),
            pl.BlockSpec((1, 1, CMP_LEN * NSA_DK), lambda t, bi, gi: (t, 0, 0)),
            pl.BlockSpec((1, CMP_LEN * NSA_DK, CMP_HIDDEN), lambda t, bi, gi: (t, 0, 0)),
            pl.BlockSpec((1, CMP_HIDDEN, NSA_DK), lambda t, bi, gi: (t, 0, 0)),
        ],
        out_specs=pl.BlockSpec((1, 1, 1, n, NSA_DK), lambda t, bi, gi: (t, bi, gi, 0, 0)),
        out_shape=jax.ShapeDtypeStruct((2, b, g, n, NSA_DK), BF16),
        compiler_params=_cparams(("arbitrary", "arbitrary", "arbitrary")),
        name="nsa_compress",
    )(r, pos, w1, w2)


def _nsa_kernel(q_ref, kc_ref, vc_ref, ks_ref, vs_ref, kw_ref, vw_ref, ts_ref, tw_ref, gate_ref,
                selmap_ref, o_ref, qx_ref, m_ref, l_ref, acc_ref, part_ref):
    tq, tk, hg = NSA_TQ, NSA_TK, NSA_GROUP
    rows = hg * tq
    i = pl.program_id(2)
    q0 = i * tq

    def gcol(c):
        return jnp.concatenate([gate_ref[0, :, 3 * h + c:3 * h + c + 1] for h in range(hg)], axis=0)

    def before_loop(nc):
        qt = q_ref[0]
        q4 = jnp.concatenate([qt[:, h * NSA_DK:(h + 1) * NSA_DK] for h in range(hg)], axis=0)
        q4 = (q4.astype(F32) * (NSA_DK ** -0.5 * LOG2E)).astype(BF16)

        kc = kc_ref[0, 0, 0, 0:nc, :]
        cend = lax.broadcasted_iota(jnp.int32, (tq, nc), 1) * CMP_STRIDE + (CMP_LEN - 1)
        hidden = jnp.where(cend <= q0 + lax.broadcasted_iota(jnp.int32, (tq, nc), 0), 0.0, NEG)
        sc = _dot_nt(q4, kc) + jnp.concatenate([hidden] * hg, axis=0)
        mc = jnp.max(sc, axis=1, keepdims=True)
        pc = jnp.exp2(sc - mc)
        norm = jnp.where(mc > 0.5 * NEG,
                         1.0 / jnp.maximum(jnp.sum(pc, axis=1, keepdims=True), TINY), 0.0)
        pc = pc * norm
        o_cmp = _dot(pc.astype(BF16), vc_ref[0, 0, 0, 0:nc, :])

        pcsum = pc[0:tq] + pc[tq:2 * tq] + pc[2 * tq:3 * tq] + pc[3 * tq:4 * tq]
        p_hi = pcsum.astype(BF16)
        p_res = pcsum - p_hi.astype(F32)
        p_mid = p_res.astype(BF16)
        p_lo = (p_res - p_mid.astype(F32)).astype(BF16)
        selmap = selmap_ref[:, 0:nc]
        imp_t = _dot_nt(selmap, p_hi) + _dot_nt(selmap, p_mid) + _dot_nt(selmap, p_lo)
        nb = imp_t.shape[0]
        blk = lax.broadcasted_iota(jnp.int32, (nb, tq), 0)
        cur = lax.shift_right_logical(q0 + lax.broadcasted_iota(jnp.int32, (nb, tq), 1),
                                      int(math.log2(SLC_LEN)))
        forced = (blk == 0) | (blk == cur) | (blk == cur - 1)
        excluded = -3.0e38
        val = jnp.where(forced | (blk > cur), excluded, imp_t)
        for _ in range(min(SLC_TOPK, nb) - 3):
            best = jnp.max(val, axis=0, keepdims=True)
            first = jnp.min(jnp.where(val == best, blk, nb), axis=0, keepdims=True)
            val = jnp.where(blk == first, excluded, val)
        maskbias = jnp.where(val.T < 0.5 * excluded, 0.0, NEG).astype(BF16)
        qx_ref[:, 0:NSA_DK] = q4
        qx_ref[:, NSA_DK:NSA_DK + nb] = jnp.concatenate([maskbias] * hg, axis=0)

        kst = pl.multiple_of(jnp.maximum(q0 - WINDOW, 0), tq)
        w0 = pl.multiple_of(WINDOW - (q0 - kst), LANES)
        lw = _dot_nt(q4, kw_ref[0, pl.ds(kst, NSA_WK), :]) + tw_ref[0, :, pl.ds(w0, NSA_WK)]
        pw = jnp.exp2((lw - jnp.max(lw, axis=1, keepdims=True)).astype(BF16))
        vw1 = jnp.concatenate([vw_ref[0, pl.ds(kst, NSA_WK), :], jnp.ones((NSA_WK, LANES), BF16)],
                              axis=1)
        o_win = _dot(pw, vw1)
        o_win = o_win[:, 0:NSA_DV] / jnp.maximum(o_win[:, NSA_DV:], TINY)

        part_ref[...] = gcol(0) * o_cmp + gcol(2) * o_win

    nc_all = kc_ref.shape[3]
    early = q0 + tq <= (nc_all // 2) * CMP_STRIDE

    @pl.when(early)
    def _():
        before_loop(nc_all // 2)

    @pl.when(jnp.logical_not(early))
    def _():
        before_loop(nc_all)

    m_ref[...] = jnp.full(m_ref.shape, NEG, F32)
    l_ref[...] = jnp.zeros(l_ref.shape, F32)
    acc_ref[...] = jnp.zeros(acc_ref.shape, F32)

    def step(j, width, first_near):
        kstart = pl.multiple_of(j * tk, tk)

        def bias(c):
            if first_near is None or c < first_near * tk:
                return None
            u0 = pl.multiple_of(TS_OFF - (q0 - j * tk) + c, LANES)
            return ts_ref[0, :, pl.ds(u0, FLASH_SUB)]

        s = _dot_nt(qx_ref[...], ks_ref[0, pl.ds(kstart, width), :])
        _online_softmax_update(s, vs_ref[0, pl.ds(kstart, width), :], m_ref, l_ref, acc_ref, bias)

    jl = lax.shift_right_logical(q0 + tq - 1, int(math.log2(tk)))
    nfar = jnp.maximum(jl - 1, 0)
    nwide = nfar // FAR_WIDE

    def far_body(j, carry):
        step(FAR_WIDE * j, FAR_WIDE * tk, None)
        return carry

    lax.fori_loop(0, nwide, far_body, 0)
    left = nfar - nwide * FAR_WIDE
    odd = lax.rem(left, 2) == 1

    if FAR_WIDE > 2:
        @pl.when(left >= 2)
        def _():
            step(nwide * FAR_WIDE, 2 * tk, None)

    @pl.when(odd)
    def _():
        step(jl - 2, 3 * tk, 1)

    @pl.when(jnp.logical_and(jl > 0, jnp.logical_not(odd)))
    def _():
        step(jl - 1, 2 * tk, 0)

    @pl.when(jl == 0)
    def _():
        step(0, tk, 0)
    o = part_ref[...] + gcol(1) * (acc_ref[...] / jnp.maximum(l_ref[...], TINY))
    for h in range(hg):
        o_ref[0, :, h * NSA_DV:(h + 1) * NSA_DV] = o[h * tq:(h + 1) * tq].astype(BF16)


def _nsa_attention(proj3, kvc, ksx3, t_sel, t_win, gates3, selmap):
    b, s, _ = proj3.shape
    tq, hg = NSA_TQ, NSA_GROUP
    nc = kvc.shape[3]
    nb = s // SLC_LEN
    qcols = hg * NSA_DK
    return pl.pallas_call(
        _nsa_kernel,
        grid=(b, NSA_KV, s // tq),
        in_specs=[
            pl.BlockSpec((1, tq, qcols), lambda bi, g, i: (bi, i, COL_NQ // qcols + g)),
            pl.BlockSpec((1, 1, 1, nc, NSA_DK), lambda bi, g, i: (0, bi, g, 0, 0)),
            pl.BlockSpec((1, 1, 1, nc, NSA_DV), lambda bi, g, i: (1, bi, g, 0, 0)),
            pl.BlockSpec((1, s, NSA_DK + nb), lambda bi, g, i: (bi, 0, g)),
            pl.BlockSpec((1, s, NSA_DV), lambda bi, g, i: (bi, 0, COL_VS // NSA_DV + g)),
            pl.BlockSpec((1, s, NSA_DK), lambda bi, g, i: (bi, 0, COL_KW // NSA_DK + g)),
            pl.BlockSpec((1, s, NSA_DV), lambda bi, g, i: (bi, 0, COL_VW // NSA_DV + g)),
            pl.BlockSpec((1, hg * tq, TS_W), lambda bi, g, i: (g, 0, 0),
                         pipeline_mode=pl.Buffered(1)),
            pl.BlockSpec((1, hg * tq, TW_W), lambda bi, g, i: (g, 0, 0),
                         pipeline_mode=pl.Buffered(1)),
            pl.BlockSpec((1, tq, LANES), lambda bi, g, i: (bi, i, g)),
            pl.BlockSpec((nb, nc), lambda bi, g, i: (0, 0)),
        ],
        out_specs=pl.BlockSpec((1, tq, hg * NSA_DV), lambda bi, g, i: (bi, i, g)),
        out_shape=jax.ShapeDtypeStruct((b, s, NSA_HEADS * NSA_DV), BF16),
        scratch_shapes=[
            pltpu.VMEM((hg * tq, NSA_DK + nb), BF16),
            pltpu.VMEM((hg * tq, LANES), F32),
            pltpu.VMEM((hg * tq, LANES), F32),
            pltpu.VMEM((hg * tq, NSA_DV), F32),
            pltpu.VMEM((hg * tq, NSA_DV), F32),
        ],
        compiler_params=_cparams(("arbitrary", "arbitrary", "arbitrary")),
        name="nsa_attn",
    )(proj3, kvc, kvc, ksx3, proj3, proj3, proj3, t_sel, t_win, gates3, selmap)


def _outproj_kernel(od_ref, on_ref, x_ref, wd_ref, wn_ref, g_ref, o_ref):
    y = _dot(od_ref[...], wd_ref[...]) + _dot(on_ref[...], wn_ref[...])
    o_ref[...] = x_ref[...] + _rms(y, g_ref[...])


def _outproj(od, on, xf, w_d, w_n, g, tm):
    m = xf.shape[0]
    kd, kn = od.shape[1], on.shape[1]
    return pl.pallas_call(
        _outproj_kernel,
        grid=(m // tm,),
        in_specs=[
            pl.BlockSpec((tm, kd), lambda i: (i, 0)),
            pl.BlockSpec((tm, kn), lambda i: (i, 0)),
            pl.BlockSpec((tm, D_MODEL), lambda i: (i, 0)),
            pl.BlockSpec((kd, D_MODEL), lambda i: (0, 0)),
            pl.BlockSpec((kn, D_MODEL), lambda i: (0, 0)),
            pl.BlockSpec((1, D_MODEL), lambda i: (0, 0)),
        ],
        out_specs=pl.BlockSpec((tm, D_MODEL), lambda i: (i, 0)),
        out_shape=jax.ShapeDtypeStruct((m, D_MODEL), F32),
        compiler_params=_cparams(("arbitrary",)),
        name="outproj",
    )(od, on, xf, w_d, w_n, g)


def _ffn_kernel(x_ref, gpre_ref, wg_ref, wu_ref, cwg_ref, cwu_ref, cbg_ref, cbu_ref, wd_ref,
                gpost_ref, o_ref, h_ref, carry_ref, yg_ref, yu_ref, *, tiles_per_seq):
    i = pl.program_id(0)
    j = pl.program_id(1)
    tm = x_ref.shape[0]
    halo = SUBLANES

    @pl.when(j == 0)
    def _():
        h_ref[...] = _rms(x_ref[...], gpre_ref[...]).astype(BF16)
        o_ref[...] = jnp.zeros(o_ref.shape, F32)

    @pl.when(lax.rem(i, tiles_per_seq) == 0)
    def _():
        carry_ref[j] = jnp.zeros(carry_ref.shape[1:], F32)

    def conv(w_ref, cw_ref, cb_ref, slot, y_ref):
        u = _dot(h_ref[...], w_ref[...])
        prev = carry_ref[j, slot]
        carry_ref[j, slot] = u[tm - halo:tm]
        w0, w1, w2 = cw_ref[0:1, :], cw_ref[1:2, :], cw_ref[2:3, :]
        bias = cb_ref[...]

        def taps(z):
            return bias + w2 * z + w1 * pltpu.roll(z, 1, axis=0) + w0 * pltpu.roll(z, 2, axis=0)

        y_ref[...] = taps(u)
        y_ref[0:halo, :] = taps(jnp.concatenate([prev, u[0:halo]], axis=0))[halo:2 * halo]

    conv(wg_ref, cwg_ref, cbg_ref, 0, yg_ref)
    conv(wu_ref, cwu_ref, cbu_ref, 1, yu_ref)
    act = (_gelu_tanh(yg_ref[...]) * yu_ref[...]).astype(BF16)
    o_ref[...] += _dot(act, wd_ref[...])

    @pl.when(j == pl.num_programs(1) - 1)
    def _():
        o_ref[...] = x_ref[...] + _rms(o_ref[...], gpost_ref[...])


def _ffn(x1, gpre, w_up, conv_w, conv_b, w_down, gpost, tm, tf, seq):
    m = x1.shape[0]
    nj = D_FF // tf
    return pl.pallas_call(
        functools.partial(_ffn_kernel, tiles_per_seq=seq // tm),
        grid=(m // tm, nj),
        in_specs=[
            pl.BlockSpec((tm, D_MODEL), lambda i, j: (i, 0)),
            pl.BlockSpec((1, D_MODEL), lambda i, j: (0, 0)),
            pl.BlockSpec((D_MODEL, tf), lambda i, j: (0, j)),
            pl.BlockSpec((D_MODEL, tf), lambda i, j: (0, j + nj)),
            pl.BlockSpec((CONV_W, tf), lambda i, j: (0, j)),
            pl.BlockSpec((CONV_W, tf), lambda i, j: (0, j + nj)),
            pl.BlockSpec((1, tf), lambda i, j: (0, j)),
            pl.BlockSpec((1, tf), lambda i, j: (0, j + nj)),
            pl.BlockSpec((tf, D_MODEL), lambda i, j: (j, 0)),
            pl.BlockSpec((1, D_MODEL), lambda i, j: (0, 0)),
        ],
        out_specs=pl.BlockSpec((tm, D_MODEL), lambda i, j: (i, 0)),
        out_shape=jax.ShapeDtypeStruct((m, D_MODEL), F32),
        scratch_shapes=[
            pltpu.VMEM((tm, D_MODEL), BF16),
            pltpu.VMEM((nj, 2, SUBLANES, tf), F32),
            pltpu.VMEM((tm, tf), F32),
            pltpu.VMEM((tm, tf), F32),
        ],
        compiler_params=_cparams(("arbitrary", "arbitrary"), FFN_VMEM_LIMIT),
        name="ffn",
    )(x1, gpre, w_up, w_up, conv_w, conv_w, conv_b, conv_b, w_down, gpost)


def _rel_bucket(delta):
    n = jnp.maximum(delta, 0)
    max_exact = N_BUCKETS // 2
    nf = jnp.maximum(n, 1).astype(F32)
    large = max_exact + (jnp.log(nf / max_exact) / math.log(MAX_DISTANCE / max_exact)
                         * (N_BUCKETS - max_exact)).astype(jnp.int32)
    large = jnp.minimum(large, N_BUCKETS - 1)
    return jnp.where(n < max_exact, n, large)


def _bias_kernel(tab_ref, bkt_ref, o_ref):
    h = pl.program_id(0)
    i = pl.program_id(1)
    bkt = bkt_ref[...]
    row = jnp.full(bkt.shape, NEG, F32)
    for bucket in range(N_BUCKETS):
        row = jnp.where(bkt == bucket, tab_ref[bucket, h], row)
    tr, width = o_ref.shape[1:]
    rows = jnp.concatenate([row] * (tr // SUBLANES), axis=0)
    o_ref[0] = pltpu.roll(rows, i * tr, axis=1, stride=1, stride_axis=0)[:, 0:width]


def _bias_strip(table, rows, width, off, hi):
    heads = table.shape[1]
    tr = LANES
    wp = width + rows
    t = jnp.arange(wp)
    dist = off - jnp.where(t < width, t, t - wp)
    bkt = jnp.where((dist < 0) | (dist >= hi), -1, _rel_bucket(dist)).astype(jnp.int32)
    bkt = jnp.broadcast_to(bkt[None, :], (SUBLANES, wp))
    return pl.pallas_call(
        _bias_kernel,
        grid=(heads, rows // tr),
        in_specs=[
            pl.BlockSpec(memory_space=pltpu.SMEM),
            pl.BlockSpec((SUBLANES, wp), lambda h, i: (0, 0)),
        ],
        out_specs=pl.BlockSpec((1, tr, width), lambda h, i: (h, i, 0)),
        out_shape=jax.ShapeDtypeStruct((heads, rows, width), F32),
        compiler_params=_cparams(("arbitrary", "arbitrary")),
        name="bias_strip",
    )(table, bkt)


def _selection_map(nb, nc_pad):
    r = SLC_LEN // CMP_STRIDE
    j = jnp.arange(nb)[:, None]
    c = jnp.arange(nc_pad)[None, :]
    off = r * j - c
    out = jnp.zeros((nb, nc_pad), F32)
    for mm in range(r):
        for nn in range(CMP_LEN // CMP_STRIDE):
            out = out + (off == mm + nn).astype(F32)
    return out * (c < nc_pad - 1)


def kernel(x, pre_mix_norm, w_in, lambda_q1, lambda_k1, lambda_q2, lambda_k2, diff_subln,
           cmp_pos_k, cmp_pos_v, cmp_k_w1, cmp_k_w2, cmp_v_w1, cmp_v_w2, rel_bias, w_out,
           post_mix_norm, pre_ffn_norm, w_up, conv_w, conv_b, w_down, post_ffn_norm):
    b, s, d = x.shape
    assert d == D_MODEL and s % DIFF_TQ == 0 and s // SLC_LEN >= SLC_TOPK
    m = b * s
    depth = w_in.shape[0]
    big = 1 << 30
    for l in range(depth):
        xf = x.reshape(m, d)
        w_main = w_in[l].astype(BF16)
        w_gate = jnp.pad(w_in[l][:, MAIN_COLS:].reshape(d, NSA_KV, 3 * NSA_GROUP),
                         ((0, 0), (0, 0), (0, LANES - 3 * NSA_GROUP)))
        w_gate = w_gate.reshape(d, NSA_KV * LANES).astype(BF16)
        proj, gates, ksx, rows16 = _inproj(xf, pre_mix_norm[l][None], w_main, w_gate, tm=1024,
                                           tn=512, seq=s)
        proj3 = proj.reshape(b, s, MAIN_COLS)

        bias_diff = rel_bias[:, :DIFF_HEADS]
        bias_nsa = rel_bias[:, DIFF_HEADS:]
        t_diff = _bias_strip((bias_diff - bias_diff[N_BUCKETS - 1]) * LOG2E,
                             DIFF_TQ, 2 * DIFF_TQ, DIFF_TQ, big)
        t_sel = _bias_strip((bias_nsa - bias_nsa[N_BUCKETS - 1]) * LOG2E,
                            NSA_TQ, TS_W, TS_OFF, big)
        t_win = _bias_strip(bias_nsa * LOG2E, NSA_TQ, TW_W, WINDOW, WINDOW)
        t_sel = t_sel.reshape(NSA_KV, NSA_GROUP * NSA_TQ, TS_W)
        t_win = t_win.reshape(NSA_KV, NSA_GROUP * NSA_TQ, TW_W)

        lam_init = 0.8 - 0.6 * math.exp(-0.3 * l)
        o_diff = _diff_attention(proj3, t_diff, lambda_q1[l][None], lambda_k1[l][None],
                                 lambda_q2[l][None], lambda_k2[l][None], diff_subln[l][None],
                                 lam_init)

        nrow = s // CMP_STRIDE

        r = rows16.reshape(2, NSA_KV, b, nrow, CMP_STRIDE * NSA_DK)
        pos = jnp.stack([cmp_pos_k[l].reshape(1, -1), cmp_pos_v[l].reshape(1, -1)])
        w1 = jnp.stack([cmp_k_w1[l], cmp_v_w1[l]]).astype(BF16)
        w2 = jnp.stack([cmp_k_w2[l], cmp_v_w2[l]]).astype(BF16)
        kvc = _compress(r, pos, w1, w2)

        nb = s // SLC_LEN
        o_nsa = _nsa_attention(proj3, kvc, ksx.reshape(b, s, -1), t_sel, t_win,
                               gates.reshape(b, s, -1), _selection_map(nb, nrow).astype(BF16))

        half = DIFF_HEADS * 2 * DIFF_D
        x1 = _outproj(o_diff.reshape(m, -1), o_nsa.reshape(m, -1), xf,
                      w_out[l][:half].astype(BF16), w_out[l][half:].astype(BF16),
                      post_mix_norm[l][None], tm=512)
        x2 = _ffn(x1, pre_ffn_norm[l][None], w_up[l].astype(BF16), conv_w[l], conv_b[l][None],
                  w_down[l].astype(BF16), post_ffn_norm[l][None], tm=1024, tf=512, seq=s)
        x = x2.reshape(b, s, d)
    return x
```

```python
import functools
import math

import jax
import jax.numpy as jnp
from jax import lax
from jax.experimental import pallas as pl
from jax.experimental.pallas import tpu as pltpu

F32 = jnp.float32
BF16 = jnp.bfloat16

D_MODEL = 2048
DIFF_HEADS = 8
DIFF_D = 64
NSA_HEADS = 8
NSA_KV = 2
NSA_GROUP = NSA_HEADS // NSA_KV
NSA_DK = 128
NSA_DV = 128
CMP_LEN = 32
CMP_STRIDE = 16
CMP_HIDDEN = 256
SLC_LEN = 64
SLC_TOPK = 16
WINDOW = 512
N_BUCKETS = 32
MAX_DISTANCE = 128
D_FF = 5632
CONV_W = 3
EPS = 1e-6
NEG = -1e30
TINY = 1e-30
LOG2E = math.log2(math.e)

LANES = 128
SUBLANES = 8
VMEM_LIMIT = 56 * 1024 * 1024
FFN_VMEM_LIMIT = 61 * 1024 * 1024

MAIN_COLS = 5632
COL_DQ, COL_DK, COL_DV, COL_NQ = 0, 1024, 2048, 3072
COL_KC, COL_VC, COL_KS, COL_VS, COL_KW, COL_VW = 4096, 4352, 4608, 4864, 5120, 5376

INPROJ_TM, INPROJ_TN = 1024, 512
OUTPROJ_TM = 512
FFN_TM, FFN_TF = 1024, 512
DIFF_TQ = 512
NSA_TQ = 256
FLASH_SUB = 512
FAR_WIDE = 2
NSA_TK = 512
NSA_WK = WINDOW + NSA_TQ
TS_OFF = 2 * NSA_TK - NSA_TQ
TS_W = TS_OFF + NSA_TK
TW_W = WINDOW + NSA_WK


def _cparams(sem, vmem_limit=VMEM_LIMIT):
    return pltpu.CompilerParams(dimension_semantics=sem, vmem_limit_bytes=vmem_limit)


def _rms(x, g):
    return x * lax.rsqrt(jnp.mean(x * x, axis=-1, keepdims=True) + EPS) * g


def _gelu_tanh(x):
    return 0.5 * x * (1.0 + jnp.tanh(math.sqrt(2.0 / math.pi) * (x + 0.044715 * (x * x * x))))


def _dot(a, b):
    return jnp.dot(a, b, preferred_element_type=F32)


def _dot_nt(a, b, precision=None):
    return lax.dot_general(a, b, (((1,), (1,)), ((), ())), preferred_element_type=F32,
                           precision=precision)


def _lane_tile(a, n):
    return a if n == 1 else jnp.concatenate([a] * n, axis=1)


def _online_softmax_update(s, v, m_ref, l_ref, acc_ref, bias=None):
    m_prev, l, acc = m_ref[...], l_ref[...], acc_ref[...]
    dv = v.shape[1]
    ones = jnp.ones((FLASH_SUB, LANES), BF16)
    for c in range(0, s.shape[1], FLASH_SUB):
        sc = s[:, c:c + FLASH_SUB]
        extra = None if bias is None else bias(c)
        if extra is not None:
            sc = sc + extra
        m_new = jnp.maximum(m_prev, jnp.max(sc, axis=1, keepdims=True))
        alpha = jnp.exp2(m_prev - m_new)
        p = jnp.exp2((sc - _lane_tile(m_new, FLASH_SUB // LANES)).astype(BF16))
        pv = _dot(p, jnp.concatenate([v[c:c + FLASH_SUB], ones], axis=1))
        l = alpha * l + pv[:, dv:]
        acc = alpha * acc + pv[:, 0:dv]
        m_prev = m_new
    m_ref[...], l_ref[...], acc_ref[...] = m_prev, l, acc


def _inproj_kernel(x_ref, g_ref, w_ref, wg_ref, proj_ref, gates_ref, ksx_ref, r16_ref, h_ref,
                   stage_ref, *, seq):
    i = pl.program_id(0)
    j = pl.program_id(1)
    tm, tn = proj_ref.shape

    @pl.when(j == 0)
    def _():
        h_ref[...] = _rms(x_ref[...], g_ref[...]).astype(BF16)
        gl = _dot(h_ref[...], wg_ref[...])
        gates_ref[...] = 1.0 / (1.0 + jnp.exp(-gl))

    acc = _dot(h_ref[...], w_ref[...])
    proj_ref[...] = acc.astype(BF16)

    @pl.when(j == COL_KS // tn)
    def _():
        nb = seq // SLC_LEN
        xw = NSA_DK + nb
        pos = lax.rem(i * tm + lax.broadcasted_iota(jnp.int32, (tm, nb), 0), seq)
        blk = lax.shift_right_logical(pos, int(math.log2(SLC_LEN)))
        onehot = jnp.where(blk == lax.broadcasted_iota(jnp.int32, (tm, nb), 1), 1.0, 0.0)
        for g in range(NSA_KV):
            ksx_ref[:, g * xw:g * xw + NSA_DK] = acc[:, g * NSA_DK:(g + 1) * NSA_DK].astype(BF16)
            ksx_ref[:, g * xw + NSA_DK:(g + 1) * xw] = onehot.astype(BF16)

    @pl.when(j == COL_KC // tn)
    def _():
        nr = tm // CMP_STRIDE
        for t in range(2):
            for g in range(NSA_KV):
                col = (t * NSA_KV + g) * NSA_DK
                stage_ref[...] = acc[:, col:col + NSA_DK]
                for p in range(CMP_STRIDE):
                    r16_ref[t, g, :, p * NSA_DK:(p + 1) * NSA_DK] = (
                        stage_ref[pl.ds(p, nr, stride=CMP_STRIDE), :].astype(BF16))


def _inproj(xf, g, w_main, w_gate, tm, tn, seq):
    m = xf.shape[0]
    assert COL_KS % tn == 0 and COL_KC % tn == 0 and tn == 2 * NSA_KV * NSA_DK
    xw = NSA_DK + seq // SLC_LEN
    ng = w_gate.shape[1]
    return pl.pallas_call(
        functools.partial(_inproj_kernel, seq=seq),
        grid=(m // tm, MAIN_COLS // tn),
        in_specs=[
            pl.BlockSpec((tm, D_MODEL), lambda i, j: (i, 0)),
            pl.BlockSpec((1, D_MODEL), lambda i, j: (0, 0)),
            pl.BlockSpec((D_MODEL, tn), lambda i, j: (0, j)),
            pl.BlockSpec((D_MODEL, ng), lambda i, j: (0, 0)),
        ],
        out_specs=[
            pl.BlockSpec((tm, tn), lambda i, j: (i, j)),
            pl.BlockSpec((tm, ng), lambda i, j: (i, 0)),
            pl.BlockSpec((tm, NSA_KV * xw), lambda i, j: (i, 0)),
            pl.BlockSpec((2, NSA_KV, tm // CMP_STRIDE, CMP_STRIDE * NSA_DK),
                         lambda i, j: (0, 0, i, 0)),
        ],
        out_shape=[
            jax.ShapeDtypeStruct((m, MAIN_COLS), BF16),
            jax.ShapeDtypeStruct((m, ng), F32),
            jax.ShapeDtypeStruct((m, NSA_KV * xw), BF16),
            jax.ShapeDtypeStruct((2, NSA_KV, m // CMP_STRIDE, CMP_STRIDE * NSA_DK), BF16),
        ],
        scratch_shapes=[pltpu.VMEM((tm, D_MODEL), BF16), pltpu.VMEM((tm, NSA_DK), F32)],
        compiler_params=_cparams(("arbitrary", "arbitrary")),
        name="inproj",
    )(xf, g, w_main, w_gate)


def _diff_kernel(q_ref, k_ref, v_ref, t_ref, lq1_ref, lk1_ref, lq2_ref, lk2_ref, sub_ref,
                 o_ref, qs_ref, m_ref, l_ref, acc_ref, *, lam_init):
    tq = DIFF_TQ
    lam = (jnp.exp(jnp.sum(lq1_ref[...] * lk1_ref[...], axis=1, keepdims=True))
           - jnp.exp(jnp.sum(lq2_ref[...] * lk2_ref[...], axis=1, keepdims=True)) + lam_init)

    def query_tile(i, carry):
        _diff_query_tile(i, lam, q_ref, k_ref, v_ref, t_ref, sub_ref, o_ref, qs_ref, m_ref, l_ref,
                         acc_ref, lam_init)
        return carry

    lax.fori_loop(0, q_ref.shape[1] // tq, query_tile, 0)


def _diff_query_tile(i, lam, q_ref, k_ref, v_ref, t_ref, sub_ref, o_ref, qs_ref, m_ref, l_ref,
                     acc_ref, lam_init):
    tq = DIFF_TQ
    rows = pl.ds(pl.multiple_of(i * tq, tq), tq)
    q = q_ref[0, rows, :].astype(F32) * (DIFF_D ** -0.5 * LOG2E)
    lane = lax.broadcasted_iota(jnp.int32, q.shape, 1)
    qs_ref[...] = jnp.concatenate(
        [jnp.where(lane < DIFF_D, q, 0.0), jnp.where(lane >= DIFF_D, q, 0.0)], axis=0).astype(BF16)
    m_ref[...] = jnp.full(m_ref.shape, NEG, F32)
    l_ref[...] = jnp.zeros(l_ref.shape, F32)
    acc_ref[...] = jnp.zeros(acc_ref.shape, F32)

    def step(kstart, width, t_col):
        def bias(c):
            if t_col is None or t_col + c < 0:
                return None
            t = t_ref[0, :, t_col + c:t_col + c + FLASH_SUB]
            return jnp.concatenate([t, t], axis=0)

        k = k_ref[0, pl.ds(kstart, width), :]
        s = _dot_nt(qs_ref[...], k)
        _online_softmax_update(s, v_ref[0, pl.ds(kstart, width), :], m_ref, l_ref, acc_ref, bias)

    nfar = jnp.maximum(i - 1, 0)
    nwide = nfar // FAR_WIDE

    def far_body(j, carry):
        step(pl.multiple_of(j * (FAR_WIDE * tq), FAR_WIDE * tq), FAR_WIDE * tq, None)
        return carry

    lax.fori_loop(0, nwide, far_body, 0)
    left = nfar - nwide * FAR_WIDE
    odd = lax.rem(left, 2) == 1

    if FAR_WIDE > 2:
        @pl.when(left >= 2)
        def _():
            step(pl.multiple_of(nwide * (FAR_WIDE * tq), 2 * tq), 2 * tq, None)

    @pl.when(odd)
    def _():
        step(pl.multiple_of((i - 2) * tq, tq), 3 * tq, -tq)

    @pl.when(jnp.logical_and(i > 0, jnp.logical_not(odd)))
    def _():
        step(pl.multiple_of((i - 1) * tq, tq), 2 * tq, 0)

    @pl.when(i == 0)
    def _():
        step(0, tq, tq)

    o = acc_ref[...] / jnp.maximum(l_ref[...], TINY)
    a = o[0:tq] - lam * o[tq:2 * tq]
    o_ref[0, rows, :] = (_rms(a, sub_ref[...]) * (1.0 - lam_init)).astype(BF16)


def _diff_attention(proj3, t_diff, lq1, lk1, lq2, lk2, subln, lam_init):
    b, s, _ = proj3.shape
    tq = DIFF_TQ
    vec = lambda n: pl.BlockSpec((1, n), lambda bi, h: (0, 0))
    return pl.pallas_call(
        functools.partial(_diff_kernel, lam_init=lam_init),
        grid=(b, DIFF_HEADS),
        in_specs=[
            pl.BlockSpec((1, s, LANES), lambda bi, h: (bi, 0, COL_DQ // LANES + h)),
            pl.BlockSpec((1, s, LANES), lambda bi, h: (bi, 0, COL_DK // LANES + h)),
            pl.BlockSpec((1, s, LANES), lambda bi, h: (bi, 0, COL_DV // LANES + h)),
            pl.BlockSpec((1, tq, 2 * tq), lambda bi, h: (h, 0, 0)),
            vec(DIFF_D), vec(DIFF_D), vec(DIFF_D), vec(DIFF_D), vec(2 * DIFF_D),
        ],
        out_specs=pl.BlockSpec((1, s, LANES), lambda bi, h: (bi, 0, h)),
        out_shape=jax.ShapeDtypeStruct((b, s, DIFF_HEADS * 2 * DIFF_D), BF16),
        scratch_shapes=[
            pltpu.VMEM((2 * tq, LANES), BF16),
            pltpu.VMEM((2 * tq, LANES), F32),
            pltpu.VMEM((2 * tq, LANES), F32),
            pltpu.VMEM((2 * tq, LANES), F32),
        ],
        compiler_params=_cparams(("arbitrary", "arbitrary")),
        name="diff_attn",
    )(proj3, proj3, proj3, t_diff, lq1, lk1, lq2, lk2, subln)


def _compress_kernel(r_ref, pos_ref, w1_ref, w2_ref, o_ref):
    half = CMP_STRIDE * NSA_DK
    r = r_ref[0, 0, 0]
    n = r.shape[0]
    a = _dot(r, w1_ref[0, 0:half, :])
    bm = _dot(r, w1_ref[0, half:2 * half, :])
    posb = jnp.broadcast_to(pos_ref[0], (SUBLANES, 2 * half)).astype(BF16)
    pt = _dot(posb, w1_ref[0])[0:1]
    hid = a + pltpu.roll(bm, n - 1, axis=0) + pt
    o_ref[0, 0, 0] = _dot(_gelu_tanh(hid).astype(BF16), w2_ref[0]).astype(BF16)


def _compress(r, pos, w1, w2):
    _, g, b, n, width = r.shape
    return pl.pallas_call(
        _compress_kernel,
        grid=(2, b, g),
        in_specs=[
            pl.BlockSpec((1, 1, 1, n, width), lambda t, bi, gi: (t, gi, bi, 0, 0)),
            pl.BlockSpec((1, 1, CMP_LEN * NSA_DK), lambda t, bi, gi: (t, 0, 0)),
            pl.BlockSpec((1, CMP_LEN * NSA_DK, CMP_HIDDEN), lambda t, bi, gi: (t, 0, 0)),
            pl.BlockSpec((1, CMP_HIDDEN, NSA_DK), lambda t, bi, gi: (t, 0, 0)),
        ],
        out_specs=pl.BlockSpec((1, 1, 1, n, NSA_DK), lambda t, bi, gi: (t, bi, gi, 0, 0)),
        out_shape=jax.ShapeDtypeStruct((2, b, g, n, NSA_DK), BF16),
        compiler_params=_cparams(("arbitrary", "arbitrary", "arbitrary")),
        name="nsa_compress",
    )(r, pos, w1, w2)


def _nsa_kernel(q_ref, kc_ref, vc_ref, ks_ref, vs_ref, kw_ref, vw_ref, ts_ref, tw_ref, gate_ref,
                selmap_ref, o_ref, qx_ref, m_ref, l_ref, acc_ref, part_ref):
    tq, tk, hg = NSA_TQ, NSA_TK, NSA_GROUP
    rows = hg * tq
    i = pl.program_id(2)
    q0 = i * tq

    def gcol(c):
        return jnp.concatenate([gate_ref[0, :, 3 * h + c:3 * h + c + 1] for h in range(hg)], axis=0)

    def before_loop(nc):
        qt = q_ref[0]
        q4 = jnp.concatenate([qt[:, h * NSA_DK:(h + 1) * NSA_DK] for h in range(hg)], axis=0)
        q4 = (q4.astype(F32) * (NSA_DK ** -0.5 * LOG2E)).astype(BF16)

        kc = kc_ref[0, 0, 0, 0:nc, :]
        cend = lax.broadcasted_iota(jnp.int32, (tq, nc), 1) * CMP_STRIDE + (CMP_LEN - 1)
        hidden = jnp.where(cend <= q0 + lax.broadcasted_iota(jnp.int32, (tq, nc), 0), 0.0, NEG)
        sc = _dot_nt(q4, kc) + jnp.concatenate([hidden] * hg, axis=0)
        mc = jnp.max(sc, axis=1, keepdims=True)
        pc = jnp.exp2(sc - mc)
        norm = jnp.where(mc > 0.5 * NEG,
                         1.0 / jnp.maximum(jnp.sum(pc, axis=1, keepdims=True), TINY), 0.0)
        pc = pc * norm
        o_cmp = _dot(pc.astype(BF16), vc_ref[0, 0, 0, 0:nc, :])

        pcsum = pc[0:tq] + pc[tq:2 * tq] + pc[2 * tq:3 * tq] + pc[3 * tq:4 * tq]
        p_hi = pcsum.astype(BF16)
        p_res = pcsum - p_hi.astype(F32)
        p_mid = p_res.astype(BF16)
        p_lo = (p_res - p_mid.astype(F32)).astype(BF16)
        selmap = selmap_ref[:, 0:nc]
        imp_t = _dot_nt(selmap, p_hi) + _dot_nt(selmap, p_mid) + _dot_nt(selmap, p_lo)
        nb = imp_t.shape[0]
        blk = lax.broadcasted_iota(jnp.int32, (nb, tq), 0)
        cur = lax.shift_right_logical(q0 + lax.broadcasted_iota(jnp.int32, (nb, tq), 1),
                                      int(math.log2(SLC_LEN)))
        forced = (blk == 0) | (blk == cur) | (blk == cur - 1)
        excluded = -3.0e38
        val = jnp.where(forced | (blk > cur), excluded, imp_t)
        for _ in range(min(SLC_TOPK, nb) - 3):
            best = jnp.max(val, axis=0, keepdims=True)
            first = jnp.min(jnp.where(val == best, blk, nb), axis=0, keepdims=True)
            val = jnp.where(blk == first, excluded, val)
        maskbias = jnp.where(val.T < 0.5 * excluded, 0.0, NEG).astype(BF16)
        qx_ref[:, 0:NSA_DK] = q4
        qx_ref[:, NSA_DK:NSA_DK + nb] = jnp.concatenate([maskbias] * hg, axis=0)

        kst = pl.multiple_of(jnp.maximum(q0 - WINDOW, 0), tq)
        w0 = pl.multiple_of(WINDOW - (q0 - kst), LANES)
        lw = _dot_nt(q4, kw_ref[0, pl.ds(kst, NSA_WK), :]) + tw_ref[0, :, pl.ds(w0, NSA_WK)]
        pw = jnp.exp2((lw - jnp.max(lw, axis=1, keepdims=True)).astype(BF16))
        vw1 = jnp.concatenate([vw_ref[0, pl.ds(kst, NSA_WK), :], jnp.ones((NSA_WK, LANES), BF16)],
                              axis=1)
        o_win = _dot(pw, vw1)
        o_win = o_win[:, 0:NSA_DV] / jnp.maximum(o_win[:, NSA_DV:], TINY)

        part_ref[...] = gcol(0) * o_cmp + gcol(2) * o_win

    nc_all = kc_ref.shape[3]
    early = q0 + tq <= (nc_all // 2) * CMP_STRIDE

    @pl.when(early)
    def _():
        before_loop(nc_all // 2)

    @pl.when(jnp.logical_not(early))
    def _():
        before_loop(nc_all)

    m_ref[...] = jnp.full(m_ref.shape, NEG, F32)
    l_ref[...] = jnp.zeros(l_ref.shape, F32)
    acc_ref[...] = jnp.zeros(acc_ref.shape, F32)

    def step(j, width, first_near):
        kstart = pl.multiple_of(j * tk, tk)

        def bias(c):
            if first_near is None or c < first_near * tk:
                return None
            u0 = pl.multiple_of(TS_OFF - (q0 - j * tk) + c, LANES)
            return ts_ref[0, :, pl.ds(u0, FLASH_SUB)]

        s = _dot_nt(qx_ref[...], ks_ref[0, pl.ds(kstart, width), :])
        _online_softmax_update(s, vs_ref[0, pl.ds(kstart, width), :], m_ref, l_ref, acc_ref, bias)

    jl = lax.shift_right_logical(q0 + tq - 1, int(math.log2(tk)))
    nfar = jnp.maximum(jl - 1, 0)
    nwide = nfar // FAR_WIDE

    def far_body(j, carry):
        step(FAR_WIDE * j, FAR_WIDE * tk, None)
        return carry

    lax.fori_loop(0, nwide, far_body, 0)
    left = nfar - nwide * FAR_WIDE
    odd = lax.rem(left, 2) == 1

    if FAR_WIDE > 2:
        @pl.when(left >= 2)
        def _():
            step(nwide * FAR_WIDE, 2 * tk, None)

    @pl.when(odd)
    def _():
        step(jl - 2, 3 * tk, 1)

    @pl.when(jnp.logical_and(jl > 0, jnp.logical_not(odd)))
    def _():
        step(jl - 1, 2 * tk, 0)

    @pl.when(jl == 0)
    def _():
        step(0, tk, 0)
    o = part_ref[...] + gcol(1) * (acc_ref[...] / jnp.maximum(l_ref[...], TINY))
    for h in range(hg):
        o_ref[0, :, h * NSA_DV:(h + 1) * NSA_DV] = o[h * tq:(h + 1) * tq].astype(BF16)


def _nsa_attention(proj3, kvc, ksx3, t_sel, t_win, gates3, selmap):
    b, s, _ = proj3.shape
    tq, hg = NSA_TQ, NSA_GROUP
    nc = kvc.shape[3]
    nb = s // SLC_LEN
    qcols = hg * NSA_DK
    return pl.pallas_call(
        _nsa_kernel,
        grid=(b, NSA_KV, s // tq),
        in_specs=[
            pl.BlockSpec((1, tq, qcols), lambda bi, g, i: (bi, i, COL_NQ // qcols + g)),
            pl.BlockSpec((1, 1, 1, nc, NSA_DK), lambda bi, g, i: (0, bi, g, 0, 0)),
            pl.BlockSpec((1, 1, 1, nc, NSA_DV), lambda bi, g, i: (1, bi, g, 0, 0)),
            pl.BlockSpec((1, s, NSA_DK + nb), lambda bi, g, i: (bi, 0, g)),
            pl.BlockSpec((1, s, NSA_DV), lambda bi, g, i: (bi, 0, COL_VS // NSA_DV + g)),
            pl.BlockSpec((1, s, NSA_DK), lambda bi, g, i: (bi, 0, COL_KW // NSA_DK + g)),
            pl.BlockSpec((1, s, NSA_DV), lambda bi, g, i: (bi, 0, COL_VW // NSA_DV + g)),
            pl.BlockSpec((1, hg * tq, TS_W), lambda bi, g, i: (g, 0, 0),
                         pipeline_mode=pl.Buffered(1)),
            pl.BlockSpec((1, hg * tq, TW_W), lambda bi, g, i: (g, 0, 0),
                         pipeline_mode=pl.Buffered(1)),
            pl.BlockSpec((1, tq, LANES), lambda bi, g, i: (bi, i, g)),
            pl.BlockSpec((nb, nc), lambda bi, g, i: (0, 0)),
        ],
        out_specs=pl.BlockSpec((1, tq, hg * NSA_DV), lambda bi, g, i: (bi, i, g)),
        out_shape=jax.ShapeDtypeStruct((b, s, NSA_HEADS * NSA_DV), BF16),
        scratch_shapes=[
            pltpu.VMEM((hg * tq, NSA_DK + nb), BF16),
            pltpu.VMEM((hg * tq, LANES), F32),
            pltpu.VMEM((hg * tq, LANES), F32),
            pltpu.VMEM((hg * tq, NSA_DV), F32),
            pltpu.VMEM((hg * tq, NSA_DV), F32),
        ],
        compiler_params=_cparams(("arbitrary", "arbitrary", "arbitrary")),
        name="nsa_attn",
    )(proj3, kvc, kvc, ksx3, proj3, proj3, proj3, t_sel, t_win, gates3, selmap)


def _outproj_kernel(od_ref, on_ref, x_ref, wd_ref, wn_ref, g_ref, o_ref):
    y = _dot(od_ref[...], wd_ref[...]) + _dot(on_ref[...], wn_ref[...])
    o_ref[...] = x_ref[...] + _rms(y, g_ref[...])


def _outproj(od, on, xf, w_d, w_n, g, tm):
    m = xf.shape[0]
    kd, kn = od.shape[1], on.shape[1]
    return pl.pallas_call(
        _outproj_kernel,
        grid=(m // tm,),
        in_specs=[
            pl.BlockSpec((tm, kd), lambda i: (i, 0)),
            pl.BlockSpec((tm, kn), lambda i: (i, 0)),
            pl.BlockSpec((tm, D_MODEL), lambda i: (i, 0)),
            pl.BlockSpec((kd, D_MODEL), lambda i: (0, 0)),
            pl.BlockSpec((kn, D_MODEL), lambda i: (0, 0)),
            pl.BlockSpec((1, D_MODEL), lambda i: (0, 0)),
        ],
        out_specs=pl.BlockSpec((tm, D_MODEL), lambda i: (i, 0)),
        out_shape=jax.ShapeDtypeStruct((m, D_MODEL), F32),
        compiler_params=_cparams(("arbitrary",)),
        name="outproj",
    )(od, on, xf, w_d, w_n, g)


def _ffn_kernel(x_ref, gpre_ref, wg_ref, wu_ref, cwg_ref, cwu_ref, cbg_ref, cbu_ref, wd_ref,
                gpost_ref, o_ref, h_ref, carry_ref, yg_ref, yu_ref, *, tiles_per_seq):
    i = pl.program_id(0)
    j = pl.program_id(1)
    tm = x_ref.shape[0]
    halo = SUBLANES

    @pl.when(j == 0)
    def _():
        h_ref[...] = _rms(x_ref[...], gpre_ref[...]).astype(BF16)
        o_ref[...] = jnp.zeros(o_ref.shape, F32)

    @pl.when(lax.rem(i, tiles_per_seq) == 0)
    def _():
        carry_ref[j] = jnp.zeros(carry_ref.shape[1:], F32)

    def conv(w_ref, cw_ref, cb_ref, slot, y_ref):
        u = _dot(h_ref[...], w_ref[...])
        prev = carry_ref[j, slot]
        carry_ref[j, slot] = u[tm - halo:tm]
        w0, w1, w2 = cw_ref[0:1, :], cw_ref[1:2, :], cw_ref[2:3, :]
        bias = cb_ref[...]

        def taps(z):
            return bias + w2 * z + w1 * pltpu.roll(z, 1, axis=0) + w0 * pltpu.roll(z, 2, axis=0)

        y_ref[...] = taps(u)
        y_ref[0:halo, :] = taps(jnp.concatenate([prev, u[0:halo]], axis=0))[halo:2 * halo]

    conv(wg_ref, cwg_ref, cbg_ref, 0, yg_ref)
    conv(wu_ref, cwu_ref, cbu_ref, 1, yu_ref)
    act = (_gelu_tanh(yg_ref[...]) * yu_ref[...]).astype(BF16)
    o_ref[...] += _dot(act, wd_ref[...])

    @pl.when(j == pl.num_programs(1) - 1)
    def _():
        o_ref[...] = x_ref[...] + _rms(o_ref[...], gpost_ref[...])


def _ffn(x1, gpre, w_up, conv_w, conv_b, w_down, gpost, tm, tf, seq):
    m = x1.shape[0]
    nj = D_FF // tf
    return pl.pallas_call(
        functools.partial(_ffn_kernel, tiles_per_seq=seq // tm),
        grid=(m // tm, nj),
        in_specs=[
            pl.BlockSpec((tm, D_MODEL), lambda i, j: (i, 0)),
            pl.BlockSpec((1, D_MODEL), lambda i, j: (0, 0)),
            pl.BlockSpec((D_MODEL, tf), lambda i, j: (0, j)),
            pl.BlockSpec((D_MODEL, tf), lambda i, j: (0, j + nj)),
            pl.BlockSpec((CONV_W, tf), lambda i, j: (0, j)),
            pl.BlockSpec((CONV_W, tf), lambda i, j: (0, j + nj)),
            pl.BlockSpec((1, tf), lambda i, j: (0, j)),
            pl.BlockSpec((1, tf), lambda i, j: (0, j + nj)),
            pl.BlockSpec((tf, D_MODEL), lambda i, j: (j, 0)),
            pl.BlockSpec((1, D_MODEL), lambda i, j: (0, 0)),
        ],
        out_specs=pl.BlockSpec((tm, D_MODEL), lambda i, j: (i, 0)),
        out_shape=jax.ShapeDtypeStruct((m, D_MODEL), F32),
        scratch_shapes=[
            pltpu.VMEM((tm, D_MODEL), BF16),
            pltpu.VMEM((nj, 2, SUBLANES, tf), F32),
            pltpu.VMEM((tm, tf), F32),
            pltpu.VMEM((tm, tf), F32),
        ],
        compiler_params=_cparams(("arbitrary", "arbitrary"), FFN_VMEM_LIMIT),
        name="ffn",
    )(x1, gpre, w_up, w_up, conv_w, conv_w, conv_b, conv_b, w_down, gpost)


def _rel_bucket(delta):
    n = jnp.maximum(delta, 0)
    max_exact = N_BUCKETS // 2
    nf = jnp.maximum(n, 1).astype(F32)
    large = max_exact + (jnp.log(nf / max_exact) / math.log(MAX_DISTANCE / max_exact)
                         * (N_BUCKETS - max_exact)).astype(jnp.int32)
    large = jnp.minimum(large, N_BUCKETS - 1)
    return jnp.where(n < max_exact, n, large)


def _bias_kernel(tab_ref, bkt_ref, o_ref):
    h = pl.program_id(0)
    i = pl.program_id(1)
    bkt = bkt_ref[...]
    row = jnp.full(bkt.shape, NEG, F32)
    for bucket in range(N_BUCKETS):
        row = jnp.where(bkt == bucket, tab_ref[bucket, h], row)
    tr, width = o_ref.shape[1:]
    rows = jnp.concatenate([row] * (tr // SUBLANES), axis=0)
    o_ref[0] = pltpu.roll(rows, i * tr, axis=1, stride=1, stride_axis=0)[:, 0:width]


def _bias_strip(table, rows, width, off, hi):
    heads = table.shape[1]
    tr = LANES
    wp = width + rows
    t = jnp.arange(wp)
    dist = off - jnp.where(t < width, t, t - wp)
    bkt = jnp.where((dist < 0) | (dist >= hi), -1, _rel_bucket(dist)).astype(jnp.int32)
    bkt = jnp.broadcast_to(bkt[None, :], (SUBLANES, wp))
    return pl.pallas_call(
        _bias_kernel,
        grid=(heads, rows // tr),
        in_specs=[
            pl.BlockSpec(memory_space=pltpu.SMEM),
            pl.BlockSpec((SUBLANES, wp), lambda h, i: (0, 0)),
        ],
        out_specs=pl.BlockSpec((1, tr, width), lambda h, i: (h, i, 0)),
        out_shape=jax.ShapeDtypeStruct((heads, rows, width), F32),
        compiler_params=_cparams(("arbitrary", "arbitrary")),
        name="bias_strip",
    )(table, bkt)


def _selection_map(nb, nc_pad):
    r = SLC_LEN // CMP_STRIDE
    j = jnp.arange(nb)[:, None]
    c = jnp.arange(nc_pad)[None, :]
    off = r * j - c
    out = jnp.zeros((nb, nc_pad), F32)
    for mm in range(r):
        for nn in range(CMP_LEN // CMP_STRIDE):
            out = out + (off == mm + nn).astype(F32)
    return out * (c < nc_pad - 1)


def kernel(x, pre_mix_norm, w_in, lambda_q1, lambda_k1, lambda_q2, lambda_k2, diff_subln,
           cmp_pos_k, cmp_pos_v, cmp_k_w1, cmp_k_w2, cmp_v_w1, cmp_v_w2, rel_bias, w_out,
           post_mix_norm, pre_ffn_norm, w_up, conv_w, conv_b, w_down, post_ffn_norm):
    b, s, d = x.shape
    assert d == D_MODEL and s // SLC_LEN >= SLC_TOPK
    assert s % max(DIFF_TQ, NSA_TQ, NSA_TK, INPROJ_TM, OUTPROJ_TM, FFN_TM) == 0
    m = b * s
    depth = w_in.shape[0]
    unbounded = 1 << 30
    for l in range(depth):
        xf = x.reshape(m, d)
        w_main = w_in[l].astype(BF16)
        w_gate = jnp.pad(w_in[l][:, MAIN_COLS:].reshape(d, NSA_KV, 3 * NSA_GROUP),
                         ((0, 0), (0, 0), (0, LANES - 3 * NSA_GROUP)))
        w_gate = w_gate.reshape(d, NSA_KV * LANES).astype(BF16)
        proj, gates, ksx, rows16 = _inproj(xf, pre_mix_norm[l][None], w_main, w_gate,
                                           tm=INPROJ_TM, tn=INPROJ_TN, seq=s)
        proj3 = proj.reshape(b, s, MAIN_COLS)

        bias_diff = rel_bias[:, :DIFF_HEADS]
        bias_nsa = rel_bias[:, DIFF_HEADS:]
        t_diff = _bias_strip((bias_diff - bias_diff[N_BUCKETS - 1]) * LOG2E,
                             DIFF_TQ, 2 * DIFF_TQ, DIFF_TQ, unbounded)
        t_sel = _bias_strip((bias_nsa - bias_nsa[N_BUCKETS - 1]) * LOG2E,
                            NSA_TQ, TS_W, TS_OFF, unbounded)
        t_win = _bias_strip(bias_nsa * LOG2E, NSA_TQ, TW_W, WINDOW, WINDOW)
        t_sel = t_sel.reshape(NSA_KV, NSA_GROUP * NSA_TQ, TS_W)
        t_win = t_win.reshape(NSA_KV, NSA_GROUP * NSA_TQ, TW_W)

        lam_init = 0.8 - 0.6 * math.exp(-0.3 * l)
        o_diff = _diff_attention(proj3, t_diff, lambda_q1[l][None], lambda_k1[l][None],
                                 lambda_q2[l][None], lambda_k2[l][None], diff_subln[l][None],
                                 lam_init)

        nrow = s // CMP_STRIDE
        r = rows16.reshape(2, NSA_KV, b, nrow, CMP_STRIDE * NSA_DK)
        pos = jnp.stack([cmp_pos_k[l].reshape(1, -1), cmp_pos_v[l].reshape(1, -1)])
        w1 = jnp.stack([cmp_k_w1[l], cmp_v_w1[l]]).astype(BF16)
        w2 = jnp.stack([cmp_k_w2[l], cmp_v_w2[l]]).astype(BF16)
        kvc = _compress(r, pos, w1, w2)

        nb = s // SLC_LEN
        o_nsa = _nsa_attention(proj3, kvc, ksx.reshape(b, s, -1), t_sel, t_win,
                               gates.reshape(b, s, -1), _selection_map(nb, nrow).astype(BF16))

        half = DIFF_HEADS * 2 * DIFF_D
        x1 = _outproj(o_diff.reshape(m, -1), o_nsa.reshape(m, -1), xf,
                      w_out[l][:half].astype(BF16), w_out[l][half:].astype(BF16),
                      post_mix_norm[l][None], tm=OUTPROJ_TM)
        x2 = _ffn(x1, pre_ffn_norm[l][None], w_up[l].astype(BF16), conv_w[l], conv_b[l][None],
                  w_down[l].astype(BF16), post_ffn_norm[l][None], tm=FFN_TM, tf=FFN_TF, seq=s)
        x = x2.reshape(b, s, d)
    return x
```

```python
import functools
import math

import jax
import jax.numpy as jnp
from jax import lax
from jax.experimental import pallas as pl
from jax.experimental.pallas import tpu as pltpu

F32 = jnp.float32
BF16 = jnp.bfloat16

D_MODEL = 2048
DIFF_HEADS = 8
DIFF_D = 64
NSA_HEADS = 8
NSA_KV = 2
NSA_GROUP = NSA_HEADS // NSA_KV
NSA_DK = 128
NSA_DV = 128
CMP_LEN = 32
CMP_STRIDE = 16
CMP_HIDDEN = 256
SLC_LEN = 64
SLC_TOPK = 16
WINDOW = 512
N_BUCKETS = 32
MAX_DISTANCE = 128
D_FF = 5632
CONV_W = 3
EPS = 1e-6
NEG = -1e30
TINY = 1e-30
LOG2E = math.log2(math.e)

LANES = 128
SUBLANES = 8
VMEM_LIMIT = 56 * 1024 * 1024
FFN_VMEM_LIMIT = 61 * 1024 * 1024

MAIN_COLS = 5632
COL_DQ, COL_DK, COL_DV, COL_NQ = 0, 1024, 2048, 3072
COL_KC, COL_VC, COL_KS, COL_VS, COL_KW, COL_VW = 4096, 4352, 4608, 4864, 5120, 5376

INPROJ_TM, INPROJ_TN = 1024, 512
OUTPROJ_TM = 512
FFN_TM, FFN_TF = 1024, 512
DIFF_TQ = 512
NSA_TQ = 256
NSA_QPS = 2
FLASH_SUB = 512
FAR_WIDE = 2
NSA_TK = 512
NSA_WK = WINDOW + NSA_TQ
TS_OFF = 2 * NSA_TK - NSA_TQ
TS_W = TS_OFF + NSA_TK
TW_W = WINDOW + NSA_WK


def _cparams(sem, vmem_limit=VMEM_LIMIT):
    return pltpu.CompilerParams(dimension_semantics=sem, vmem_limit_bytes=vmem_limit)


def _rms(x, g):
    return x * lax.rsqrt(jnp.mean(x * x, axis=-1, keepdims=True) + EPS) * g


def _gelu_tanh(x):
    return 0.5 * x * (1.0 + jnp.tanh(math.sqrt(2.0 / math.pi) * (x + 0.044715 * (x * x * x))))


def _dot(a, b):
    return jnp.dot(a, b, preferred_element_type=F32)


def _dot_nt(a, b, precision=None):
    return lax.dot_general(a, b, (((1,), (1,)), ((), ())), preferred_element_type=F32,
                           precision=precision)


def _lane_tile(a, n):
    return a if n == 1 else jnp.concatenate([a] * n, axis=1)


def _online_softmax_update(s, v, m_ref, l_ref, acc_ref, bias=None):
    m_prev, l, acc = m_ref[...], l_ref[...], acc_ref[...]
    dv = v.shape[1]
    ones = jnp.ones((FLASH_SUB, LANES), BF16)
    for c in range(0, s.shape[1], FLASH_SUB):
        sc = s[:, c:c + FLASH_SUB]
        extra = None if bias is None else bias(c)
        if extra is not None:
            sc = sc + extra
        m_new = jnp.maximum(m_prev, jnp.max(sc, axis=1, keepdims=True))
        alpha = jnp.exp2(m_prev - m_new)
        p = jnp.exp2((sc - _lane_tile(m_new, FLASH_SUB // LANES)).astype(BF16))
        pv = _dot(p, jnp.concatenate([v[c:c + FLASH_SUB], ones], axis=1))
        l = alpha * l + pv[:, dv:]
        acc = alpha * acc + pv[:, 0:dv]
        m_prev = m_new
    m_ref[...], l_ref[...], acc_ref[...] = m_prev, l, acc


def _inproj_kernel(x_ref, g_ref, w_ref, wg_ref, proj_ref, gates_ref, ksx_ref, r16_ref, h_ref,
                   stage_ref, *, seq):
    i = pl.program_id(0)
    j = pl.program_id(1)
    tm, tn = proj_ref.shape

    @pl.when(j == 0)
    def _():
        h_ref[...] = _rms(x_ref[...], g_ref[...]).astype(BF16)
        gl = _dot(h_ref[...], wg_ref[...])
        gates_ref[...] = 1.0 / (1.0 + jnp.exp(-gl))

    acc = _dot(h_ref[...], w_ref[...])
    proj_ref[...] = acc.astype(BF16)

    @pl.when(j == COL_KS // tn)
    def _():
        nb = seq // SLC_LEN
        xw = NSA_DK + nb
        pos = lax.rem(i * tm + lax.broadcasted_iota(jnp.int32, (tm, nb), 0), seq)
        blk = lax.shift_right_logical(pos, int(math.log2(SLC_LEN)))
        onehot = jnp.where(blk == lax.broadcasted_iota(jnp.int32, (tm, nb), 1), 1.0, 0.0)
        for g in range(NSA_KV):
            ksx_ref[:, g * xw:g * xw + NSA_DK] = acc[:, g * NSA_DK:(g + 1) * NSA_DK].astype(BF16)
            ksx_ref[:, g * xw + NSA_DK:(g + 1) * xw] = onehot.astype(BF16)

    @pl.when(j == COL_KC // tn)
    def _():
        nr = tm // CMP_STRIDE
        for t in range(2):
            for g in range(NSA_KV):
                col = (t * NSA_KV + g) * NSA_DK
                stage_ref[...] = acc[:, col:col + NSA_DK]
                for p in range(CMP_STRIDE):
                    r16_ref[t, g, :, p * NSA_DK:(p + 1) * NSA_DK] = (
                        stage_ref[pl.ds(p, nr, stride=CMP_STRIDE), :].astype(BF16))


def _inproj(xf, g, w_main, w_gate, tm, tn, seq):
    m = xf.shape[0]
    assert COL_KS % tn == 0 and COL_KC % tn == 0 and tn == 2 * NSA_KV * NSA_DK
    xw = NSA_DK + seq // SLC_LEN
    ng = w_gate.shape[1]
    return pl.pallas_call(
        functools.partial(_inproj_kernel, seq=seq),
        grid=(m // tm, MAIN_COLS // tn),
        in_specs=[
            pl.BlockSpec((tm, D_MODEL), lambda i, j: (i, 0)),
            pl.BlockSpec((1, D_MODEL), lambda i, j: (0, 0)),
            pl.BlockSpec((D_MODEL, tn), lambda i, j: (0, j)),
            pl.BlockSpec((D_MODEL, ng), lambda i, j: (0, 0)),
        ],
        out_specs=[
            pl.BlockSpec((tm, tn), lambda i, j: (i, j)),
            pl.BlockSpec((tm, ng), lambda i, j: (i, 0)),
            pl.BlockSpec((tm, NSA_KV * xw), lambda i, j: (i, 0)),
            pl.BlockSpec((2, NSA_KV, tm // CMP_STRIDE, CMP_STRIDE * NSA_DK),
                         lambda i, j: (0, 0, i, 0)),
        ],
        out_shape=[
            jax.ShapeDtypeStruct((m, MAIN_COLS), BF16),
            jax.ShapeDtypeStruct((m, ng), F32),
            jax.ShapeDtypeStruct((m, NSA_KV * xw), BF16),
            jax.ShapeDtypeStruct((2, NSA_KV, m // CMP_STRIDE, CMP_STRIDE * NSA_DK), BF16),
        ],
        scratch_shapes=[pltpu.VMEM((tm, D_MODEL), BF16), pltpu.VMEM((tm, NSA_DK), F32)],
        compiler_params=_cparams(("arbitrary", "arbitrary")),
        name="inproj",
    )(xf, g, w_main, w_gate)


def _diff_kernel(q_ref, k_ref, v_ref, t_ref, lq1_ref, lk1_ref, lq2_ref, lk2_ref, sub_ref,
                 o_ref, qs_ref, m_ref, l_ref, acc_ref, *, lam_init):
    tq = DIFF_TQ
    lam = (jnp.exp(jnp.sum(lq1_ref[...] * lk1_ref[...], axis=1, keepdims=True))
           - jnp.exp(jnp.sum(lq2_ref[...] * lk2_ref[...], axis=1, keepdims=True)) + lam_init)

    def query_tile(i, carry):
        _diff_query_tile(i, lam, q_ref, k_ref, v_ref, t_ref, sub_ref, o_ref, qs_ref, m_ref, l_ref,
                         acc_ref, lam_init)
        return carry

    lax.fori_loop(0, q_ref.shape[1] // tq, query_tile, 0)


def _diff_query_tile(i, lam, q_ref, k_ref, v_ref, t_ref, sub_ref, o_ref, qs_ref, m_ref, l_ref,
                     acc_ref, lam_init):
    tq = DIFF_TQ
    rows = pl.ds(pl.multiple_of(i * tq, tq), tq)
    q = q_ref[0, rows, :].astype(F32) * (DIFF_D ** -0.5 * LOG2E)
    lane = lax.broadcasted_iota(jnp.int32, q.shape, 1)
    qs_ref[...] = jnp.concatenate(
        [jnp.where(lane < DIFF_D, q, 0.0), jnp.where(lane >= DIFF_D, q, 0.0)], axis=0).astype(BF16)
    m_ref[...] = jnp.full(m_ref.shape, NEG, F32)
    l_ref[...] = jnp.zeros(l_ref.shape, F32)
    acc_ref[...] = jnp.zeros(acc_ref.shape, F32)

    def step(kstart, width, t_col):
        def bias(c):
            if t_col is None or t_col + c < 0:
                return None
            t = t_ref[0, :, t_col + c:t_col + c + FLASH_SUB]
            return jnp.concatenate([t, t], axis=0)

        k = k_ref[0, pl.ds(kstart, width), :]
        s = _dot_nt(qs_ref[...], k)
        _online_softmax_update(s, v_ref[0, pl.ds(kstart, width), :], m_ref, l_ref, acc_ref, bias)

    nfar = jnp.maximum(i - 1, 0)
    nwide = nfar // FAR_WIDE

    def far_body(j, carry):
        step(pl.multiple_of(j * (FAR_WIDE * tq), FAR_WIDE * tq), FAR_WIDE * tq, None)
        return carry

    lax.fori_loop(0, nwide, far_body, 0)
    left = nfar - nwide * FAR_WIDE
    odd = lax.rem(left, 2) == 1

    if FAR_WIDE > 2:
        @pl.when(left >= 2)
        def _():
            step(pl.multiple_of(nwide * (FAR_WIDE * tq), 2 * tq), 2 * tq, None)

    @pl.when(odd)
    def _():
        step(pl.multiple_of((i - 2) * tq, tq), 3 * tq, -tq)

    @pl.when(jnp.logical_and(i > 0, jnp.logical_not(odd)))
    def _():
        step(pl.multiple_of((i - 1) * tq, tq), 2 * tq, 0)

    @pl.when(i == 0)
    def _():
        step(0, tq, tq)

    o = acc_ref[...] / jnp.maximum(l_ref[...], TINY)
    a = o[0:tq] - lam * o[tq:2 * tq]
    o_ref[0, rows, :] = (_rms(a, sub_ref[...]) * (1.0 - lam_init)).astype(BF16)


def _diff_attention(proj3, t_diff, lq1, lk1, lq2, lk2, subln, lam_init):
    b, s, _ = proj3.shape
    tq = DIFF_TQ
    vec = lambda n: pl.BlockSpec((1, n), lambda bi, h: (0, 0))
    return pl.pallas_call(
        functools.partial(_diff_kernel, lam_init=lam_init),
        grid=(b, DIFF_HEADS),
        in_specs=[
            pl.BlockSpec((1, s, LANES), lambda bi, h: (bi, 0, COL_DQ // LANES + h)),
            pl.BlockSpec((1, s, LANES), lambda bi, h: (bi, 0, COL_DK // LANES + h)),
            pl.BlockSpec((1, s, LANES), lambda bi, h: (bi, 0, COL_DV // LANES + h)),
            pl.BlockSpec((1, tq, 2 * tq), lambda bi, h: (h, 0, 0)),
            vec(DIFF_D), vec(DIFF_D), vec(DIFF_D), vec(DIFF_D), vec(2 * DIFF_D),
        ],
        out_specs=pl.BlockSpec((1, s, LANES), lambda bi, h: (bi, 0, h)),
        out_shape=jax.ShapeDtypeStruct((b, s, DIFF_HEADS * 2 * DIFF_D), BF16),
        scratch_shapes=[
            pltpu.VMEM((2 * tq, LANES), BF16),
            pltpu.VMEM((2 * tq, LANES), F32),
            pltpu.VMEM((2 * tq, LANES), F32),
            pltpu.VMEM((2 * tq, LANES), F32),
        ],
        compiler_params=_cparams(("arbitrary", "arbitrary")),
        name="diff_attn",
    )(proj3, proj3, proj3, t_diff, lq1, lk1, lq2, lk2, subln)


def _compress_kernel(r_ref, pos_ref, w1_ref, w2_ref, o_ref):
    half = CMP_STRIDE * NSA_DK
    r = r_ref[0, 0, 0]
    n = r.shape[0]
    a = _dot(r, w1_ref[0, 0:half, :])
    bm = _dot(r, w1_ref[0, half:2 * half, :])
    posb = jnp.broadcast_to(pos_ref[0], (SUBLANES, 2 * half)).astype(BF16)
    pt = _dot(posb, w1_ref[0])[0:1]
    hid = a + pltpu.roll(bm, n - 1, axis=0) + pt
    o_ref[0, 0, 0] = _dot(_gelu_tanh(hid).astype(BF16), w2_ref[0]).astype(BF16)


def _compress(r, pos, w1, w2):
    _, g, b, n, width = r.shape
    return pl.pallas_call(
        _compress_kernel,
        grid=(2, b, g),
        in_specs=[
            pl.BlockSpec((1, 1, 1, n, width), lambda t, bi, gi: (t, gi, bi, 0, 0)),
            pl.BlockSpec((1, 1, CMP_LEN * NSA_DK), lambda t, bi, gi: (t, 0, 0)),
            pl.BlockSpec((1, CMP_LEN * NSA_DK, CMP_HIDDEN), lambda t, bi, gi: (t, 0, 0)),
            pl.BlockSpec((1, CMP_HIDDEN, NSA_DK), lambda t, bi, gi: (t, 0, 0)),
        ],
        out_specs=pl.BlockSpec((1, 1, 1, n, NSA_DK), lambda t, bi, gi: (t, bi, gi, 0, 0)),
        out_shape=jax.ShapeDtypeStruct((2, b, g, n, NSA_DK), BF16),
        compiler_params=_cparams(("arbitrary", "arbitrary", "arbitrary")),
        name="nsa_compress",
    )(r, pos, w1, w2)


def _nsa_kernel(q_ref, kc_ref, vc_ref, ks_ref, vs_ref, kw_ref, vw_ref, ts_ref, tw_ref, gate_ref,
                selmap_ref, o_ref, qx_ref, m_ref, l_ref, acc_ref, part_ref):
    tq, tk, hg = NSA_TQ, NSA_TK, NSA_GROUP
    rows = hg * tq
    first_tile = pl.program_id(2) * NSA_QPS

    def gcol(c, trows):
        return jnp.concatenate([gate_ref[0, trows, 3 * h + c:3 * h + c + 1] for h in range(hg)],
                               axis=0)

    def before_loop(t, nc):
        q0 = (first_tile + t) * tq
        trows = slice(t * tq, (t + 1) * tq)
        qt = q_ref[0, trows, :]
        q4 = jnp.concatenate([qt[:, h * NSA_DK:(h + 1) * NSA_DK] for h in range(hg)], axis=0)
        q4 = (q4.astype(F32) * (NSA_DK ** -0.5 * LOG2E)).astype(BF16)

        kc = kc_ref[0, 0, 0, 0:nc, :]
        cend = lax.broadcasted_iota(jnp.int32, (tq, nc), 1) * CMP_STRIDE + (CMP_LEN - 1)
        hidden = jnp.where(cend <= q0 + lax.broadcasted_iota(jnp.int32, (tq, nc), 0), 0.0, NEG)
        sc = _dot_nt(q4, kc) + jnp.concatenate([hidden] * hg, axis=0)
        mc = jnp.max(sc, axis=1, keepdims=True)
        pc = jnp.exp2(sc - mc)
        norm = jnp.where(mc > 0.5 * NEG,
                         1.0 / jnp.maximum(jnp.sum(pc, axis=1, keepdims=True), TINY), 0.0)
        pc = pc * norm
        o_cmp = _dot(pc.astype(BF16), vc_ref[0, 0, 0, 0:nc, :])

        pcsum = pc[0:tq] + pc[tq:2 * tq] + pc[2 * tq:3 * tq] + pc[3 * tq:4 * tq]
        p_hi = pcsum.astype(BF16)
        p_res = pcsum - p_hi.astype(F32)
        p_mid = p_res.astype(BF16)
        p_lo = (p_res - p_mid.astype(F32)).astype(BF16)
        selmap = selmap_ref[:, 0:nc]
        imp_t = _dot_nt(selmap, p_hi) + _dot_nt(selmap, p_mid) + _dot_nt(selmap, p_lo)
        nb = imp_t.shape[0]
        blk = lax.broadcasted_iota(jnp.int32, (nb, tq), 0)
        cur = lax.shift_right_logical(q0 + lax.broadcasted_iota(jnp.int32, (nb, tq), 1),
                                      int(math.log2(SLC_LEN)))
        forced = (blk == 0) | (blk == cur) | (blk == cur - 1)
        excluded = -3.0e38
        val = jnp.where(forced | (blk > cur), excluded, imp_t)
        for _ in range(min(SLC_TOPK, nb) - 3):
            best = jnp.max(val, axis=0, keepdims=True)
            first = jnp.min(jnp.where(val == best, blk, nb), axis=0, keepdims=True)
            val = jnp.where(blk == first, excluded, val)
        maskbias = jnp.where(val.T < 0.5 * excluded, 0.0, NEG).astype(BF16)
        qx_ref[t, :, 0:NSA_DK] = q4
        qx_ref[t, :, NSA_DK:NSA_DK + nb] = jnp.concatenate([maskbias] * hg, axis=0)

        kst = pl.multiple_of(jnp.maximum(q0 - WINDOW, 0), tq)
        w0 = pl.multiple_of(WINDOW - (q0 - kst), LANES)
        lw = _dot_nt(q4, kw_ref[0, pl.ds(kst, NSA_WK), :]) + tw_ref[0, :, pl.ds(w0, NSA_WK)]
        pw = jnp.exp2((lw - jnp.max(lw, axis=1, keepdims=True)).astype(BF16))
        vw1 = jnp.concatenate([vw_ref[0, pl.ds(kst, NSA_WK), :], jnp.ones((NSA_WK, LANES), BF16)],
                              axis=1)
        o_win = _dot(pw, vw1)
        o_win = o_win[:, 0:NSA_DV] / jnp.maximum(o_win[:, NSA_DV:], TINY)

        part_ref[t] = gcol(0, trows) * o_cmp + gcol(2, trows) * o_win

    nc_all = kc_ref.shape[3]
    early = (first_tile + NSA_QPS) * tq <= (nc_all // 2) * CMP_STRIDE

    @pl.when(early)
    def _():
        for t in range(NSA_QPS):
            before_loop(t, nc_all // 2)

    @pl.when(jnp.logical_not(early))
    def _():
        for t in range(NSA_QPS):
            before_loop(t, nc_all)

    def flash_tile(t, carry):
        _nsa_flash_tile(t, (first_tile + t) * tq, gcol, ks_ref, vs_ref, ts_ref, o_ref, qx_ref, m_ref,
                        l_ref, acc_ref, part_ref)
        return carry

    lax.fori_loop(0, NSA_QPS, flash_tile, 0)


def _nsa_flash_tile(t, q0, gcol, ks_ref, vs_ref, ts_ref, o_ref, qx_ref, m_ref, l_ref, acc_ref,
                    part_ref):
    tq, tk, hg = NSA_TQ, NSA_TK, NSA_GROUP
    trows = pl.ds(pl.multiple_of(t * tq, tq), tq)
    m_ref[...] = jnp.full(m_ref.shape, NEG, F32)
    l_ref[...] = jnp.zeros(l_ref.shape, F32)
    acc_ref[...] = jnp.zeros(acc_ref.shape, F32)

    def step(j, width, first_near):
        kstart = pl.multiple_of(j * tk, tk)

        def bias(c):
            if first_near is None or c < first_near * tk:
                return None
            u0 = pl.multiple_of(TS_OFF - (q0 - j * tk) + c, LANES)
            return ts_ref[0, :, pl.ds(u0, FLASH_SUB)]

        s = _dot_nt(qx_ref[t], ks_ref[0, pl.ds(kstart, width), :])
        _online_softmax_update(s, vs_ref[0, pl.ds(kstart, width), :], m_ref, l_ref, acc_ref, bias)

    jl = lax.shift_right_logical(q0 + tq - 1, int(math.log2(tk)))
    nfar = jnp.maximum(jl - 1, 0)
    nwide = nfar // FAR_WIDE

    def far_body(j, carry):
        step(FAR_WIDE * j, FAR_WIDE * tk, None)
        return carry

    lax.fori_loop(0, nwide, far_body, 0)
    left = nfar - nwide * FAR_WIDE
    odd = lax.rem(left, 2) == 1

    if FAR_WIDE > 2:
        @pl.when(left >= 2)
        def _():
            step(nwide * FAR_WIDE, 2 * tk, None)

    @pl.when(odd)
    def _():
        step(jl - 2, 3 * tk, 1)

    @pl.when(jnp.logical_and(jl > 0, jnp.logical_not(odd)))
    def _():
        step(jl - 1, 2 * tk, 0)

    @pl.when(jl == 0)
    def _():
        step(0, tk, 0)
    o = part_ref[t] + gcol(1, trows) * (acc_ref[...] / jnp.maximum(l_ref[...], TINY))
    for h in range(hg):
        o_ref[0, trows, h * NSA_DV:(h + 1) * NSA_DV] = o[h * tq:(h + 1) * tq].astype(BF16)


def _nsa_attention(proj3, kvc, ksx3, t_sel, t_win, gates3, selmap):
    b, s, _ = proj3.shape
    tq, hg = NSA_TQ, NSA_GROUP
    nc = kvc.shape[3]
    nb = s // SLC_LEN
    qcols = hg * NSA_DK
    return pl.pallas_call(
        _nsa_kernel,
        grid=(b, NSA_KV, s // (NSA_QPS * tq)),
        in_specs=[
            pl.BlockSpec((1, NSA_QPS * tq, qcols), lambda bi, g, i: (bi, i, COL_NQ // qcols + g)),
            pl.BlockSpec((1, 1, 1, nc, NSA_DK), lambda bi, g, i: (0, bi, g, 0, 0)),
            pl.BlockSpec((1, 1, 1, nc, NSA_DV), lambda bi, g, i: (1, bi, g, 0, 0)),
            pl.BlockSpec((1, s, NSA_DK + nb), lambda bi, g, i: (bi, 0, g)),
            pl.BlockSpec((1, s, NSA_DV), lambda bi, g, i: (bi, 0, COL_VS // NSA_DV + g)),
            pl.BlockSpec((1, s, NSA_DK), lambda bi, g, i: (bi, 0, COL_KW // NSA_DK + g)),
            pl.BlockSpec((1, s, NSA_DV), lambda bi, g, i: (bi, 0, COL_VW // NSA_DV + g)),
            pl.BlockSpec((1, hg * tq, TS_W), lambda bi, g, i: (g, 0, 0),
                         pipeline_mode=pl.Buffered(1)),
            pl.BlockSpec((1, hg * tq, TW_W), lambda bi, g, i: (g, 0, 0),
                         pipeline_mode=pl.Buffered(1)),
            pl.BlockSpec((1, NSA_QPS * tq, LANES), lambda bi, g, i: (bi, i, g)),
            pl.BlockSpec((nb, nc), lambda bi, g, i: (0, 0)),
        ],
        out_specs=pl.BlockSpec((1, NSA_QPS * tq, hg * NSA_DV), lambda bi, g, i: (bi, i, g)),
        out_shape=jax.ShapeDtypeStruct((b, s, NSA_HEADS * NSA_DV), BF16),
        scratch_shapes=[
            pltpu.VMEM((NSA_QPS, hg * tq, NSA_DK + nb), BF16),
            pltpu.VMEM((hg * tq, LANES), F32),
            pltpu.VMEM((hg * tq, LANES), F32),
            pltpu.VMEM((hg * tq, NSA_DV), F32),
            pltpu.VMEM((NSA_QPS, hg * tq, NSA_DV), F32),
        ],
        compiler_params=_cparams(("arbitrary", "arbitrary", "arbitrary")),
        name="nsa_attn",
    )(proj3, kvc, kvc, ksx3, proj3, proj3, proj3, t_sel, t_win, gates3, selmap)


def _outproj_kernel(od_ref, on_ref, x_ref, wd_ref, wn_ref, g_ref, o_ref):
    y = _dot(od_ref[...], wd_ref[...]) + _dot(on_ref[...], wn_ref[...])
    o_ref[...] = x_ref[...] + _rms(y, g_ref[...])


def _outproj(od, on, xf, w_d, w_n, g, tm):
    m = xf.shape[0]
    kd, kn = od.shape[1], on.shape[1]
    return pl.pallas_call(
        _outproj_kernel,
        grid=(m // tm,),
        in_specs=[
            pl.BlockSpec((tm, kd), lambda i: (i, 0)),
            pl.BlockSpec((tm, kn), lambda i: (i, 0)),
            pl.BlockSpec((tm, D_MODEL), lambda i: (i, 0)),
            pl.BlockSpec((kd, D_MODEL), lambda i: (0, 0)),
            pl.BlockSpec((kn, D_MODEL), lambda i: (0, 0)),
            pl.BlockSpec((1, D_MODEL), lambda i: (0, 0)),
        ],
        out_specs=pl.BlockSpec((tm, D_MODEL), lambda i: (i, 0)),
        out_shape=jax.ShapeDtypeStruct((m, D_MODEL), F32),
        compiler_params=_cparams(("arbitrary",)),
        name="outproj",
    )(od, on, xf, w_d, w_n, g)


def _ffn_kernel(x_ref, gpre_ref, wg_ref, wu_ref, cwg_ref, cwu_ref, cbg_ref, cbu_ref, wd_ref,
                gpost_ref, o_ref, h_ref, carry_ref, yg_ref, yu_ref, *, tiles_per_seq):
    i = pl.program_id(0)
    j = pl.program_id(1)
    tm = x_ref.shape[0]
    halo = SUBLANES

    @pl.when(j == 0)
    def _():
        h_ref[...] = _rms(x_ref[...], gpre_ref[...]).astype(BF16)
        o_ref[...] = jnp.zeros(o_ref.shape, F32)

    @pl.when(lax.rem(i, tiles_per_seq) == 0)
    def _():
        carry_ref[j] = jnp.zeros(carry_ref.shape[1:], F32)

    def conv(w_ref, cw_ref, cb_ref, slot, y_ref):
        u = _dot(h_ref[...], w_ref[...])
        prev = carry_ref[j, slot]
        carry_ref[j, slot] = u[tm - halo:tm]
        w0, w1, w2 = cw_ref[0:1, :], cw_ref[1:2, :], cw_ref[2:3, :]
        bias = cb_ref[...]

        def taps(z):
            return bias + w2 * z + w1 * pltpu.roll(z, 1, axis=0) + w0 * pltpu.roll(z, 2, axis=0)

        y_ref[...] = taps(u)
        y_ref[0:halo, :] = taps(jnp.concatenate([prev, u[0:halo]], axis=0))[halo:2 * halo]

    conv(wg_ref, cwg_ref, cbg_ref, 0, yg_ref)
    conv(wu_ref, cwu_ref, cbu_ref, 1, yu_ref)
    act = (_gelu_tanh(yg_ref[...]) * yu_ref[...]).astype(BF16)
    o_ref[...] += _dot(act, wd_ref[...])

    @pl.when(j == pl.num_programs(1) - 1)
    def _():
        o_ref[...] = x_ref[...] + _rms(o_ref[...], gpost_ref[...])


def _ffn(x1, gpre, w_up, conv_w, conv_b, w_down, gpost, tm, tf, seq):
    m = x1.shape[0]
    nj = D_FF // tf
    return pl.pallas_call(
        functools.partial(_ffn_kernel, tiles_per_seq=seq // tm),
        grid=(m // tm, nj),
        in_specs=[
            pl.BlockSpec((tm, D_MODEL), lambda i, j: (i, 0)),
            pl.BlockSpec((1, D_MODEL), lambda i, j: (0, 0)),
            pl.BlockSpec((D_MODEL, tf), lambda i, j: (0, j)),
            pl.BlockSpec((D_MODEL, tf), lambda i, j: (0, j + nj)),
            pl.BlockSpec((CONV_W, tf), lambda i, j: (0, j)),
            pl.BlockSpec((CONV_W, tf), lambda i, j: (0, j + nj)),
            pl.BlockSpec((1, tf), lambda i, j: (0, j)),
            pl.BlockSpec((1, tf), lambda i, j: (0, j + nj)),
            pl.BlockSpec((tf, D_MODEL), lambda i, j: (j, 0)),
            pl.BlockSpec((1, D_MODEL), lambda i, j: (0, 0)),
        ],
        out_specs=pl.BlockSpec((tm, D_MODEL), lambda i, j: (i, 0)),
        out_shape=jax.ShapeDtypeStruct((m, D_MODEL), F32),
        scratch_shapes=[
            pltpu.VMEM((tm, D_MODEL), BF16),
            pltpu.VMEM((nj, 2, SUBLANES, tf), F32),
            pltpu.VMEM((tm, tf), F32),
            pltpu.VMEM((tm, tf), F32),
        ],
        compiler_params=_cparams(("arbitrary", "arbitrary"), FFN_VMEM_LIMIT),
        name="ffn",
    )(x1, gpre, w_up, w_up, conv_w, conv_w, conv_b, conv_b, w_down, gpost)


def _rel_bucket(delta):
    n = jnp.maximum(delta, 0)
    max_exact = N_BUCKETS // 2
    nf = jnp.maximum(n, 1).astype(F32)
    large = max_exact + (jnp.log(nf / max_exact) / math.log(MAX_DISTANCE / max_exact)
                         * (N_BUCKETS - max_exact)).astype(jnp.int32)
    large = jnp.minimum(large, N_BUCKETS - 1)
    return jnp.where(n < max_exact, n, large)


def _bias_kernel(tab_ref, bkt_ref, o_ref):
    h = pl.program_id(0)
    i = pl.program_id(1)
    bkt = bkt_ref[...]
    row = jnp.full(bkt.shape, NEG, F32)
    for bucket in range(N_BUCKETS):
        row = jnp.where(bkt == bucket, tab_ref[bucket, h], row)
    tr, width = o_ref.shape[1:]
    rows = jnp.concatenate([row] * (tr // SUBLANES), axis=0)
    o_ref[0] = pltpu.roll(rows, i * tr, axis=1, stride=1, stride_axis=0)[:, 0:width]


def _bias_strip(table, rows, width, off, hi):
    heads = table.shape[1]
    tr = LANES
    wp = width + rows
    t = jnp.arange(wp)
    dist = off - jnp.where(t < width, t, t - wp)
    bkt = jnp.where((dist < 0) | (dist >= hi), -1, _rel_bucket(dist)).astype(jnp.int32)
    bkt = jnp.broadcast_to(bkt[None, :], (SUBLANES, wp))
    return pl.pallas_call(
        _bias_kernel,
        grid=(heads, rows // tr),
        in_specs=[
            pl.BlockSpec(memory_space=pltpu.SMEM),
            pl.BlockSpec((SUBLANES, wp), lambda h, i: (0, 0)),
        ],
        out_specs=pl.BlockSpec((1, tr, width), lambda h, i: (h, i, 0)),
        out_shape=jax.ShapeDtypeStruct((heads, rows, width), F32),
        compiler_params=_cparams(("arbitrary", "arbitrary")),
        name="bias_strip",
    )(table, bkt)


def _selection_map(nb, nc_pad):
    r = SLC_LEN // CMP_STRIDE
    j = jnp.arange(nb)[:, None]
    c = jnp.arange(nc_pad)[None, :]
    off = r * j - c
    out = jnp.zeros((nb, nc_pad), F32)
    for mm in range(r):
        for nn in range(CMP_LEN // CMP_STRIDE):
            out = out + (off == mm + nn).astype(F32)
    return out * (c < nc_pad - 1)


def kernel(x, pre_mix_norm, w_in, lambda_q1, lambda_k1, lambda_q2, lambda_k2, diff_subln,
           cmp_pos_k, cmp_pos_v, cmp_k_w1, cmp_k_w2, cmp_v_w1, cmp_v_w2, rel_bias, w_out,
           post_mix_norm, pre_ffn_norm, w_up, conv_w, conv_b, w_down, post_ffn_norm):
    b, s, d = x.shape
    assert d == D_MODEL and s // SLC_LEN >= SLC_TOPK
    assert s % max(DIFF_TQ, NSA_QPS * NSA_TQ, NSA_TK, INPROJ_TM, OUTPROJ_TM, FFN_TM) == 0
    m = b * s
    depth = w_in.shape[0]
    unbounded = 1 << 30
    for l in range(depth):
        xf = x.reshape(m, d)
        w_main = w_in[l].astype(BF16)
        w_gate = jnp.pad(w_in[l][:, MAIN_COLS:].reshape(d, NSA_KV, 3 * NSA_GROUP),
                         ((0, 0), (0, 0), (0, LANES - 3 * NSA_GROUP)))
        w_gate = w_gate.reshape(d, NSA_KV * LANES).astype(BF16)
        proj, gates, ksx, rows16 = _inproj(xf, pre_mix_norm[l][None], w_main, w_gate,
                                           tm=INPROJ_TM, tn=INPROJ_TN, seq=s)
        proj3 = proj.reshape(b, s, MAIN_COLS)

        bias_diff = rel_bias[:, :DIFF_HEADS]
        bias_nsa = rel_bias[:, DIFF_HEADS:]
        t_diff = _bias_strip((bias_diff - bias_diff[N_BUCKETS - 1]) * LOG2E,
                             DIFF_TQ, 2 * DIFF_TQ, DIFF_TQ, unbounded)
        t_sel = _bias_strip((bias_nsa - bias_nsa[N_BUCKETS - 1]) * LOG2E,
                            NSA_TQ, TS_W, TS_OFF, unbounded)
        t_win = _bias_strip(bias_nsa * LOG2E, NSA_TQ, TW_W, WINDOW, WINDOW)
        t_sel = t_sel.reshape(NSA_KV, NSA_GROUP * NSA_TQ, TS_W)
        t_win = t_win.reshape(NSA_KV, NSA_GROUP * NSA_TQ, TW_W)

        lam_init = 0.8 - 0.6 * math.exp(-0.3 * l)
        o_diff = _diff_attention(proj3, t_diff, lambda_q1[l][None], lambda_k1[l][None],
                                 lambda_q2[l][None], lambda_k2[l][None], diff_subln[l][None],
                                 lam_init)

        nrow = s // CMP_STRIDE
        r = rows16.reshape(2, NSA_KV, b, nrow, CMP_STRIDE * NSA_DK)
        pos = jnp.stack([cmp_pos_k[l].reshape(1, -1), cmp_pos_v[l].reshape(1, -1)])
        w1 = jnp.stack([cmp_k_w1[l], cmp_v_w1[l]]).astype(BF16)
        w2 = jnp.stack([cmp_k_w2[l], cmp_v_w2[l]]).astype(BF16)
        kvc = _compress(r, pos, w1, w2)

        nb = s // SLC_LEN
        o_nsa = _nsa_attention(proj3, kvc, ksx.reshape(b, s, -1), t_sel, t_win,
                               gates.reshape(b, s, -1), _selection_map(nb, nrow).astype(BF16))

        half = DIFF_HEADS * 2 * DIFF_D
        x1 = _outproj(o_diff.reshape(m, -1), o_nsa.reshape(m, -1), xf,
                      w_out[l][:half].astype(BF16), w_out[l][half:].astype(BF16),
                      post_mix_norm[l][None], tm=OUTPROJ_TM)
        x2 = _ffn(x1, pre_ffn_norm[l][None], w_up[l].astype(BF16), conv_w[l], conv_b[l][None],
                  w_down[l].astype(BF16), post_ffn_norm[l][None], tm=FFN_TM, tf=FFN_TF, seq=s)
        x = x2.reshape(b, s, d)
    return x
```

```python
import functools
import math

import jax
import jax.numpy as jnp
from jax import lax
from jax.experimental import pallas as pl
from jax.experimental.pallas import tpu as pltpu

F32 = jnp.float32
BF16 = jnp.bfloat16

D_MODEL = 2048
DIFF_HEADS = 8
DIFF_D = 64
NSA_HEADS = 8
NSA_KV = 2
NSA_GROUP = NSA_HEADS // NSA_KV
NSA_DK = 128
NSA_DV = 128
CMP_LEN = 32
CMP_STRIDE = 16
CMP_HIDDEN = 256
SLC_LEN = 64
SLC_TOPK = 16
WINDOW = 512
N_BUCKETS = 32
MAX_DISTANCE = 128
D_FF = 5632
CONV_W = 3
EPS = 1e-6
NEG = -1e30
TINY = 1e-30
LOG2E = math.log2(math.e)

LANES = 128
SUBLANES = 8
VMEM_LIMIT = 56 * 1024 * 1024
FFN_VMEM_LIMIT = 61 * 1024 * 1024

MAIN_COLS = 5632
COL_DQ, COL_DK, COL_DV, COL_NQ = 0, 1024, 2048, 3072
COL_KC, COL_VC, COL_KS, COL_VS, COL_KW, COL_VW = 4096, 4352, 4608, 4864, 5120, 5376

INPROJ_TM, INPROJ_TN = 1024, 512
OUTPROJ_TM = 512
FFN_TM, FFN_TF = 1024, 512
DIFF_TQ = 512
NSA_TQ = 256
NSA_QPS = 2
FLASH_SUB = 512
FAR_WIDE = 2
NSA_TK = 512
NSA_WK = WINDOW + NSA_TQ
TS_OFF = 2 * NSA_TK - NSA_TQ
TS_W = TS_OFF + NSA_TK
TW_W = WINDOW + NSA_WK


def _cparams(sem, vmem_limit=VMEM_LIMIT):
    return pltpu.CompilerParams(dimension_semantics=sem, vmem_limit_bytes=vmem_limit)


def _rms(x, g):
    return x * lax.rsqrt(jnp.mean(x * x, axis=-1, keepdims=True) + EPS) * g


def _gelu_tanh(x):
    return 0.5 * x * (1.0 + jnp.tanh(math.sqrt(2.0 / math.pi) * (x + 0.044715 * (x * x * x))))


def _dot(a, b):
    return jnp.dot(a, b, preferred_element_type=F32)


def _dot_nt(a, b, precision=None):
    return lax.dot_general(a, b, (((1,), (1,)), ((), ())), preferred_element_type=F32,
                           precision=precision)


def _lane_tile(a, n):
    return a if n == 1 else jnp.concatenate([a] * n, axis=1)


def _online_softmax_update(s, v, m_ref, l_ref, acc_ref, bias=None):
    m_prev, l, acc = m_ref[...], l_ref[...], acc_ref[...]
    dv = v.shape[1]
    ones = jnp.ones((FLASH_SUB, LANES), BF16)
    for c in range(0, s.shape[1], FLASH_SUB):
        sc = s[:, c:c + FLASH_SUB]
        extra = None if bias is None else bias(c)
        if extra is not None:
            sc = sc + extra
        m_new = jnp.maximum(m_prev, jnp.max(sc, axis=1, keepdims=True))
        alpha = jnp.exp2(m_prev - m_new)
        p = jnp.exp2((sc - _lane_tile(m_new, FLASH_SUB // LANES)).astype(BF16))
        pv = _dot(p, jnp.concatenate([v[c:c + FLASH_SUB], ones], axis=1))
        l = alpha * l + pv[:, dv:]
        acc = alpha * acc + pv[:, 0:dv]
        m_prev = m_new
    m_ref[...], l_ref[...], acc_ref[...] = m_prev, l, acc


def _inproj_kernel(x_ref, g_ref, w_ref, wg_ref, proj_ref, gates_ref, ksx_ref, r16_ref, h_ref,
                   stage_ref, *, seq):
    i = pl.program_id(0)
    j = pl.program_id(1)
    tm, tn = proj_ref.shape

    @pl.when(j == 0)
    def _():
        h_ref[...] = _rms(x_ref[...], g_ref[...]).astype(BF16)
        gl = _dot(h_ref[...], wg_ref[...])
        gates_ref[...] = 1.0 / (1.0 + jnp.exp(-gl))

    acc = _dot(h_ref[...], w_ref[...])
    proj_ref[...] = acc.astype(BF16)

    @pl.when(j == COL_KS // tn)
    def _():
        nb = seq // SLC_LEN
        xw = NSA_DK + nb
        pos = lax.rem(i * tm + lax.broadcasted_iota(jnp.int32, (tm, nb), 0), seq)
        blk = lax.shift_right_logical(pos, int(math.log2(SLC_LEN)))
        onehot = jnp.where(blk == lax.broadcasted_iota(jnp.int32, (tm, nb), 1), 1.0, 0.0)
        for g in range(NSA_KV):
            ksx_ref[:, g * xw:g * xw + NSA_DK] = acc[:, g * NSA_DK:(g + 1) * NSA_DK].astype(BF16)
            ksx_ref[:, g * xw + NSA_DK:(g + 1) * xw] = onehot.astype(BF16)

    @pl.when(j == COL_KC // tn)
    def _():
        nr = tm // CMP_STRIDE
        for t in range(2):
            for g in range(NSA_KV):
                col = (t * NSA_KV + g) * NSA_DK
                stage_ref[...] = acc[:, col:col + NSA_DK]
                for p in range(CMP_STRIDE):
                    r16_ref[t, g, :, p * NSA_DK:(p + 1) * NSA_DK] = (
                        stage_ref[pl.ds(p, nr, stride=CMP_STRIDE), :].astype(BF16))


def _inproj(xf, g, w_main, w_gate, tm, tn, seq):
    m = xf.shape[0]
    assert COL_KS % tn == 0 and COL_KC % tn == 0 and tn == 2 * NSA_KV * NSA_DK
    xw = NSA_DK + seq // SLC_LEN
    ng = w_gate.shape[1]
    return pl.pallas_call(
        functools.partial(_inproj_kernel, seq=seq),
        grid=(m // tm, MAIN_COLS // tn),
        in_specs=[
            pl.BlockSpec((tm, D_MODEL), lambda i, j: (i, 0)),
            pl.BlockSpec((1, D_MODEL), lambda i, j: (0, 0)),
            pl.BlockSpec((D_MODEL, tn), lambda i, j: (0, j)),
            pl.BlockSpec((D_MODEL, ng), lambda i, j: (0, 0)),
        ],
        out_specs=[
            pl.BlockSpec((tm, tn), lambda i, j: (i, j)),
            pl.BlockSpec((tm, ng), lambda i, j: (i, 0)),
            pl.BlockSpec((tm, NSA_KV * xw), lambda i, j: (i, 0)),
            pl.BlockSpec((2, NSA_KV, tm // CMP_STRIDE, CMP_STRIDE * NSA_DK),
                         lambda i, j: (0, 0, i, 0)),
        ],
        out_shape=[
            jax.ShapeDtypeStruct((m, MAIN_COLS), BF16),
            jax.ShapeDtypeStruct((m, ng), F32),
            jax.ShapeDtypeStruct((m, NSA_KV * xw), BF16),
            jax.ShapeDtypeStruct((2, NSA_KV, m // CMP_STRIDE, CMP_STRIDE * NSA_DK), BF16),
        ],
        scratch_shapes=[pltpu.VMEM((tm, D_MODEL), BF16), pltpu.VMEM((tm, NSA_DK), F32)],
        compiler_params=_cparams(("arbitrary", "arbitrary")),
        name="inproj",
    )(xf, g, w_main, w_gate)


def _diff_kernel(q_ref, k_ref, v_ref, t_ref, lq1_ref, lk1_ref, lq2_ref, lk2_ref, sub_ref,
                 o_ref, qs_ref, m_ref, l_ref, acc_ref, *, lam_init):
    tq = DIFF_TQ
    nq = q_ref.shape[1] // tq
    lam = (jnp.exp(jnp.sum(lq1_ref[...] * lk1_ref[...], axis=1, keepdims=True))
           - jnp.exp(jnp.sum(lq2_ref[...] * lk2_ref[...], axis=1, keepdims=True)) + lam_init)

    def tile_rows(i):
        start = i * tq
        return pl.ds(start if isinstance(i, int) else pl.multiple_of(start, tq), tq)

    def begin(i):
        q = q_ref[0, tile_rows(i), :].astype(F32) * (DIFF_D ** -0.5 * LOG2E)
        lane = lax.broadcasted_iota(jnp.int32, q.shape, 1)
        qs_ref[...] = jnp.concatenate(
            [jnp.where(lane < DIFF_D, q, 0.0), jnp.where(lane >= DIFF_D, q, 0.0)],
            axis=0).astype(BF16)
        m_ref[...] = jnp.full(m_ref.shape, NEG, F32)
        l_ref[...] = jnp.zeros(l_ref.shape, F32)
        acc_ref[...] = jnp.zeros(acc_ref.shape, F32)

    def finish(i):
        o = acc_ref[...] / jnp.maximum(l_ref[...], TINY)
        a = o[0:tq] - lam * o[tq:2 * tq]
        o_ref[0, tile_rows(i), :] = (_rms(a, sub_ref[...]) * (1.0 - lam_init)).astype(BF16)

    def flash(i):
        _diff_flash_tile(i, k_ref, v_ref, t_ref, qs_ref, m_ref, l_ref, acc_ref)

    begin(0)
    flash(0)

    def query_tile(i, carry):
        finish(i - 1)
        begin(i)
        flash(i)
        return carry

    lax.fori_loop(1, nq, query_tile, 0)
    finish(nq - 1)


def _diff_flash_tile(i, k_ref, v_ref, t_ref, qs_ref, m_ref, l_ref, acc_ref):
    tq = DIFF_TQ

    def step(kstart, width, t_col):
        def bias(c):
            if t_col is None or t_col + c < 0:
                return None
            t = t_ref[0, :, t_col + c:t_col + c + FLASH_SUB]
            return jnp.concatenate([t, t], axis=0)

        k = k_ref[0, pl.ds(kstart, width), :]
        s = _dot_nt(qs_ref[...], k)
        _online_softmax_update(s, v_ref[0, pl.ds(kstart, width), :], m_ref, l_ref, acc_ref, bias)

    if isinstance(i, int) and i == 0:
        step(0, tq, tq)
        return

    nfar = i - 1
    nwide = nfar // FAR_WIDE

    def far_body(j, carry):
        step(pl.multiple_of(j * (FAR_WIDE * tq), FAR_WIDE * tq), FAR_WIDE * tq, None)
        return carry

    lax.fori_loop(0, nwide, far_body, 0)
    left = nfar - nwide * FAR_WIDE
    odd = lax.rem(left, 2) == 1

    if FAR_WIDE > 2:
        @pl.when(left >= 2)
        def _():
            step(pl.multiple_of(nwide * (FAR_WIDE * tq), 2 * tq), 2 * tq, None)

    @pl.when(odd)
    def _():
        step(pl.multiple_of((i - 2) * tq, tq), 3 * tq, -tq)

    @pl.when(jnp.logical_not(odd))
    def _():
        step(pl.multiple_of((i - 1) * tq, tq), 2 * tq, 0)


def _diff_attention(proj3, t_diff, lq1, lk1, lq2, lk2, subln, lam_init):
    b, s, _ = proj3.shape
    tq = DIFF_TQ
    vec = lambda n: pl.BlockSpec((1, n), lambda bi, h: (0, 0))
    return pl.pallas_call(
        functools.partial(_diff_kernel, lam_init=lam_init),
        grid=(b, DIFF_HEADS),
        in_specs=[
            pl.BlockSpec((1, s, LANES), lambda bi, h: (bi, 0, COL_DQ // LANES + h)),
            pl.BlockSpec((1, s, LANES), lambda bi, h: (bi, 0, COL_DK // LANES + h)),
            pl.BlockSpec((1, s, LANES), lambda bi, h: (bi, 0, COL_DV // LANES + h)),
            pl.BlockSpec((1, tq, 2 * tq), lambda bi, h: (h, 0, 0)),
            vec(DIFF_D), vec(DIFF_D), vec(DIFF_D), vec(DIFF_D), vec(2 * DIFF_D),
        ],
        out_specs=pl.BlockSpec((1, s, LANES), lambda bi, h: (bi, 0, h)),
        out_shape=jax.ShapeDtypeStruct((b, s, DIFF_HEADS * 2 * DIFF_D), BF16),
        scratch_shapes=[
            pltpu.VMEM((2 * tq, LANES), BF16),
            pltpu.VMEM((2 * tq, LANES), F32),
            pltpu.VMEM((2 * tq, LANES), F32),
            pltpu.VMEM((2 * tq, LANES), F32),
        ],
        compiler_params=_cparams(("arbitrary", "arbitrary")),
        name="diff_attn",
    )(proj3, proj3, proj3, t_diff, lq1, lk1, lq2, lk2, subln)


def _compress_kernel(r_ref, pos_ref, w1_ref, w2_ref, o_ref):
    half = CMP_STRIDE * NSA_DK
    r = r_ref[0, 0, 0]
    n = r.shape[0]
    a = _dot(r, w1_ref[0, 0:half, :])
    bm = _dot(r, w1_ref[0, half:2 * half, :])
    posb = jnp.broadcast_to(pos_ref[0], (SUBLANES, 2 * half)).astype(BF16)
    pt = _dot(posb, w1_ref[0])[0:1]
    hid = a + pltpu.roll(bm, n - 1, axis=0) + pt
    o_ref[0, 0, 0] = _dot(_gelu_tanh(hid).astype(BF16), w2_ref[0]).astype(BF16)


def _compress(r, pos, w1, w2):
    _, g, b, n, width = r.shape
    return pl.pallas_call(
        _compress_kernel,
        grid=(2, b, g),
        in_specs=[
            pl.BlockSpec((1, 1, 1, n, width), lambda t, bi, gi: (t, gi, bi, 0, 0)),
            pl.BlockSpec((1, 1, CMP_LEN * NSA_DK), lambda t, bi, gi: (t, 0, 0)),
            pl.BlockSpec((1, CMP_LEN * NSA_DK, CMP_HIDDEN), lambda t, bi, gi: (t, 0, 0)),
            pl.BlockSpec((1, CMP_HIDDEN, NSA_DK), lambda t, bi, gi: (t, 0, 0)),
        ],
        out_specs=pl.BlockSpec((1, 1, 1, n, NSA_DK), lambda t, bi, gi: (t, bi, gi, 0, 0)),
        out_shape=jax.ShapeDtypeStruct((2, b, g, n, NSA_DK), BF16),
        compiler_params=_cparams(("arbitrary", "arbitrary", "arbitrary")),
        name="nsa_compress",
    )(r, pos, w1, w2)


def _nsa_kernel(q_ref, kc_ref, vc_ref, ks_ref, vs_ref, kw_ref, vw_ref, ts_ref, tw_ref, gate_ref,
                selmap_ref, o_ref, qx_ref, m_ref, l_ref, acc_ref, part_ref):
    tq, tk, hg = NSA_TQ, NSA_TK, NSA_GROUP
    rows = hg * tq
    first_tile = pl.program_id(2) * NSA_QPS

    def gcol(c, trows):
        return jnp.concatenate([gate_ref[0, trows, 3 * h + c:3 * h + c + 1] for h in range(hg)],
                               axis=0)

    def before_loop(t, nc):
        q0 = (first_tile + t) * tq
        trows = slice(t * tq, (t + 1) * tq)
        qt = q_ref[0, trows, :]
        q4 = jnp.concatenate([qt[:, h * NSA_DK:(h + 1) * NSA_DK] for h in range(hg)], axis=0)
        q4 = (q4.astype(F32) * (NSA_DK ** -0.5 * LOG2E)).astype(BF16)

        kc = kc_ref[0, 0, 0, 0:nc, :]
        cend = lax.broadcasted_iota(jnp.int32, (tq, nc), 1) * CMP_STRIDE + (CMP_LEN - 1)
        hidden = jnp.where(cend <= q0 + lax.broadcasted_iota(jnp.int32, (tq, nc), 0), 0.0, NEG)
        sc = _dot_nt(q4, kc) + jnp.concatenate([hidden] * hg, axis=0)
        mc = jnp.max(sc, axis=1, keepdims=True)
        pc = jnp.exp2(sc - mc)
        norm = jnp.where(mc > 0.5 * NEG,
                         1.0 / jnp.maximum(jnp.sum(pc, axis=1, keepdims=True), TINY), 0.0)
        pc = pc * norm
        o_cmp = _dot(pc.astype(BF16), vc_ref[0, 0, 0, 0:nc, :])

        pcsum = pc[0:tq] + pc[tq:2 * tq] + pc[2 * tq:3 * tq] + pc[3 * tq:4 * tq]
        p_hi = pcsum.astype(BF16)
        p_res = pcsum - p_hi.astype(F32)
        p_mid = p_res.astype(BF16)
        p_lo = (p_res - p_mid.astype(F32)).astype(BF16)
        selmap = selmap_ref[:, 0:nc]
        imp_t = _dot_nt(selmap, p_hi) + _dot_nt(selmap, p_mid) + _dot_nt(selmap, p_lo)
        nb = imp_t.shape[0]
        blk = lax.broadcasted_iota(jnp.int32, (nb, tq), 0)
        cur = lax.shift_right_logical(q0 + lax.broadcasted_iota(jnp.int32, (nb, tq), 1),
                                      int(math.log2(SLC_LEN)))
        forced = (blk == 0) | (blk == cur) | (blk == cur - 1)
        excluded = -3.0e38
        val = jnp.where(forced | (blk > cur), excluded, imp_t)
        for _ in range(min(SLC_TOPK, nb) - 3):
            best = jnp.max(val, axis=0, keepdims=True)
            first = jnp.min(jnp.where(val == best, blk, nb), axis=0, keepdims=True)
            val = jnp.where(blk == first, excluded, val)
        maskbias = jnp.where(val.T < 0.5 * excluded, 0.0, NEG).astype(BF16)
        qx_ref[t, :, 0:NSA_DK] = q4
        qx_ref[t, :, NSA_DK:NSA_DK + nb] = jnp.concatenate([maskbias] * hg, axis=0)

        kst = pl.multiple_of(jnp.maximum(q0 - WINDOW, 0), tq)
        w0 = pl.multiple_of(WINDOW - (q0 - kst), LANES)
        lw = _dot_nt(q4, kw_ref[0, pl.ds(kst, NSA_WK), :]) + tw_ref[0, :, pl.ds(w0, NSA_WK)]
        pw = jnp.exp2((lw - jnp.max(lw, axis=1, keepdims=True)).astype(BF16))
        vw1 = jnp.concatenate([vw_ref[0, pl.ds(kst, NSA_WK), :], jnp.ones((NSA_WK, LANES), BF16)],
                              axis=1)
        o_win = _dot(pw, vw1)
        o_win = o_win[:, 0:NSA_DV] / jnp.maximum(o_win[:, NSA_DV:], TINY)

        part_ref[t] = gcol(0, trows) * o_cmp + gcol(2, trows) * o_win

    nc_all = kc_ref.shape[3]
    early = (first_tile + NSA_QPS) * tq <= (nc_all // 2) * CMP_STRIDE

    @pl.when(early)
    def _():
        for t in range(NSA_QPS):
            before_loop(t, nc_all // 2)

    @pl.when(jnp.logical_not(early))
    def _():
        for t in range(NSA_QPS):
            before_loop(t, nc_all)

    def flash_tile(t, carry):
        _nsa_flash_tile(t, (first_tile + t) * tq, gcol, ks_ref, vs_ref, ts_ref, o_ref, qx_ref, m_ref,
                        l_ref, acc_ref, part_ref)
        return carry

    lax.fori_loop(0, NSA_QPS, flash_tile, 0)


def _nsa_flash_tile(t, q0, gcol, ks_ref, vs_ref, ts_ref, o_ref, qx_ref, m_ref, l_ref, acc_ref,
                    part_ref):
    tq, tk, hg = NSA_TQ, NSA_TK, NSA_GROUP
    trows = pl.ds(pl.multiple_of(t * tq, tq), tq)
    m_ref[...] = jnp.full(m_ref.shape, NEG, F32)
    l_ref[...] = jnp.zeros(l_ref.shape, F32)
    acc_ref[...] = jnp.zeros(acc_ref.shape, F32)

    def step(j, width, first_near):
        kstart = pl.multiple_of(j * tk, tk)

        def bias(c):
            if first_near is None or c < first_near * tk:
                return None
            u0 = pl.multiple_of(TS_OFF - (q0 - j * tk) + c, LANES)
            return ts_ref[0, :, pl.ds(u0, FLASH_SUB)]

        s = _dot_nt(qx_ref[t], ks_ref[0, pl.ds(kstart, width), :])
        _online_softmax_update(s, vs_ref[0, pl.ds(kstart, width), :], m_ref, l_ref, acc_ref, bias)

    jl = lax.shift_right_logical(q0 + tq - 1, int(math.log2(tk)))
    nfar = jnp.maximum(jl - 1, 0)
    nwide = nfar // FAR_WIDE

    def far_body(j, carry):
        step(FAR_WIDE * j, FAR_WIDE * tk, None)
        return carry

    lax.fori_loop(0, nwide, far_body, 0)
    left = nfar - nwide * FAR_WIDE
    odd = lax.rem(left, 2) == 1

    if FAR_WIDE > 2:
        @pl.when(left >= 2)
        def _():
            step(nwide * FAR_WIDE, 2 * tk, None)

    @pl.when(odd)
    def _():
        step(jl - 2, 3 * tk, 1)

    @pl.when(jnp.logical_and(jl > 0, jnp.logical_not(odd)))
    def _():
        step(jl - 1, 2 * tk, 0)

    @pl.when(jl == 0)
    def _():
        step(0, tk, 0)
    o = part_ref[t] + gcol(1, trows) * (acc_ref[...] / jnp.maximum(l_ref[...], TINY))
    for h in range(hg):
        o_ref[0, trows, h * NSA_DV:(h + 1) * NSA_DV] = o[h * tq:(h + 1) * tq].astype(BF16)


def _nsa_attention(proj3, kvc, ksx3, t_sel, t_win, gates3, selmap):
    b, s, _ = proj3.shape
    tq, hg = NSA_TQ, NSA_GROUP
    nc = kvc.shape[3]
    nb = s // SLC_LEN
    qcols = hg * NSA_DK
    return pl.pallas_call(
        _nsa_kernel,
        grid=(b, NSA_KV, s // (NSA_QPS * tq)),
        in_specs=[
            pl.BlockSpec((1, NSA_QPS * tq, qcols), lambda bi, g, i: (bi, i, COL_NQ // qcols + g)),
            pl.BlockSpec((1, 1, 1, nc, NSA_DK), lambda bi, g, i: (0, bi, g, 0, 0)),
            pl.BlockSpec((1, 1, 1, nc, NSA_DV), lambda bi, g, i: (1, bi, g, 0, 0)),
            pl.BlockSpec((1, s, NSA_DK + nb), lambda bi, g, i: (bi, 0, g)),
            pl.BlockSpec((1, s, NSA_DV), lambda bi, g, i: (bi, 0, COL_VS // NSA_DV + g)),
            pl.BlockSpec((1, s, NSA_DK), lambda bi, g, i: (bi, 0, COL_KW // NSA_DK + g)),
            pl.BlockSpec((1, s, NSA_DV), lambda bi, g, i: (bi, 0, COL_VW // NSA_DV + g)),
            pl.BlockSpec((1, hg * tq, TS_W), lambda bi, g, i: (g, 0, 0),
                         pipeline_mode=pl.Buffered(1)),
            pl.BlockSpec((1, hg * tq, TW_W), lambda bi, g, i: (g, 0, 0),
                         pipeline_mode=pl.Buffered(1)),
            pl.BlockSpec((1, NSA_QPS * tq, LANES), lambda bi, g, i: (bi, i, g)),
            pl.BlockSpec((nb, nc), lambda bi, g, i: (0, 0)),
        ],
        out_specs=pl.BlockSpec((1, NSA_QPS * tq, hg * NSA_DV), lambda bi, g, i: (bi, i, g)),
        out_shape=jax.ShapeDtypeStruct((b, s, NSA_HEADS * NSA_DV), BF16),
        scratch_shapes=[
            pltpu.VMEM((NSA_QPS, hg * tq, NSA_DK + nb), BF16),
            pltpu.VMEM((hg * tq, LANES), F32),
            pltpu.VMEM((hg * tq, LANES), F32),
            pltpu.VMEM((hg * tq, NSA_DV), F32),
            pltpu.VMEM((NSA_QPS, hg * tq, NSA_DV), F32),
        ],
        compiler_params=_cparams(("arbitrary", "arbitrary", "arbitrary")),
        name="nsa_attn",
    )(proj3, kvc, kvc, ksx3, proj3, proj3, proj3, t_sel, t_win, gates3, selmap)


def _outproj_kernel(od_ref, on_ref, x_ref, wd_ref, wn_ref, g_ref, o_ref):
    y = _dot(od_ref[...], wd_ref[...]) + _dot(on_ref[...], wn_ref[...])
    o_ref[...] = x_ref[...] + _rms(y, g_ref[...])


def _outproj(od, on, xf, w_d, w_n, g, tm):
    m = xf.shape[0]
    kd, kn = od.shape[1], on.shape[1]
    return pl.pallas_call(
        _outproj_kernel,
        grid=(m // tm,),
        in_specs=[
            pl.BlockSpec((tm, kd), lambda i: (i, 0)),
            pl.BlockSpec((tm, kn), lambda i: (i, 0)),
            pl.BlockSpec((tm, D_MODEL), lambda i: (i, 0)),
            pl.BlockSpec((kd, D_MODEL), lambda i: (0, 0)),
            pl.BlockSpec((kn, D_MODEL), lambda i: (0, 0)),
            pl.BlockSpec((1, D_MODEL), lambda i: (0, 0)),
        ],
        out_specs=pl.BlockSpec((tm, D_MODEL), lambda i: (i, 0)),
        out_shape=jax.ShapeDtypeStruct((m, D_MODEL), F32),
        compiler_params=_cparams(("arbitrary",)),
        name="outproj",
    )(od, on, xf, w_d, w_n, g)


def _ffn_kernel(x_ref, gpre_ref, wg_ref, wu_ref, cwg_ref, cwu_ref, cbg_ref, cbu_ref, wd_ref,
                gpost_ref, o_ref, h_ref, carry_ref, yg_ref, yu_ref, *, tiles_per_seq):
    i = pl.program_id(0)
    j = pl.program_id(1)
    tm = x_ref.shape[0]
    halo = SUBLANES

    @pl.when(j == 0)
    def _():
        h_ref[...] = _rms(x_ref[...], gpre_ref[...]).astype(BF16)
        o_ref[...] = jnp.zeros(o_ref.shape, F32)

    @pl.when(lax.rem(i, tiles_per_seq) == 0)
    def _():
        carry_ref[j] = jnp.zeros(carry_ref.shape[1:], F32)

    def conv(w_ref, cw_ref, cb_ref, slot, y_ref):
        u = _dot(h_ref[...], w_ref[...])
        prev = carry_ref[j, slot]
        carry_ref[j, slot] = u[tm - halo:tm]
        w0, w1, w2 = cw_ref[0:1, :], cw_ref[1:2, :], cw_ref[2:3, :]
        bias = cb_ref[...]

        def taps(z):
            return bias + w2 * z + w1 * pltpu.roll(z, 1, axis=0) + w0 * pltpu.roll(z, 2, axis=0)

        y_ref[...] = taps(u)
        y_ref[0:halo, :] = taps(jnp.concatenate([prev, u[0:halo]], axis=0))[halo:2 * halo]

    conv(wg_ref, cwg_ref, cbg_ref, 0, yg_ref)
    conv(wu_ref, cwu_ref, cbu_ref, 1, yu_ref)
    act = (_gelu_tanh(yg_ref[...]) * yu_ref[...]).astype(BF16)
    o_ref[...] += _dot(act, wd_ref[...])

    @pl.when(j == pl.num_programs(1) - 1)
    def _():
        o_ref[...] = x_ref[...] + _rms(o_ref[...], gpost_ref[...])


def _ffn(x1, gpre, w_up, conv_w, conv_b, w_down, gpost, tm, tf, seq):
    m = x1.shape[0]
    nj = D_FF // tf
    return pl.pallas_call(
        functools.partial(_ffn_kernel, tiles_per_seq=seq // tm),
        grid=(m // tm, nj),
        in_specs=[
            pl.BlockSpec((tm, D_MODEL), lambda i, j: (i, 0)),
            pl.BlockSpec((1, D_MODEL), lambda i, j: (0, 0)),
            pl.BlockSpec((D_MODEL, tf), lambda i, j: (0, j)),
            pl.BlockSpec((D_MODEL, tf), lambda i, j: (0, j + nj)),
            pl.BlockSpec((CONV_W, tf), lambda i, j: (0, j)),
            pl.BlockSpec((CONV_W, tf), lambda i, j: (0, j + nj)),
            pl.BlockSpec((1, tf), lambda i, j: (0, j)),
            pl.BlockSpec((1, tf), lambda i, j: (0, j + nj)),
            pl.BlockSpec((tf, D_MODEL), lambda i, j: (j, 0)),
            pl.BlockSpec((1, D_MODEL), lambda i, j: (0, 0)),
        ],
        out_specs=pl.BlockSpec((tm, D_MODEL), lambda i, j: (i, 0)),
        out_shape=jax.ShapeDtypeStruct((m, D_MODEL), F32),
        scratch_shapes=[
            pltpu.VMEM((tm, D_MODEL), BF16),
            pltpu.VMEM((nj, 2, SUBLANES, tf), F32),
            pltpu.VMEM((tm, tf), F32),
            pltpu.VMEM((tm, tf), F32),
        ],
        compiler_params=_cparams(("arbitrary", "arbitrary"), FFN_VMEM_LIMIT),
        name="ffn",
    )(x1, gpre, w_up, w_up, conv_w, conv_w, conv_b, conv_b, w_down, gpost)


def _rel_bucket(delta):
    n = jnp.maximum(delta, 0)
    max_exact = N_BUCKETS // 2
    nf = jnp.maximum(n, 1).astype(F32)
    large = max_exact + (jnp.log(nf / max_exact) / math.log(MAX_DISTANCE / max_exact)
                         * (N_BUCKETS - max_exact)).astype(jnp.int32)
    large = jnp.minimum(large, N_BUCKETS - 1)
    return jnp.where(n < max_exact, n, large)


def _bias_kernel(tab_ref, bkt_ref, o_ref):
    h = pl.program_id(0)
    i = pl.program_id(1)
    bkt = bkt_ref[...]
    row = jnp.full(bkt.shape, NEG, F32)
    for bucket in range(N_BUCKETS):
        row = jnp.where(bkt == bucket, tab_ref[bucket, h], row)
    tr, width = o_ref.shape[1:]
    rows = jnp.concatenate([row] * (tr // SUBLANES), axis=0)
    o_ref[0] = pltpu.roll(rows, i * tr, axis=1, stride=1, stride_axis=0)[:, 0:width]


def _bias_strip(table, rows, width, off, hi):
    heads = table.shape[1]
    tr = LANES
    wp = width + rows
    t = jnp.arange(wp)
    dist = off - jnp.where(t < width, t, t - wp)
    bkt = jnp.where((dist < 0) | (dist >= hi), -1, _rel_bucket(dist)).astype(jnp.int32)
    bkt = jnp.broadcast_to(bkt[None, :], (SUBLANES, wp))
    return pl.pallas_call(
        _bias_kernel,
        grid=(heads, rows // tr),
        in_specs=[
            pl.BlockSpec(memory_space=pltpu.SMEM),
            pl.BlockSpec((SUBLANES, wp), lambda h, i: (0, 0)),
        ],
        out_specs=pl.BlockSpec((1, tr, width), lambda h, i: (h, i, 0)),
        out_shape=jax.ShapeDtypeStruct((heads, rows, width), F32),
        compiler_params=_cparams(("arbitrary", "arbitrary")),
        name="bias_strip",
    )(table, bkt)


def _selection_map(nb, nc_pad):
    r = SLC_LEN // CMP_STRIDE
    j = jnp.arange(nb)[:, None]
    c = jnp.arange(nc_pad)[None, :]
    off = r * j - c
    out = jnp.zeros((nb, nc_pad), F32)
    for mm in range(r):
        for nn in range(CMP_LEN // CMP_STRIDE):
            out = out + (off == mm + nn).astype(F32)
    return out * (c < nc_pad - 1)


def kernel(x, pre_mix_norm, w_in, lambda_q1, lambda_k1, lambda_q2, lambda_k2, diff_subln,
           cmp_pos_k, cmp_pos_v, cmp_k_w1, cmp_k_w2, cmp_v_w1, cmp_v_w2, rel_bias, w_out,
           post_mix_norm, pre_ffn_norm, w_up, conv_w, conv_b, w_down, post_ffn_norm):
    b, s, d = x.shape
    assert d == D_MODEL and s // SLC_LEN >= SLC_TOPK
    assert s % max(DIFF_TQ, NSA_QPS * NSA_TQ, NSA_TK, INPROJ_TM, OUTPROJ_TM, FFN_TM) == 0
    m = b * s
    depth = w_in.shape[0]
    unbounded = 1 << 30
    for l in range(depth):
        xf = x.reshape(m, d)
        w_main = w_in[l].astype(BF16)
        w_gate = jnp.pad(w_in[l][:, MAIN_COLS:].reshape(d, NSA_KV, 3 * NSA_GROUP),
                         ((0, 0), (0, 0), (0, LANES - 3 * NSA_GROUP)))
        w_gate = w_gate.reshape(d, NSA_KV * LANES).astype(BF16)
        proj, gates, ksx, rows16 = _inproj(xf, pre_mix_norm[l][None], w_main, w_gate,
                                           tm=INPROJ_TM, tn=INPROJ_TN, seq=s)
        proj3 = proj.reshape(b, s, MAIN_COLS)

        bias_diff = rel_bias[:, :DIFF_HEADS]
        bias_nsa = rel_bias[:, DIFF_HEADS:]
        t_diff = _bias_strip((bias_diff - bias_diff[N_BUCKETS - 1]) * LOG2E,
                             DIFF_TQ, 2 * DIFF_TQ, DIFF_TQ, unbounded)
        t_sel = _bias_strip((bias_nsa - bias_nsa[N_BUCKETS - 1]) * LOG2E,
                            NSA_TQ, TS_W, TS_OFF, unbounded)
        t_win = _bias_strip(bias_nsa * LOG2E, NSA_TQ, TW_W, WINDOW, WINDOW)
        t_sel = t_sel.reshape(NSA_KV, NSA_GROUP * NSA_TQ, TS_W)
        t_win = t_win.reshape(NSA_KV, NSA_GROUP * NSA_TQ, TW_W)

        lam_init = 0.8 - 0.6 * math.exp(-0.3 * l)
        o_diff = _diff_attention(proj3, t_diff, lambda_q1[l][None], lambda_k1[l][None],
                                 lambda_q2[l][None], lambda_k2[l][None], diff_subln[l][None],
                                 lam_init)

        nrow = s // CMP_STRIDE
        r = rows16.reshape(2, NSA_KV, b, nrow, CMP_STRIDE * NSA_DK)
        pos = jnp.stack([cmp_pos_k[l].reshape(1, -1), cmp_pos_v[l].reshape(1, -1)])
        w1 = jnp.stack([cmp_k_w1[l], cmp_v_w1[l]]).astype(BF16)
        w2 = jnp.stack([cmp_k_w2[l], cmp_v_w2[l]]).astype(BF16)
        kvc = _compress(r, pos, w1, w2)

        nb = s // SLC_LEN
        o_nsa = _nsa_attention(proj3, kvc, ksx.reshape(b, s, -1), t_sel, t_win,
                               gates.reshape(b, s, -1), _selection_map(nb, nrow).astype(BF16))

        half = DIFF_HEADS * 2 * DIFF_D
        x1 = _outproj(o_diff.reshape(m, -1), o_nsa.reshape(m, -1), xf,
                      w_out[l][:half].astype(BF16), w_out[l][half:].astype(BF16),
                      post_mix_norm[l][None], tm=OUTPROJ_TM)
        x2 = _ffn(x1, pre_ffn_norm[l][None], w_up[l].astype(BF16), conv_w[l], conv_b[l][None],
                  w_down[l].astype(BF16), post_ffn_norm[l][None], tm=FFN_TM, tf=FFN_TF, seq=s)
        x = x2.reshape(b, s, d)
    return x
```

```python
import functools
import math

import jax
import jax.numpy as jnp
from jax import lax
from jax.experimental import pallas as pl
from jax.experimental.pallas import tpu as pltpu

F32 = jnp.float32
BF16 = jnp.bfloat16

D_MODEL = 2048
DIFF_HEADS = 8
DIFF_D = 64
NSA_HEADS = 8
NSA_KV = 2
NSA_GROUP = NSA_HEADS // NSA_KV
NSA_DK = 128
NSA_DV = 128
CMP_LEN = 32
CMP_STRIDE = 16
CMP_HIDDEN = 256
SLC_LEN = 64
SLC_TOPK = 16
WINDOW = 512
N_BUCKETS = 32
MAX_DISTANCE = 128
D_FF = 5632
CONV_W = 3
EPS = 1e-6
NEG = -1e30
TINY = 1e-30
LOG2E = math.log2(math.e)

LANES = 128
SUBLANES = 8
VMEM_LIMIT = 56 * 1024 * 1024
FFN_VMEM_LIMIT = 61 * 1024 * 1024

MAIN_COLS = 5632
COL_DQ, COL_DK, COL_DV, COL_NQ = 0, 1024, 2048, 3072
COL_KC, COL_VC, COL_KS, COL_VS, COL_KW, COL_VW = 4096, 4352, 4608, 4864, 5120, 5376

INPROJ_TM, INPROJ_TN = 1024, 512
OUTPROJ_TM = 512
FFN_TM, FFN_TF = 1024, 512
DIFF_TQ = 512
NSA_TQ = 256
NSA_QPS = 2
FLASH_SUB = 512
FAR_WIDE = 2
NSA_TK = 512
NSA_WK = WINDOW + NSA_TQ
TS_OFF = 2 * NSA_TK - NSA_TQ
TS_W = TS_OFF + NSA_TK
TW_W = WINDOW + NSA_WK


def _cparams(sem, vmem_limit=VMEM_LIMIT):
    return pltpu.CompilerParams(dimension_semantics=sem, vmem_limit_bytes=vmem_limit)


def _rms(x, g):
    return x * lax.rsqrt(jnp.mean(x * x, axis=-1, keepdims=True) + EPS) * g


def _gelu_tanh(x):
    return 0.5 * x * (1.0 + jnp.tanh(math.sqrt(2.0 / math.pi) * (x + 0.044715 * (x * x * x))))


def _dot(a, b):
    return jnp.dot(a, b, preferred_element_type=F32)


def _dot_nt(a, b, precision=None):
    return lax.dot_general(a, b, (((1,), (1,)), ((), ())), preferred_element_type=F32,
                           precision=precision)


def _lane_tile(a, n):
    return a if n == 1 else jnp.concatenate([a] * n, axis=1)


def _online_softmax_update(s, v, m_ref, l_ref, acc_ref, bias=None):
    m_prev, l, acc = m_ref[...], l_ref[...], acc_ref[...]
    dv = v.shape[1]
    ones = jnp.ones((FLASH_SUB, LANES), BF16)
    for c in range(0, s.shape[1], FLASH_SUB):
        sc = s[:, c:c + FLASH_SUB]
        extra = None if bias is None else bias(c)
        if extra is not None:
            sc = sc + extra
        m_new = jnp.maximum(m_prev, jnp.max(sc, axis=1, keepdims=True))
        alpha = jnp.exp2(m_prev - m_new)
        p = jnp.exp2((sc - _lane_tile(m_new, FLASH_SUB // LANES)).astype(BF16))
        pv = _dot(p, jnp.concatenate([v[c:c + FLASH_SUB], ones], axis=1))
        l = alpha * l + pv[:, dv:]
        acc = alpha * acc + pv[:, 0:dv]
        m_prev = m_new
    m_ref[...], l_ref[...], acc_ref[...] = m_prev, l, acc


def _inproj_kernel(x_hbm, g_ref, w_ref, wg_ref, proj_ref, gates_ref, ksx_ref, r16_ref, h_ref,
                   stage_ref, xbuf_ref, xsem, *, seq):
    i = pl.program_id(0)
    j = pl.program_id(1)
    tm, tn = proj_ref.shape
    slot = lax.rem(i, 2)

    def x_copy(tile, buf):
        return pltpu.make_async_copy(x_hbm.at[pl.ds(tile * tm, tm), :], xbuf_ref.at[buf],
                                     xsem.at[buf])

    @pl.when(jnp.logical_and(j == 0, i == 0))
    def _():
        x_copy(0, 0).start()

    @pl.when(j == 0)
    def _():
        x_copy(i, slot).wait()
        h_ref[...] = _rms(xbuf_ref[slot], g_ref[...]).astype(BF16)
        gl = _dot(h_ref[...], wg_ref[...])
        gates_ref[...] = 1.0 / (1.0 + jnp.exp(-gl))

    @pl.when(jnp.logical_and(j == 0, i + 1 < pl.num_programs(0)))
    def _():
        x_copy(i + 1, 1 - slot).start()

    acc = _dot(h_ref[...], w_ref[...])
    proj_ref[...] = acc.astype(BF16)

    @pl.when(j == COL_KS // tn)
    def _():
        nb = seq // SLC_LEN
        xw = NSA_DK + nb
        pos = lax.rem(i * tm + lax.broadcasted_iota(jnp.int32, (tm, nb), 0), seq)
        blk = lax.shift_right_logical(pos, int(math.log2(SLC_LEN)))
        onehot = jnp.where(blk == lax.broadcasted_iota(jnp.int32, (tm, nb), 1), 1.0, 0.0)
        for g in range(NSA_KV):
            ksx_ref[:, g * xw:g * xw + NSA_DK] = acc[:, g * NSA_DK:(g + 1) * NSA_DK].astype(BF16)
            ksx_ref[:, g * xw + NSA_DK:(g + 1) * xw] = onehot.astype(BF16)

    @pl.when(j == COL_KC // tn)
    def _():
        nr = tm // CMP_STRIDE
        for t in range(2):
            for g in range(NSA_KV):
                col = (t * NSA_KV + g) * NSA_DK
                stage_ref[...] = acc[:, col:col + NSA_DK]
                for p in range(CMP_STRIDE):
                    r16_ref[t, g, :, p * NSA_DK:(p + 1) * NSA_DK] = (
                        stage_ref[pl.ds(p, nr, stride=CMP_STRIDE), :].astype(BF16))


def _inproj(xf, g, w_main, w_gate, tm, tn, seq):
    m = xf.shape[0]
    assert COL_KS % tn == 0 and COL_KC % tn == 0 and tn == 2 * NSA_KV * NSA_DK
    xw = NSA_DK + seq // SLC_LEN
    ng = w_gate.shape[1]
    return pl.pallas_call(
        functools.partial(_inproj_kernel, seq=seq),
        grid=(m // tm, MAIN_COLS // tn),
        in_specs=[
            pl.BlockSpec(memory_space=pl.ANY),
            pl.BlockSpec((1, D_MODEL), lambda i, j: (0, 0)),
            pl.BlockSpec((D_MODEL, tn), lambda i, j: (0, j)),
            pl.BlockSpec((D_MODEL, ng), lambda i, j: (0, 0)),
        ],
        out_specs=[
            pl.BlockSpec((tm, tn), lambda i, j: (i, j)),
            pl.BlockSpec((tm, ng), lambda i, j: (i, 0)),
            pl.BlockSpec((tm, NSA_KV * xw), lambda i, j: (i, 0)),
            pl.BlockSpec((2, NSA_KV, tm // CMP_STRIDE, CMP_STRIDE * NSA_DK),
                         lambda i, j: (0, 0, i, 0)),
        ],
        out_shape=[
            jax.ShapeDtypeStruct((m, MAIN_COLS), BF16),
            jax.ShapeDtypeStruct((m, ng), F32),
            jax.ShapeDtypeStruct((m, NSA_KV * xw), BF16),
            jax.ShapeDtypeStruct((2, NSA_KV, m // CMP_STRIDE, CMP_STRIDE * NSA_DK), BF16),
        ],
        scratch_shapes=[pltpu.VMEM((tm, D_MODEL), BF16), pltpu.VMEM((tm, NSA_DK), F32),
                        pltpu.VMEM((2, tm, D_MODEL), F32), pltpu.SemaphoreType.DMA((2,))],
        compiler_params=_cparams(("arbitrary", "arbitrary")),
        name="inproj",
    )(xf, g, w_main, w_gate)


def _diff_kernel(q_ref, k_ref, v_ref, t_ref, lq1_ref, lk1_ref, lq2_ref, lk2_ref, sub_ref,
                 o_ref, qs_ref, m_ref, l_ref, acc_ref, *, lam_init):
    tq = DIFF_TQ
    nq = q_ref.shape[1] // tq
    lam = (jnp.exp(jnp.sum(lq1_ref[...] * lk1_ref[...], axis=1, keepdims=True))
           - jnp.exp(jnp.sum(lq2_ref[...] * lk2_ref[...], axis=1, keepdims=True)) + lam_init)

    def tile_rows(i):
        start = i * tq
        return pl.ds(start if isinstance(i, int) else pl.multiple_of(start, tq), tq)

    def begin(i):
        q = q_ref[0, tile_rows(i), :].astype(F32) * (DIFF_D ** -0.5 * LOG2E)
        lane = lax.broadcasted_iota(jnp.int32, q.shape, 1)
        qs_ref[...] = jnp.concatenate(
            [jnp.where(lane < DIFF_D, q, 0.0), jnp.where(lane >= DIFF_D, q, 0.0)],
            axis=0).astype(BF16)
        m_ref[...] = jnp.full(m_ref.shape, NEG, F32)
        l_ref[...] = jnp.zeros(l_ref.shape, F32)
        acc_ref[...] = jnp.zeros(acc_ref.shape, F32)

    def finish(i):
        o = acc_ref[...] / jnp.maximum(l_ref[...], TINY)
        a = o[0:tq] - lam * o[tq:2 * tq]
        o_ref[0, tile_rows(i), :] = (_rms(a, sub_ref[...]) * (1.0 - lam_init)).astype(BF16)

    def flash(i):
        _diff_flash_tile(i, k_ref, v_ref, t_ref, qs_ref, m_ref, l_ref, acc_ref)

    begin(0)
    flash(0)

    def query_tile(i, carry):
        finish(i - 1)
        begin(i)
        flash(i)
        return carry

    lax.fori_loop(1, nq, query_tile, 0)
    finish(nq - 1)


def _diff_flash_tile(i, k_ref, v_ref, t_ref, qs_ref, m_ref, l_ref, acc_ref):
    tq = DIFF_TQ

    def step(kstart, width, t_col):
        def bias(c):
            if t_col is None or t_col + c < 0:
                return None
            t = t_ref[0, :, t_col + c:t_col + c + FLASH_SUB]
            return jnp.concatenate([t, t], axis=0)

        k = k_ref[0, pl.ds(kstart, width), :]
        s = _dot_nt(qs_ref[...], k)
        _online_softmax_update(s, v_ref[0, pl.ds(kstart, width), :], m_ref, l_ref, acc_ref, bias)

    if isinstance(i, int) and i == 0:
        step(0, tq, tq)
        return

    nfar = i - 1
    nwide = nfar // FAR_WIDE

    def far_body(j, carry):
        step(pl.multiple_of(j * (FAR_WIDE * tq), FAR_WIDE * tq), FAR_WIDE * tq, None)
        return carry

    lax.fori_loop(0, nwide, far_body, 0)
    left = nfar - nwide * FAR_WIDE
    odd = lax.rem(left, 2) == 1

    if FAR_WIDE > 2:
        @pl.when(left >= 2)
        def _():
            step(pl.multiple_of(nwide * (FAR_WIDE * tq), 2 * tq), 2 * tq, None)

    @pl.when(odd)
    def _():
        step(pl.multiple_of((i - 2) * tq, tq), 3 * tq, -tq)

    @pl.when(jnp.logical_not(odd))
    def _():
        step(pl.multiple_of((i - 1) * tq, tq), 2 * tq, 0)


def _diff_attention(proj3, t_diff, lq1, lk1, lq2, lk2, subln, lam_init):
    b, s, _ = proj3.shape
    tq = DIFF_TQ
    vec = lambda n: pl.BlockSpec((1, n), lambda bi, h: (0, 0))
    return pl.pallas_call(
        functools.partial(_diff_kernel, lam_init=lam_init),
        grid=(b, DIFF_HEADS),
        in_specs=[
            pl.BlockSpec((1, s, LANES), lambda bi, h: (bi, 0, COL_DQ // LANES + h)),
            pl.BlockSpec((1, s, LANES), lambda bi, h: (bi, 0, COL_DK // LANES + h)),
            pl.BlockSpec((1, s, LANES), lambda bi, h: (bi, 0, COL_DV // LANES + h)),
            pl.BlockSpec((1, tq, 2 * tq), lambda bi, h: (h, 0, 0)),
            vec(DIFF_D), vec(DIFF_D), vec(DIFF_D), vec(DIFF_D), vec(2 * DIFF_D),
        ],
        out_specs=pl.BlockSpec((1, s, LANES), lambda bi, h: (bi, 0, h)),
        out_shape=jax.ShapeDtypeStruct((b, s, DIFF_HEADS * 2 * DIFF_D), BF16),
        scratch_shapes=[
            pltpu.VMEM((2 * tq, LANES), BF16),
            pltpu.VMEM((2 * tq, LANES), F32),
            pltpu.VMEM((2 * tq, LANES), F32),
            pltpu.VMEM((2 * tq, LANES), F32),
        ],
        compiler_params=_cparams(("arbitrary", "arbitrary")),
        name="diff_attn",
    )(proj3, proj3, proj3, t_diff, lq1, lk1, lq2, lk2, subln)


def _compress_kernel(r_ref, pos_ref, w1_ref, w2_ref, o_ref):
    half = CMP_STRIDE * NSA_DK
    r = r_ref[0, 0, 0]
    n = r.shape[0]
    a = _dot(r, w1_ref[0, 0:half, :])
    bm = _dot(r, w1_ref[0, half:2 * half, :])
    posb = jnp.broadcast_to(pos_ref[0], (SUBLANES, 2 * half)).astype(BF16)
    pt = _dot(posb, w1_ref[0])[0:1]
    hid = a + pltpu.roll(bm, n - 1, axis=0) + pt
    o_ref[0, 0, 0] = _dot(_gelu_tanh(hid).astype(BF16), w2_ref[0]).astype(BF16)


def _compress(r, pos, w1, w2):
    _, g, b, n, width = r.shape
    return pl.pallas_call(
        _compress_kernel,
        grid=(2, b, g),
        in_specs=[
            pl.BlockSpec((1, 1, 1, n, width), lambda t, bi, gi: (t, gi, bi, 0, 0)),
            pl.BlockSpec((1, 1, CMP_LEN * NSA_DK), lambda t, bi, gi: (t, 0, 0)),
            pl.BlockSpec((1, CMP_LEN * NSA_DK, CMP_HIDDEN), lambda t, bi, gi: (t, 0, 0)),
            pl.BlockSpec((1, CMP_HIDDEN, NSA_DK), lambda t, bi, gi: (t, 0, 0)),
        ],
        out_specs=pl.BlockSpec((1, 1, 1, n, NSA_DK), lambda t, bi, gi: (t, bi, gi, 0, 0)),
        out_shape=jax.ShapeDtypeStruct((2, b, g, n, NSA_DK), BF16),
        compiler_params=_cparams(("arbitrary", "arbitrary", "arbitrary")),
        name="nsa_compress",
    )(r, pos, w1, w2)


def _nsa_kernel(q_ref, kc_ref, vc_ref, ks_ref, vs_ref, kw_ref, vw_ref, ts_ref, tw_ref, gate_ref,
                selmap_ref, o_ref, qx_ref, m_ref, l_ref, acc_ref, part_ref):
    tq, tk, hg = NSA_TQ, NSA_TK, NSA_GROUP
    rows = hg * tq
    first_tile = pl.program_id(2) * NSA_QPS

    def gcol(c, trows):
        return jnp.concatenate([gate_ref[0, trows, 3 * h + c:3 * h + c + 1] for h in range(hg)],
                               axis=0)

    def before_loop(t, nc):
        q0 = (first_tile + t) * tq
        trows = slice(t * tq, (t + 1) * tq)
        qt = q_ref[0, trows, :]
        q4 = jnp.concatenate([qt[:, h * NSA_DK:(h + 1) * NSA_DK] for h in range(hg)], axis=0)
        q4 = (q4.astype(F32) * (NSA_DK ** -0.5 * LOG2E)).astype(BF16)

        kc = kc_ref[0, 0, 0, 0:nc, :]
        cend = lax.broadcasted_iota(jnp.int32, (tq, nc), 1) * CMP_STRIDE + (CMP_LEN - 1)
        hidden = jnp.where(cend <= q0 + lax.broadcasted_iota(jnp.int32, (tq, nc), 0), 0.0, NEG)
        sc = _dot_nt(q4, kc) + jnp.concatenate([hidden] * hg, axis=0)
        mc = jnp.max(sc, axis=1, keepdims=True)
        pc = jnp.exp2(sc - mc)
        norm = jnp.where(mc > 0.5 * NEG,
                         1.0 / jnp.maximum(jnp.sum(pc, axis=1, keepdims=True), TINY), 0.0)
        pc = pc * norm
        o_cmp = _dot(pc.astype(BF16), vc_ref[0, 0, 0, 0:nc, :])

        pcsum = pc[0:tq] + pc[tq:2 * tq] + pc[2 * tq:3 * tq] + pc[3 * tq:4 * tq]
        p_hi = pcsum.astype(BF16)
        p_res = pcsum - p_hi.astype(F32)
        p_mid = p_res.astype(BF16)
        p_lo = (p_res - p_mid.astype(F32)).astype(BF16)
        selmap = selmap_ref[:, 0:nc]
        imp_t = _dot_nt(selmap, p_hi) + _dot_nt(selmap, p_mid) + _dot_nt(selmap, p_lo)
        nb = imp_t.shape[0]
        blk = lax.broadcasted_iota(jnp.int32, (nb, tq), 0)
        cur = lax.shift_right_logical(q0 + lax.broadcasted_iota(jnp.int32, (nb, tq), 1),
                                      int(math.log2(SLC_LEN)))
        forced = (blk == 0) | (blk == cur) | (blk == cur - 1)
        excluded = -3.0e38
        val = jnp.where(forced | (blk > cur), excluded, imp_t)
        for _ in range(min(SLC_TOPK, nb) - 3):
            best = jnp.max(val, axis=0, keepdims=True)
            first = jnp.min(jnp.where(val == best, blk, nb), axis=0, keepdims=True)
            val = jnp.where(blk == first, excluded, val)
        maskbias = jnp.where(val.T < 0.5 * excluded, 0.0, NEG).astype(BF16)
        qx_ref[t, :, 0:NSA_DK] = q4
        qx_ref[t, :, NSA_DK:NSA_DK + nb] = jnp.concatenate([maskbias] * hg, axis=0)

        kst = pl.multiple_of(jnp.maximum(q0 - WINDOW, 0), tq)
        w0 = pl.multiple_of(WINDOW - (q0 - kst), LANES)
        lw = _dot_nt(q4, kw_ref[0, pl.ds(kst, NSA_WK), :]) + tw_ref[0, :, pl.ds(w0, NSA_WK)]
        pw = jnp.exp2((lw - jnp.max(lw, axis=1, keepdims=True)).astype(BF16))
        vw1 = jnp.concatenate([vw_ref[0, pl.ds(kst, NSA_WK), :], jnp.ones((NSA_WK, LANES), BF16)],
                              axis=1)
        o_win = _dot(pw, vw1)
        o_win = o_win[:, 0:NSA_DV] / jnp.maximum(o_win[:, NSA_DV:], TINY)

        part_ref[t] = gcol(0, trows) * o_cmp + gcol(2, trows) * o_win

    nc_all = kc_ref.shape[3]
    early = (first_tile + NSA_QPS) * tq <= (nc_all // 2) * CMP_STRIDE

    @pl.when(early)
    def _():
        for t in range(NSA_QPS):
            before_loop(t, nc_all // 2)

    @pl.when(jnp.logical_not(early))
    def _():
        for t in range(NSA_QPS):
            before_loop(t, nc_all)

    def flash_tile(t, carry):
        _nsa_flash_tile(t, (first_tile + t) * tq, gcol, ks_ref, vs_ref, ts_ref, o_ref, qx_ref, m_ref,
                        l_ref, acc_ref, part_ref)
        return carry

    lax.fori_loop(0, NSA_QPS, flash_tile, 0)


def _nsa_flash_tile(t, q0, gcol, ks_ref, vs_ref, ts_ref, o_ref, qx_ref, m_ref, l_ref, acc_ref,
                    part_ref):
    tq, tk, hg = NSA_TQ, NSA_TK, NSA_GROUP
    trows = pl.ds(pl.multiple_of(t * tq, tq), tq)
    m_ref[...] = jnp.full(m_ref.shape, NEG, F32)
    l_ref[...] = jnp.zeros(l_ref.shape, F32)
    acc_ref[...] = jnp.zeros(acc_ref.shape, F32)

    def step(j, width, first_near):
        kstart = pl.multiple_of(j * tk, tk)

        def bias(c):
            if first_near is None or c < first_near * tk:
                return None
            u0 = pl.multiple_of(TS_OFF - (q0 - j * tk) + c, LANES)
            return ts_ref[0, :, pl.ds(u0, FLASH_SUB)]

        s = _dot_nt(qx_ref[t], ks_ref[0, pl.ds(kstart, width), :])
        _online_softmax_update(s, vs_ref[0, pl.ds(kstart, width), :], m_ref, l_ref, acc_ref, bias)

    jl = lax.shift_right_logical(q0 + tq - 1, int(math.log2(tk)))
    nfar = jnp.maximum(jl - 1, 0)
    nwide = nfar // FAR_WIDE

    def far_body(j, carry):
        step(FAR_WIDE * j, FAR_WIDE * tk, None)
        return carry

    lax.fori_loop(0, nwide, far_body, 0)
    left = nfar - nwide * FAR_WIDE
    odd = lax.rem(left, 2) == 1

    if FAR_WIDE > 2:
        @pl.when(left >= 2)
        def _():
            step(nwide * FAR_WIDE, 2 * tk, None)

    @pl.when(odd)
    def _():
        step(jl - 2, 3 * tk, 1)

    @pl.when(jnp.logical_and(jl > 0, jnp.logical_not(odd)))
    def _():
        step(jl - 1, 2 * tk, 0)

    @pl.when(jl == 0)
    def _():
        step(0, tk, 0)
    o = part_ref[t] + gcol(1, trows) * (acc_ref[...] / jnp.maximum(l_ref[...], TINY))
    for h in range(hg):
        o_ref[0, trows, h * NSA_DV:(h + 1) * NSA_DV] = o[h * tq:(h + 1) * tq].astype(BF16)


def _nsa_attention(proj3, kvc, ksx3, t_sel, t_win, gates3, selmap):
    b, s, _ = proj3.shape
    tq, hg = NSA_TQ, NSA_GROUP
    nc = kvc.shape[3]
    nb = s // SLC_LEN
    qcols = hg * NSA_DK
    return pl.pallas_call(
        _nsa_kernel,
        grid=(b, NSA_KV, s // (NSA_QPS * tq)),
        in_specs=[
            pl.BlockSpec((1, NSA_QPS * tq, qcols), lambda bi, g, i: (bi, i, COL_NQ // qcols + g)),
            pl.BlockSpec((1, 1, 1, nc, NSA_DK), lambda bi, g, i: (0, bi, g, 0, 0)),
            pl.BlockSpec((1, 1, 1, nc, NSA_DV), lambda bi, g, i: (1, bi, g, 0, 0)),
            pl.BlockSpec((1, s, NSA_DK + nb), lambda bi, g, i: (bi, 0, g)),
            pl.BlockSpec((1, s, NSA_DV), lambda bi, g, i: (bi, 0, COL_VS // NSA_DV + g)),
            pl.BlockSpec((1, s, NSA_DK), lambda bi, g, i: (bi, 0, COL_KW // NSA_DK + g)),
            pl.BlockSpec((1, s, NSA_DV), lambda bi, g, i: (bi, 0, COL_VW // NSA_DV + g)),
            pl.BlockSpec((1, hg * tq, TS_W), lambda bi, g, i: (g, 0, 0),
                         pipeline_mode=pl.Buffered(1)),
            pl.BlockSpec((1, hg * tq, TW_W), lambda bi, g, i: (g, 0, 0),
                         pipeline_mode=pl.Buffered(1)),
            pl.BlockSpec((1, NSA_QPS * tq, LANES), lambda bi, g, i: (bi, i, g)),
            pl.BlockSpec((nb, nc), lambda bi, g, i: (0, 0)),
        ],
        out_specs=pl.BlockSpec((1, NSA_QPS * tq, hg * NSA_DV), lambda bi, g, i: (bi, i, g)),
        out_shape=jax.ShapeDtypeStruct((b, s, NSA_HEADS * NSA_DV), BF16),
        scratch_shapes=[
            pltpu.VMEM((NSA_QPS, hg * tq, NSA_DK + nb), BF16),
            pltpu.VMEM((hg * tq, LANES), F32),
            pltpu.VMEM((hg * tq, LANES), F32),
            pltpu.VMEM((hg * tq, NSA_DV), F32),
            pltpu.VMEM((NSA_QPS, hg * tq, NSA_DV), F32),
        ],
        compiler_params=_cparams(("arbitrary", "arbitrary", "arbitrary")),
        name="nsa_attn",
    )(proj3, kvc, kvc, ksx3, proj3, proj3, proj3, t_sel, t_win, gates3, selmap)


def _outproj_kernel(od_ref, on_ref, x_ref, wd_ref, wn_ref, g_ref, o_ref):
    y = _dot(od_ref[...], wd_ref[...]) + _dot(on_ref[...], wn_ref[...])
    o_ref[...] = x_ref[...] + _rms(y, g_ref[...])


def _outproj(od, on, xf, w_d, w_n, g, tm):
    m = xf.shape[0]
    kd, kn = od.shape[1], on.shape[1]
    return pl.pallas_call(
        _outproj_kernel,
        grid=(m // tm,),
        in_specs=[
            pl.BlockSpec((tm, kd), lambda i: (i, 0)),
            pl.BlockSpec((tm, kn), lambda i: (i, 0)),
            pl.BlockSpec((tm, D_MODEL), lambda i: (i, 0)),
            pl.BlockSpec((kd, D_MODEL), lambda i: (0, 0)),
            pl.BlockSpec((kn, D_MODEL), lambda i: (0, 0)),
            pl.BlockSpec((1, D_MODEL), lambda i: (0, 0)),
        ],
        out_specs=pl.BlockSpec((tm, D_MODEL), lambda i: (i, 0)),
        out_shape=jax.ShapeDtypeStruct((m, D_MODEL), F32),
        compiler_params=_cparams(("arbitrary",)),
        name="outproj",
    )(od, on, xf, w_d, w_n, g)


def _ffn_kernel(x_ref, gpre_ref, wg_ref, wu_ref, cwg_ref, cwu_ref, cbg_ref, cbu_ref, wd_ref,
                gpost_ref, o_ref, h_ref, carry_ref, yg_ref, yu_ref, *, tiles_per_seq):
    i = pl.program_id(0)
    j = pl.program_id(1)
    tm = x_ref.shape[0]
    halo = SUBLANES

    @pl.when(j == 0)
    def _():
        h_ref[...] = _rms(x_ref[...], gpre_ref[...]).astype(BF16)
        o_ref[...] = jnp.zeros(o_ref.shape, F32)

    @pl.when(lax.rem(i, tiles_per_seq) == 0)
    def _():
        carry_ref[j] = jnp.zeros(carry_ref.shape[1:], F32)

    def conv(w_ref, cw_ref, cb_ref, slot, y_ref):
        u = _dot(h_ref[...], w_ref[...])
        prev = carry_ref[j, slot]
        carry_ref[j, slot] = u[tm - halo:tm]
        w0, w1, w2 = cw_ref[0:1, :], cw_ref[1:2, :], cw_ref[2:3, :]
        bias = cb_ref[...]

        def taps(z):
            return bias + w2 * z + w1 * pltpu.roll(z, 1, axis=0) + w0 * pltpu.roll(z, 2, axis=0)

        y_ref[...] = taps(u)
        y_ref[0:halo, :] = taps(jnp.concatenate([prev, u[0:halo]], axis=0))[halo:2 * halo]

    conv(wg_ref, cwg_ref, cbg_ref, 0, yg_ref)
    conv(wu_ref, cwu_ref, cbu_ref, 1, yu_ref)
    act = (_gelu_tanh(yg_ref[...]) * yu_ref[...]).astype(BF16)
    o_ref[...] += _dot(act, wd_ref[...])

    @pl.when(j == pl.num_programs(1) - 1)
    def _():
        o_ref[...] = x_ref[...] + _rms(o_ref[...], gpost_ref[...])


def _ffn(x1, gpre, w_up, conv_w, conv_b, w_down, gpost, tm, tf, seq):
    m = x1.shape[0]
    nj = D_FF // tf
    return pl.pallas_call(
        functools.partial(_ffn_kernel, tiles_per_seq=seq // tm),
        grid=(m // tm, nj),
        in_specs=[
            pl.BlockSpec((tm, D_MODEL), lambda i, j: (i, 0)),
            pl.BlockSpec((1, D_MODEL), lambda i, j: (0, 0)),
            pl.BlockSpec((D_MODEL, tf), lambda i, j: (0, j)),
            pl.BlockSpec((D_MODEL, tf), lambda i, j: (0, j + nj)),
            pl.BlockSpec((CONV_W, tf), lambda i, j: (0, j)),
            pl.BlockSpec((CONV_W, tf), lambda i, j: (0, j + nj)),
            pl.BlockSpec((1, tf), lambda i, j: (0, j)),
            pl.BlockSpec((1, tf), lambda i, j: (0, j + nj)),
            pl.BlockSpec((tf, D_MODEL), lambda i, j: (j, 0)),
            pl.BlockSpec((1, D_MODEL), lambda i, j: (0, 0)),
        ],
        out_specs=pl.BlockSpec((tm, D_MODEL), lambda i, j: (i, 0)),
        out_shape=jax.ShapeDtypeStruct((m, D_MODEL), F32),
        scratch_shapes=[
            pltpu.VMEM((tm, D_MODEL), BF16),
            pltpu.VMEM((nj, 2, SUBLANES, tf), F32),
            pltpu.VMEM((tm, tf), F32),
            pltpu.VMEM((tm, tf), F32),
        ],
        compiler_params=_cparams(("arbitrary", "arbitrary"), FFN_VMEM_LIMIT),
        name="ffn",
    )(x1, gpre, w_up, w_up, conv_w, conv_w, conv_b, conv_b, w_down, gpost)


def _rel_bucket(delta):
    n = jnp.maximum(delta, 0)
    max_exact = N_BUCKETS // 2
    nf = jnp.maximum(n, 1).astype(F32)
    large = max_exact + (jnp.log(nf / max_exact) / math.log(MAX_DISTANCE / max_exact)
                         * (N_BUCKETS - max_exact)).astype(jnp.int32)
    large = jnp.minimum(large, N_BUCKETS - 1)
    return jnp.where(n < max_exact, n, large)


def _bias_kernel(tab_ref, bkt_ref, o_ref):
    h = pl.program_id(0)
    i = pl.program_id(1)
    bkt = bkt_ref[...]
    row = jnp.full(bkt.shape, NEG, F32)
    for bucket in range(N_BUCKETS):
        row = jnp.where(bkt == bucket, tab_ref[bucket, h], row)
    tr, width = o_ref.shape[1:]
    rows = jnp.concatenate([row] * (tr // SUBLANES), axis=0)
    o_ref[0] = pltpu.roll(rows, i * tr, axis=1, stride=1, stride_axis=0)[:, 0:width]


def _bias_strip(table, rows, width, off, hi):
    heads = table.shape[1]
    tr = LANES
    wp = width + rows
    t = jnp.arange(wp)
    dist = off - jnp.where(t < width, t, t - wp)
    bkt = jnp.where((dist < 0) | (dist >= hi), -1, _rel_bucket(dist)).astype(jnp.int32)
    bkt = jnp.broadcast_to(bkt[None, :], (SUBLANES, wp))
    return pl.pallas_call(
        _bias_kernel,
        grid=(heads, rows // tr),
        in_specs=[
            pl.BlockSpec(memory_space=pltpu.SMEM),
            pl.BlockSpec((SUBLANES, wp), lambda h, i: (0, 0)),
        ],
        out_specs=pl.BlockSpec((1, tr, width), lambda h, i: (h, i, 0)),
        out_shape=jax.ShapeDtypeStruct((heads, rows, width), F32),
        compiler_params=_cparams(("arbitrary", "arbitrary")),
        name="bias_strip",
    )(table, bkt)


def _selection_map(nb, nc_pad):
    r = SLC_LEN // CMP_STRIDE
    j = jnp.arange(nb)[:, None]
    c = jnp.arange(nc_pad)[None, :]
    off = r * j - c
    out = jnp.zeros((nb, nc_pad), F32)
    for mm in range(r):
        for nn in range(CMP_LEN // CMP_STRIDE):
            out = out + (off == mm + nn).astype(F32)
    return out * (c < nc_pad - 1)


def kernel(x, pre_mix_norm, w_in, lambda_q1, lambda_k1, lambda_q2, lambda_k2, diff_subln,
           cmp_pos_k, cmp_pos_v, cmp_k_w1, cmp_k_w2, cmp_v_w1, cmp_v_w2, rel_bias, w_out,
           post_mix_norm, pre_ffn_norm, w_up, conv_w, conv_b, w_down, post_ffn_norm):
    b, s, d = x.shape
    assert d == D_MODEL and s // SLC_LEN >= SLC_TOPK
    assert s % max(DIFF_TQ, NSA_QPS * NSA_TQ, NSA_TK, INPROJ_TM, OUTPROJ_TM, FFN_TM) == 0
    m = b * s
    depth = w_in.shape[0]
    unbounded = 1 << 30
    for l in range(depth):
        xf = x.reshape(m, d)
        w_main = w_in[l].astype(BF16)
        w_gate = jnp.pad(w_in[l][:, MAIN_COLS:].reshape(d, NSA_KV, 3 * NSA_GROUP),
                         ((0, 0), (0, 0), (0, LANES - 3 * NSA_GROUP)))
        w_gate = w_gate.reshape(d, NSA_KV * LANES).astype(BF16)
        proj, gates, ksx, rows16 = _inproj(xf, pre_mix_norm[l][None], w_main, w_gate,
                                           tm=INPROJ_TM, tn=INPROJ_TN, seq=s)
        proj3 = proj.reshape(b, s, MAIN_COLS)

        bias_diff = rel_bias[:, :DIFF_HEADS]
        bias_nsa = rel_bias[:, DIFF_HEADS:]
        t_diff = _bias_strip((bias_diff - bias_diff[N_BUCKETS - 1]) * LOG2E,
                             DIFF_TQ, 2 * DIFF_TQ, DIFF_TQ, unbounded)
        t_sel = _bias_strip((bias_nsa - bias_nsa[N_BUCKETS - 1]) * LOG2E,
                            NSA_TQ, TS_W, TS_OFF, unbounded)
        t_win = _bias_strip(bias_nsa * LOG2E, NSA_TQ, TW_W, WINDOW, WINDOW)
        t_sel = t_sel.reshape(NSA_KV, NSA_GROUP * NSA_TQ, TS_W)
        t_win = t_win.reshape(NSA_KV, NSA_GROUP * NSA_TQ, TW_W)

        lam_init = 0.8 - 0.6 * math.exp(-0.3 * l)
        o_diff = _diff_attention(proj3, t_diff, lambda_q1[l][None], lambda_k1[l][None],
                                 lambda_q2[l][None], lambda_k2[l][None], diff_subln[l][None],
                                 lam_init)

        nrow = s // CMP_STRIDE
        r = rows16.reshape(2, NSA_KV, b, nrow, CMP_STRIDE * NSA_DK)
        pos = jnp.stack([cmp_pos_k[l].reshape(1, -1), cmp_pos_v[l].reshape(1, -1)])
        w1 = jnp.stack([cmp_k_w1[l], cmp_v_w1[l]]).astype(BF16)
        w2 = jnp.stack([cmp_k_w2[l], cmp_v_w2[l]]).astype(BF16)
        kvc = _compress(r, pos, w1, w2)

        nb = s // SLC_LEN
        o_nsa = _nsa_attention(proj3, kvc, ksx.reshape(b, s, -1), t_sel, t_win,
                               gates.reshape(b, s, -1), _selection_map(nb, nrow).astype(BF16))

        half = DIFF_HEADS * 2 * DIFF_D
        x1 = _outproj(o_diff.reshape(m, -1), o_nsa.reshape(m, -1), xf,
                      w_out[l][:half].astype(BF16), w_out[l][half:].astype(BF16),
                      post_mix_norm[l][None], tm=OUTPROJ_TM)
        x2 = _ffn(x1, pre_ffn_norm[l][None], w_up[l].astype(BF16), conv_w[l], conv_b[l][None],
                  w_down[l].astype(BF16), post_ffn_norm[l][None], tm=FFN_TM, tf=FFN_TF, seq=s)
        x = x2.reshape(b, s, d)
    return x
```
